```python
import math
import jax, jax.numpy as jnp
from jax import lax
import numpy as np

D_MODEL = 1024
BATCH = 4
SEQ = 4096
DEPTH = 4
DEC_BATCH = 32
DEC_SEQ = 4
PAST_LEN = 8192
PAGE_SIZE = 128

N_MEM = 256
RMS_EPS = 1e-6
N_BRANCH = 4

SSD_D_INNER = D_MODEL // 2
SSD_HEAD_DIM = 64
SSD_HEADS = SSD_D_INNER // SSD_HEAD_DIM
SSD_GROUPS = 2
SSD_STATE = 64
SSD_CONV = 4
SSD_CHUNK = 128
SSD_CONV_DIM = SSD_D_INNER + 2 * SSD_GROUPS * SSD_STATE

SWA_PATTERN = ((128, 1), (512, 4), (2048, 16))
SWA_N_GROUPS = 3
SWA_HEADS_PER_GROUP = 4
SWA_HEAD_DIM = 64
SWA_HEADS = SWA_N_GROUPS * SWA_HEADS_PER_GROUP
SWA_WIDTH = SWA_HEADS * SWA_HEAD_DIM
SWA_OUT = SWA_HEADS_PER_GROUP * SWA_HEAD_DIM

S5_WIDTH = D_MODEL // 2
S5_GROUP = 16
S5_GROUPS = S5_WIDTH // S5_GROUP
S5_STATE = 64

MLSTM_WIDTH = D_MODEL // 2
MLSTM_HEADS = 4
MLSTM_HEAD_DIM = MLSTM_WIDTH // MLSTM_HEADS
MLSTM_CHUNK = 64

XA_HEADS = 4
XA_HEAD_DIM = D_MODEL // XA_HEADS
D_FF = 4 * D_MODEL

REL_BUCKETS = 32
REL_MAX_EXACT = 16
REL_MAX_DIST = 2048

IN_SIZES = (SSD_D_INNER, SSD_CONV_DIM, SSD_HEADS, SWA_WIDTH, SWA_WIDTH, SWA_WIDTH, S5_WIDTH,
            MLSTM_WIDTH, MLSTM_WIDTH, MLSTM_WIDTH, MLSTM_WIDTH, MLSTM_HEADS, MLSTM_HEADS, N_BRANCH * D_MODEL)
D_IN = sum(IN_SIZES)

kernel_name = 'hybrid_gated_ssd_dilswa_s5_mlstm_step'


def split_cols(t, sizes):
    idx = [int(s) for s in np.cumsum(sizes)[:-1]]
    return jnp.split(t, idx, axis=-1)


def rmsnorm(x, g):
    xf = x.astype(jnp.float32)
    y = xf * lax.rsqrt(jnp.mean(xf * xf, axis=-1, keepdims=True) + RMS_EPS)
    return (y * g.astype(jnp.float32)).astype(x.dtype)


def head_rms(h, g):
    bsz, seqlen, nh, hd = h.shape
    y = h * lax.rsqrt(jnp.mean(h * h, axis=-1, keepdims=True) + RMS_EPS)
    return y.reshape(bsz, seqlen, nh * hd) * g.astype(jnp.float32)


def t5_bucket(dist):
    dist = np.asarray(dist)
    large = REL_MAX_EXACT + (np.log(np.maximum(dist, 1) / REL_MAX_EXACT)
                             / math.log(REL_MAX_DIST / REL_MAX_EXACT)
                             * (REL_BUCKETS - REL_MAX_EXACT)).astype(np.int32)
    large = np.minimum(large, REL_BUCKETS - 1)
    return np.where(dist < REL_MAX_EXACT, dist, large).astype(np.int32)


def causal_conv(u, buf, w, b):
    full = jnp.concatenate([buf.astype(u.dtype), u], axis=1)
    nc = u.shape[-1]
    out = lax.conv_general_dilated(full, w[:, None, :].astype(u.dtype), window_strides=(1,), padding='VALID',
                                   dimension_numbers=('NWC', 'WIO', 'NWC'), feature_group_count=nc)
    return out + b.astype(u.dtype), full[:, -(SSD_CONV - 1):]


def ssd_scan(x, dt, a, bm, cm, s0):
    bsz, seqlen = x.shape[:2]
    q = seqlen if seqlen < SSD_CHUNK else SSD_CHUNK
    nc = seqlen // q
    rep = SSD_HEADS // SSD_GROUPS
    xr = x.reshape(bsz, nc, q, SSD_GROUPS, rep, SSD_HEAD_DIM)
    da = (dt * a).reshape(bsz, nc, q, SSD_GROUPS, rep)
    dtr = dt.reshape(bsz, nc, q, SSD_GROUPS, rep)
    br = bm.reshape(bsz, nc, q, SSD_GROUPS, SSD_STATE)
    cr = cm.reshape(bsz, nc, q, SSD_GROUPS, SSD_STATE)
    s_init = s0.reshape(bsz, SSD_GROUPS, rep, SSD_HEAD_DIM, SSD_STATE)
    causal = jnp.tril(jnp.ones((q, q), dtype=bool))[None, :, :, None, None]

    def body(s, inp):
        xc, dac, dtc, bc, cc = inp
        cum = jnp.cumsum(dac, axis=1)
        seg = jnp.where(causal, cum[:, :, None] - cum[:, None, :], -jnp.inf)
        w = jnp.einsum('btgn,bsgn->btsg', cc, bc)[..., None] * jnp.exp(seg) * dtc[:, None]
        y = jnp.einsum('btsgr,bsgrp->btgrp', w, xc)
        y = y + jnp.exp(cum)[..., None] * jnp.einsum('btgn,bgrpn->btgrp', cc, s)
        tail = jnp.exp(cum[:, -1:] - cum) * dtc
        s = s * jnp.exp(cum[:, -1])[..., None, None] + jnp.einsum('bsgr,bsgrp,bsgn->bgrpn', tail, xc, bc)
        return s, y

    xs = tuple(jnp.moveaxis(t, 1, 0) for t in (xr, da, dtr, br, cr))
    s_fin, ys = lax.scan(body, s_init, xs)
    y = jnp.moveaxis(ys, 0, 1).reshape(bsz, seqlen, SSD_HEADS, SSD_HEAD_DIM)
    return y, s_fin.reshape(bsz, SSD_HEADS, SSD_HEAD_DIM, SSD_STATE)


def dil_attn_prompt(q, k, v, bias_steps, window, dil):
    bsz, seqlen, nh, hd = q.shape
    n = window // dil
    sd = -(-seqlen // dil)
    nb = -(-sd // n)

    def strided(t):
        t = jnp.pad(t, ((0, 0), (0, sd * dil - seqlen), (0, 0), (0, 0)))
        t = t.reshape(bsz, sd, dil, nh, hd)
        t = jnp.pad(t, ((0, 0), (0, nb * n - sd), (0, 0), (0, 0), (0, 0)))
        return t.reshape(bsz, nb, n, dil, nh, hd)

    def with_prev(t):
        prev = jnp.pad(t[:, :-1], ((0, 0), (1, 0), (0, 0), (0, 0), (0, 0), (0, 0)))
        return jnp.concatenate([prev, t], axis=2)

    qb = strided(q)
    kk = with_prev(strided(k))
    vv = with_prev(strided(v))
    i = np.arange(n)[:, None]
    j = np.arange(2 * n)[None, :]
    dist = i + n - j
    band = (dist >= 0) & (dist <= n)
    mask = band[None] & ~((np.arange(nb) == 0)[:, None, None] & (j < n)[None])
    bias = jnp.transpose(bias_steps[np.clip(dist, 0, n)], (2, 0, 1))
    logits = jnp.einsum('bcirhd,bcjrhd->bcrhij', qb, kk).astype(jnp.float32) * SWA_HEAD_DIM ** -0.5 + bias
    logits = jnp.where(mask[None, :, None, None], logits, -jnp.inf)
    m = jnp.max(logits, axis=-1, keepdims=True)
    p = jnp.exp(logits - m)
    den = jnp.sum(p, axis=-1, keepdims=True)
    o = jnp.einsum('bcrhij,bcjrhd->bcirhd', p / den, vv.astype(jnp.float32))
    lse = (m + jnp.log(den))[..., 0]
    o = o.reshape(bsz, nb * n, dil, nh, hd)[:, :sd].reshape(bsz, sd * dil, nh, hd)[:, :seqlen]
    lse = jnp.transpose(lse, (0, 1, 4, 2, 3)).reshape(bsz, nb * n, dil, nh)[:, :sd]
    lse = lse.reshape(bsz, sd * dil, nh)[:, :seqlen]
    return o, lse


def dil_attn_sample(q, k, v, buf, bias_steps, window, dil):
    t_new = q.shape[1]
    wb = buf.shape[1]
    n = window // dil
    kv = jnp.concatenate([buf.astype(k.dtype), jnp.stack([k, v], axis=2)], axis=1)
    idx = wb + np.arange(t_new)[:, None] - dil * np.arange(n + 1)[None, :]
    valid = idx >= 0
    g = kv[:, np.maximum(idx, 0)]
    logits = jnp.einsum('bthd,btkhd->bhtk', q, g[:, :, :, 0]).astype(jnp.float32) * SWA_HEAD_DIM ** -0.5
    logits = logits + jnp.transpose(bias_steps)[None, :, None, :]
    logits = jnp.where(valid[None, None], logits, -jnp.inf)
    m = jnp.max(logits, axis=-1, keepdims=True)
    p = jnp.exp(logits - m)
    den = jnp.sum(p, axis=-1, keepdims=True)
    o = jnp.einsum('bhtk,btkhd->bthd', p / den, g[:, :, :, 1].astype(jnp.float32))
    lse = jnp.transpose((m + jnp.log(den))[..., 0], (0, 2, 1))
    return o, lse, kv[:, -wb:]


def _complex_affine(e1, e2):
    a1r, a1i, b1r, b1i = e1
    a2r, a2i, b2r, b2i = e2
    return (a2r * a1r - a2i * a1i, a2r * a1i + a2i * a1r,
            a2r * b1r - a2i * b1i + b2r, a2r * b1i + a2i * b1r + b2i)


def s5_scan(u, a_re, a_im, log_dt, b_re, b_im, c_re, c_im, d_skip, h0):
    f32 = jnp.float32
    bsz, seqlen, _ = u.shape
    uf = u.astype(f32)
    ug = uf.reshape(bsz, seqlen, S5_GROUPS, S5_GROUP)
    a_re = a_re.astype(f32)
    a_im = a_im.astype(f32)
    dt = jnp.exp(log_dt.astype(f32))[:, None]
    mag = jnp.exp(a_re * dt)
    ab_re = mag * jnp.cos(a_im * dt)
    ab_im = mag * jnp.sin(a_im * dt)
    inv = 1.0 / (a_re * a_re + a_im * a_im)
    co_re = ((ab_re - 1.0) * a_re + ab_im * a_im) * inv
    co_im = (ab_im * a_re - (ab_re - 1.0) * a_im) * inv
    bu_re = jnp.einsum('blgc,gnc->blgn', ug, b_re.astype(f32))
    bu_im = jnp.einsum('blgc,gnc->blgn', ug, b_im.astype(f32))
    x_re = co_re * bu_re - co_im * bu_im
    x_im = co_re * bu_im + co_im * bu_re
    h0_re = h0[..., 0].astype(f32)
    h0_im = h0[..., 1].astype(f32)
    x_re = x_re.at[:, 0].add(ab_re * h0_re - ab_im * h0_im)
    x_im = x_im.at[:, 0].add(ab_re * h0_im + ab_im * h0_re)
    shape = x_re.shape
    _, _, h_re, h_im = lax.associative_scan(
        _complex_affine, (jnp.broadcast_to(ab_re, shape), jnp.broadcast_to(ab_im, shape), x_re, x_im), axis=1)
    y = (jnp.einsum('blgn,gcn->blgc', h_re, c_re.astype(f32))
         - jnp.einsum('blgn,gcn->blgc', h_im, c_im.astype(f32)))
    y = y.reshape(bsz, seqlen, S5_WIDTH) + d_skip.astype(f32) * uf
    return y, jnp.stack([h_re[:, -1], h_im[:, -1]], axis=-1)


def mlstm_scan(q, k, v, ig, fg, c0, n0, m0):
    bsz, seqlen = q.shape[:2]
    ql = seqlen if seqlen < MLSTM_CHUNK else MLSTM_CHUNK
    nc = seqlen // ql
    k = k * MLSTM_HEAD_DIM ** -0.5
    logf = jax.nn.log_sigmoid(fg)

    def chunks(t):
        return jnp.moveaxis(t.reshape(bsz, nc, ql, *t.shape[2:]), 1, 0)

    causal = jnp.tril(jnp.ones((ql, ql), dtype=bool))[None, :, :, None]

    def body(carry, inp):
        c, n, m = carry
        qc, kc, vc, ic, lfc = inp
        b = jnp.cumsum(lfc, axis=1)
        intra = jnp.where(causal, b[:, :, None] - b[:, None, :] + ic[:, None], -jnp.inf)
        inter = b + m[:, None]
        m_t = jnp.maximum(inter, jnp.max(intra, axis=2))
        w = jnp.einsum('bthd,bshd->btsh', qc, kc) * jnp.exp(intra - m_t[:, :, None])
        w_inter = jnp.exp(inter - m_t)
        num = jnp.einsum('btsh,bshd->bthd', w, vc) + w_inter[..., None] * jnp.einsum('bthk,bhkv->bthv', qc, c)
        den = jnp.sum(w, axis=2) + w_inter * jnp.einsum('bthk,bhk->bth', qc, n)
        h = num / jnp.maximum(jnp.abs(den), jnp.exp(-m_t))[..., None]
        m_new = m_t[:, -1]
        wk = jnp.exp(b[:, -1:] - b + ic - m_new[:, None])
        decay = jnp.exp(b[:, -1] + m - m_new)
        c = decay[..., None, None] * c + jnp.einsum('bsh,bshk,bshv->bhkv', wk, kc, vc)
        n = decay[..., None] * n + jnp.einsum('bsh,bshk->bhk', wk, kc)
        return (c, n, m_new), h

    (c, n, m), hs = lax.scan(body, (c0, n0, m0), tuple(chunks(t) for t in (q, k, v, ig, logf)))
    return jnp.moveaxis(hs, 0, 1).reshape(bsz, seqlen, MLSTM_HEADS, MLSTM_HEAD_DIM), c, n, m


def mem_kv(mem, g, wk, wv):
    bsz, nm, _ = mem.shape
    mn = rmsnorm(mem, g)
    k = (mn @ wk).reshape(bsz, nm, XA_HEADS, XA_HEAD_DIM)
    v = (mn @ wv).reshape(bsz, nm, XA_HEADS, XA_HEAD_DIM)
    return jnp.stack([k, v], axis=2)


def cross_attn(h, kv, wq, wo):
    bsz, seqlen, _ = h.shape
    q = (h @ wq).reshape(bsz, seqlen, XA_HEADS, XA_HEAD_DIM)
    logits = jnp.einsum('blhd,bmhd->bhlm', q, kv[:, :, 0]).astype(jnp.float32) * XA_HEAD_DIM ** -0.5
    p = jax.nn.softmax(logits, axis=-1)
    o = jnp.einsum('bhlm,bmhd->blhd', p, kv[:, :, 1].astype(jnp.float32))
    return o.reshape(bsz, seqlen, D_MODEL).astype(h.dtype) @ wo


def zero_state(bsz):
    f32 = jnp.float32
    return {'ssd': jnp.zeros((bsz, SSD_HEADS, SSD_HEAD_DIM, SSD_STATE), f32),
            'ssd_conv': jnp.zeros((bsz, SSD_CONV - 1, SSD_CONV_DIM), f32),
            'swa': None,
            's5': jnp.zeros((bsz, S5_GROUPS, S5_STATE, 2), f32),
            'mlstm_c': jnp.zeros((bsz, MLSTM_HEADS, MLSTM_HEAD_DIM, MLSTM_HEAD_DIM), f32),
            'mlstm_n': jnp.zeros((bsz, MLSTM_HEADS, MLSTM_HEAD_DIM), f32),
            'mlstm_m': jnp.zeros((bsz, MLSTM_HEADS), f32)}


def layer_forward(x, kv_mem, st, lw, rel_bias, is_prompt):
    f32 = jnp.float32
    bsz, seqlen, _ = x.shape
    h = rmsnorm(x, lw['norm_mix'])
    (z, xbc, dt_raw, sq, sk, sv, s5u, mq, mk, mv, mo, mi, mf, gate_pre) = split_cols(h @ lw['w_in'], IN_SIZES)
    new = {}

    xbc, new['ssd_conv'] = causal_conv(xbc, st['ssd_conv'], lw['ssd_conv_w'], lw['ssd_conv_b'])
    xbc = jax.nn.silu(xbc.astype(f32))
    xa_in, b_in, c_in = split_cols(xbc, (SSD_D_INNER, SSD_GROUPS * SSD_STATE, SSD_GROUPS * SSD_STATE))
    dt = jax.nn.softplus(dt_raw.astype(f32) + lw['ssd_dt_bias'].astype(f32))
    a = -jnp.exp(lw['ssd_a_log'].astype(f32))
    xh = xa_in.reshape(bsz, seqlen, SSD_HEADS, SSD_HEAD_DIM)
    y, new['ssd'] = ssd_scan(xh, dt, a, b_in.reshape(bsz, seqlen, SSD_GROUPS, SSD_STATE),
                             c_in.reshape(bsz, seqlen, SSD_GROUPS, SSD_STATE), st['ssd'])
    y = (y + lw['ssd_d'].astype(f32)[:, None] * xh).reshape(bsz, seqlen, SSD_D_INNER) * jax.nn.silu(z.astype(f32))
    y_ssd = rmsnorm(y, lw['ssd_norm'])

    qh = sq.reshape(bsz, seqlen, SWA_HEADS, SWA_HEAD_DIM)
    kh = sk.reshape(bsz, seqlen, SWA_HEADS, SWA_HEAD_DIM)
    vh = sv.reshape(bsz, seqlen, SWA_HEADS, SWA_HEAD_DIM)
    outs, lses = [], []
    for g, (win, dil) in enumerate(SWA_PATTERN):
        hs = slice(g * SWA_HEADS_PER_GROUP, (g + 1) * SWA_HEADS_PER_GROUP)
        bias_steps = rel_bias[t5_bucket(dil * np.arange(win // dil + 1))][:, hs].astype(f32)
        if is_prompt:
            o, lse = dil_attn_prompt(qh[:, :, hs], kh[:, :, hs], vh[:, :, hs], bias_steps, win, dil)
            buf = jnp.stack([kh[:, :, hs], vh[:, :, hs]], axis=2)[:, -min(win, seqlen):]
        else:
            o, lse, buf = dil_attn_sample(qh[:, :, hs], kh[:, :, hs], vh[:, :, hs], st['swa'][g], bias_steps, win, dil)
        outs.append(o)
        lses.append(lse)
        new['swa%d' % g] = buf
    wts = jax.nn.softmax(jnp.stack(lses), axis=0)
    y_swa = jnp.sum(wts[..., None] * jnp.stack(outs), axis=0).reshape(bsz, seqlen, SWA_OUT)

    y, new['s5'] = s5_scan(s5u, lw['s5_a_re'], lw['s5_a_im'], lw['s5_log_dt'], lw['s5_b_re'], lw['s5_b_im'],
                           lw['s5_c_re'], lw['s5_c_im'], lw['s5_d'], st['s5'])
    y_s5 = y * jax.nn.sigmoid(y @ lw['s5_w_glu'].astype(f32))

    qm = mq.reshape(bsz, seqlen, MLSTM_HEADS, MLSTM_HEAD_DIM).astype(f32)
    km = mk.reshape(bsz, seqlen, MLSTM_HEADS, MLSTM_HEAD_DIM).astype(f32)
    vm = mv.reshape(bsz, seqlen, MLSTM_HEADS, MLSTM_HEAD_DIM).astype(f32)
    ig = mi.astype(f32) + lw['mlstm_i_bias'].astype(f32)
    fg = mf.astype(f32) + lw['mlstm_f_bias'].astype(f32)
    hc, new['mlstm_c'], new['mlstm_n'], new['mlstm_m'] = mlstm_scan(qm, km, vm, ig, fg, st['mlstm_c'], st['mlstm_n'], st['mlstm_m'])
    y_ml = jax.nn.sigmoid(mo.astype(f32)) * head_rms(hc, lw['mlstm_norm'])

    gates = jax.nn.sigmoid(gate_pre.astype(f32)).reshape(bsz, seqlen, N_BRANCH, D_MODEL)
    merged = (gates[:, :, 0] * (y_ssd @ lw['w_br_ssd'])
              + gates[:, :, 1] * (y_swa @ lw['w_br_swa'])
              + gates[:, :, 2] * (y_s5 @ lw['w_br_s5'])
              + gates[:, :, 3] * (y_ml @ lw['w_br_mlstm']))
    x = x + (merged.astype(x.dtype) @ lw['w_out']).astype(x.dtype)

    x = x + cross_attn(rmsnorm(x, lw['norm_xa']), kv_mem, lw['xa_wq'], lw['xa_wo']).astype(x.dtype)

    hm = rmsnorm(x, lw['norm_mlp'])
    x = x + (jnp.square(jax.nn.relu(hm @ lw['w_ff1'])) @ lw['w_ff2']).astype(x.dtype)
    return x, new


def setup_inputs(seed: int = 0) -> dict:
    key = jax.random.key(seed)
    keys = iter(jax.random.split(key, 64))

    def nrm(shape, scale):
        return scale * jax.random.normal(next(keys), shape, jnp.float32)

    def gain(shape):
        return 1.0 + nrm(shape, 0.01)

    def unif(shape, lo, hi):
        return jax.random.uniform(next(keys), shape, jnp.float32, lo, hi)

    L = DEPTH
    dt0 = jnp.exp(unif((L, SSD_HEADS), math.log(1e-3), math.log(1e-1)))
    return {
        'x_prompt': nrm((BATCH, SEQ, D_MODEL), 1.0),
        'x_sample': nrm((DEC_BATCH, DEC_SEQ, D_MODEL), 1.0),
        'state_ssd': nrm((L, DEC_BATCH, SSD_HEADS, SSD_HEAD_DIM, SSD_STATE), 0.1),
        'state_ssd_conv': nrm((L, DEC_BATCH, SSD_CONV - 1, SSD_CONV_DIM), 1.0),
        'cache_swa_w128': nrm((L, DEC_BATCH, min(SWA_PATTERN[0][0], PAST_LEN), 2, SWA_HEADS_PER_GROUP, SWA_HEAD_DIM), 1.0),
        'cache_swa_w512': nrm((L, DEC_BATCH, min(SWA_PATTERN[1][0], PAST_LEN), 2, SWA_HEADS_PER_GROUP, SWA_HEAD_DIM), 1.0),
        'cache_swa_w2048': nrm((L, DEC_BATCH, min(SWA_PATTERN[2][0], PAST_LEN), 2, SWA_HEADS_PER_GROUP, SWA_HEAD_DIM), 1.0),
        'state_s5': nrm((L, DEC_BATCH, S5_GROUPS, S5_STATE, 2), 0.1),
        'state_mlstm_c': nrm((L, DEC_BATCH, MLSTM_HEADS, MLSTM_HEAD_DIM, MLSTM_HEAD_DIM), 0.1),
        'state_mlstm_n': nrm((L, DEC_BATCH, MLSTM_HEADS, MLSTM_HEAD_DIM), 0.1),
        'state_mlstm_m': nrm((L, DEC_BATCH, MLSTM_HEADS), 1.0),
        'cache_mem_kv': nrm((L, DEC_BATCH, N_MEM, 2, XA_HEADS, XA_HEAD_DIM), 1.0),
        'mem_prompt': nrm((BATCH, N_MEM, D_MODEL), 1.0),
        'norm_mix': gain((L, D_MODEL)),
        'w_in': nrm((L, D_MODEL, D_IN), D_MODEL ** -0.5),
        'ssd_conv_w': nrm((L, SSD_CONV, SSD_CONV_DIM), SSD_CONV ** -0.5),
        'ssd_conv_b': nrm((L, SSD_CONV_DIM), 0.01),
        'ssd_dt_bias': dt0 + jnp.log(-jnp.expm1(-dt0)),
        'ssd_a_log': jnp.log(unif((L, SSD_HEADS), 1.0, 16.0)),
        'ssd_d': 1.0 + nrm((L, SSD_HEADS), 0.1),
        'ssd_norm': gain((L, SSD_D_INNER)),
        'rel_bias': nrm((REL_BUCKETS, SWA_HEADS), 0.2),
        's5_a_re': -0.5 + nrm((L, S5_GROUPS, S5_STATE), 0.01),
        's5_a_im': jnp.pi * jnp.arange(S5_STATE, dtype=jnp.float32) + nrm((L, S5_GROUPS, S5_STATE), 0.01),
        's5_log_dt': unif((L, S5_GROUPS), math.log(1e-3), math.log(1e-1)),
        's5_b_re': nrm((L, S5_GROUPS, S5_STATE, S5_GROUP), (2 * S5_GROUP) ** -0.5),
        's5_b_im': nrm((L, S5_GROUPS, S5_STATE, S5_GROUP), (2 * S5_GROUP) ** -0.5),
        's5_c_re': nrm((L, S5_GROUPS, S5_GROUP, S5_STATE), S5_STATE ** -0.5),
        's5_c_im': nrm((L, S5_GROUPS, S5_GROUP, S5_STATE), S5_STATE ** -0.5),
        's5_d': nrm((L, S5_WIDTH), 1.0),
        's5_w_glu': nrm((L, S5_WIDTH, S5_WIDTH), S5_WIDTH ** -0.5),
        'mlstm_i_bias': nrm((L, MLSTM_HEADS), 0.1),
        'mlstm_f_bias': jnp.linspace(3.0, 6.0, MLSTM_HEADS, dtype=jnp.float32)[None] + nrm((L, MLSTM_HEADS), 0.01),
        'mlstm_norm': gain((L, MLSTM_WIDTH)),
        'w_br_ssd': nrm((L, SSD_D_INNER, D_MODEL), SSD_D_INNER ** -0.5),
        'w_br_swa': nrm((L, SWA_OUT, D_MODEL), SWA_OUT ** -0.5),
        'w_br_s5': nrm((L, S5_WIDTH, D_MODEL), S5_WIDTH ** -0.5),
        'w_br_mlstm': nrm((L, MLSTM_WIDTH, D_MODEL), MLSTM_WIDTH ** -0.5),
        'w_out': nrm((L, D_MODEL, D_MODEL), D_MODEL ** -0.5),
        'norm_xa': gain((L, D_MODEL)),
        'norm_mem': gain((L, D_MODEL)),
        'xa_wq': nrm((L, D_MODEL, D_MODEL), D_MODEL ** -0.5),
        'xa_wk': nrm((L, D_MODEL, D_MODEL), D_MODEL ** -0.5),
        'xa_wv': nrm((L, D_MODEL, D_MODEL), D_MODEL ** -0.5),
        'xa_wo': nrm((L, D_MODEL, D_MODEL), D_MODEL ** -0.5),
        'norm_mlp': gain((L, D_MODEL)),
        'w_ff1': nrm((L, D_MODEL, D_FF), D_MODEL ** -0.5),
        'w_ff2': nrm((L, D_FF, D_MODEL), D_FF ** -0.5),
        'norm_final': gain((D_MODEL,)),
    }


def reference(x_prompt, x_sample, state_ssd, state_ssd_conv, cache_swa_w128, cache_swa_w512, cache_swa_w2048,
              state_s5, state_mlstm_c, state_mlstm_n, state_mlstm_m, cache_mem_kv, mem_prompt,
              norm_mix, w_in, ssd_conv_w, ssd_conv_b, ssd_dt_bias, ssd_a_log, ssd_d, ssd_norm, rel_bias,
              s5_a_re, s5_a_im, s5_log_dt, s5_b_re, s5_b_im, s5_c_re, s5_c_im, s5_d, s5_w_glu,
              mlstm_i_bias, mlstm_f_bias, mlstm_norm, w_br_ssd, w_br_swa, w_br_s5, w_br_mlstm, w_out,
              norm_xa, norm_mem, xa_wq, xa_wk, xa_wv, xa_wo, norm_mlp, w_ff1, w_ff2, norm_final):
    f32 = jnp.float32
    skeys = ('ssd', 'ssd_conv', 'swa0', 'swa1', 'swa2', 's5', 'mlstm_c', 'mlstm_n', 'mlstm_m')
    new_p = {k: [] for k in skeys}
    new_s = {k: [] for k in skeys}
    kv_p_list = []
    xp, xs = x_prompt, x_sample
    for l in range(DEPTH):
        lw = {'norm_mix': norm_mix[l], 'w_in': w_in[l], 'ssd_conv_w': ssd_conv_w[l], 'ssd_conv_b': ssd_conv_b[l],
              'ssd_dt_bias': ssd_dt_bias[l], 'ssd_a_log': ssd_a_log[l], 'ssd_d': ssd_d[l], 'ssd_norm': ssd_norm[l],
              's5_a_re': s5_a_re[l], 's5_a_im': s5_a_im[l], 's5_log_dt': s5_log_dt[l], 's5_b_re': s5_b_re[l],
              's5_b_im': s5_b_im[l], 's5_c_re': s5_c_re[l], 's5_c_im': s5_c_im[l], 's5_d': s5_d[l],
              's5_w_glu': s5_w_glu[l], 'mlstm_i_bias': mlstm_i_bias[l], 'mlstm_f_bias': mlstm_f_bias[l],
              'mlstm_norm': mlstm_norm[l], 'w_br_ssd': w_br_ssd[l], 'w_br_swa': w_br_swa[l], 'w_br_s5': w_br_s5[l],
              'w_br_mlstm': w_br_mlstm[l], 'w_out': w_out[l], 'norm_xa': norm_xa[l], 'xa_wq': xa_wq[l],
              'xa_wo': xa_wo[l], 'norm_mlp': norm_mlp[l], 'w_ff1': w_ff1[l], 'w_ff2': w_ff2[l]}
        kv_p = mem_kv(mem_prompt, norm_mem[l], xa_wk[l], xa_wv[l])
        kv_p_list.append(kv_p)
        st_s = {'ssd': state_ssd[l].astype(f32), 'ssd_conv': state_ssd_conv[l],
                'swa': (cache_swa_w128[l], cache_swa_w512[l], cache_swa_w2048[l]),
                's5': state_s5[l].astype(f32), 'mlstm_c': state_mlstm_c[l].astype(f32),
                'mlstm_n': state_mlstm_n[l].astype(f32), 'mlstm_m': state_mlstm_m[l].astype(f32)}
        xp, sp = layer_forward(xp, kv_p, zero_state(x_prompt.shape[0]), lw, rel_bias, True)
        xs, ss = layer_forward(xs, cache_mem_kv[l], st_s, lw, rel_bias, False)
        for k in skeys:
            new_p[k].append(sp[k])
            new_s[k].append(ss[k])
    y_prompt = rmsnorm(xp, norm_final)
    y_sample = rmsnorm(xs, norm_final)
    ssd_p, ssd_s = jnp.stack(new_p['ssd']), jnp.stack(new_s['ssd'])
    ssd_conv_p, ssd_conv_s = jnp.stack(new_p['ssd_conv']), jnp.stack(new_s['ssd_conv'])
    swa128_p, swa128_s = jnp.stack(new_p['swa0']), jnp.stack(new_s['swa0'])
    swa512_p, swa512_s = jnp.stack(new_p['swa1']), jnp.stack(new_s['swa1'])
    swa2048_p, swa2048_s = jnp.stack(new_p['swa2']), jnp.stack(new_s['swa2'])
    s5_p, s5_s = jnp.stack(new_p['s5']), jnp.stack(new_s['s5'])
    mlstm_c_p, mlstm_c_s = jnp.stack(new_p['mlstm_c']), jnp.stack(new_s['mlstm_c'])
    mlstm_n_p, mlstm_n_s = jnp.stack(new_p['mlstm_n']), jnp.stack(new_s['mlstm_n'])
    mlstm_m_p, mlstm_m_s = jnp.stack(new_p['mlstm_m']), jnp.stack(new_s['mlstm_m'])
    mem_kv_p = jnp.stack(kv_p_list)
    return (y_prompt, y_sample, ssd_p, ssd_s, ssd_conv_p, ssd_conv_s, swa128_p, swa128_s, swa512_p, swa512_s,
            swa2048_p, swa2048_s, s5_p, s5_s, mlstm_c_p, mlstm_c_s, mlstm_n_p, mlstm_n_s, mlstm_m_p, mlstm_m_s,
            mem_kv_p)
```

```python
import functools
import math

import numpy as np
import jax
import jax.numpy as jnp
from jax import lax
from jax.experimental import pallas as pl
from jax.experimental.pallas import tpu as pltpu

F32 = jnp.float32
BF16 = jnp.bfloat16

D_MODEL = 1024
RMS_EPS = 1e-6
N_BRANCH = 4

SSD_D_INNER = 512
SSD_HEAD_DIM = 64
SSD_HEADS = 8
SSD_GROUPS = 2
SSD_STATE = 64
SSD_CONV = 4
SSD_CONV_DIM = SSD_D_INNER + 2 * SSD_GROUPS * SSD_STATE
SSD_SEG = SSD_D_INNER + SSD_CONV_DIM

SWA_PATTERN = ((128, 1), (512, 4), (2048, 16))
SWA_GROUP_HEADS = 4
SWA_HEAD_DIM = 64
SWA_GROUP_WIDTH = SWA_GROUP_HEADS * SWA_HEAD_DIM
SWA_WIDTH = 3 * SWA_GROUP_WIDTH
SWA_BLOCK = 128
REL_BUCKETS = 32
REL_MAX_EXACT = 16
REL_MAX_DIST = 2048

S5_WIDTH = 512
S5_GROUP = 16
S5_GROUPS = 32
S5_STATE = 64
S5_CHUNK = 16

ML_WIDTH = 512
ML_HEADS = 4
ML_HEAD_DIM = 128

XA_HEADS = 4
XA_HEAD_DIM = 256
D_FF = 4096

SMALL_COLS = 128
SMALL_ROWS = 16

SAMPLE_ROWS = 16
TOKEN_TILE = 256
MASKED = -1e30

VMEM_LIMIT = 56 * 1024 * 1024

IN_SIZES = (SSD_D_INNER, SSD_CONV_DIM, SSD_HEADS, SWA_WIDTH, SWA_WIDTH, SWA_WIDTH, S5_WIDTH,
            ML_WIDTH, ML_WIDTH, ML_WIDTH, ML_WIDTH, ML_HEADS, ML_HEADS, N_BRANCH * D_MODEL)
IN_OFFS = tuple(int(v) for v in np.concatenate([[0], np.cumsum(IN_SIZES)]))


def _params(n_axes):
    return pltpu.CompilerParams(dimension_semantics=("arbitrary",) * n_axes, vmem_limit_bytes=VMEM_LIMIT)


def _full(shape):
    nd = len(shape)
    return pl.BlockSpec(shape, lambda *_: (0,) * nd)


def _dot(a, b):
    return jnp.dot(a.astype(BF16), b.astype(BF16), preferred_element_type=F32)


def _dot_nt(a, b):
    return lax.dot_general(a.astype(BF16), b.astype(BF16), (((1,), (1,)), ((), ())), preferred_element_type=F32)


def _dot_tn(a, b):
    return lax.dot_general(a.astype(BF16), b.astype(BF16), (((0,), (0,)), ((), ())), preferred_element_type=F32)


def _split3(x):
    hi = x.astype(BF16)
    r = x - hi.astype(F32)
    mid = r.astype(BF16)
    lo = (r - mid.astype(F32)).astype(BF16)
    return hi, mid, lo


def _dot_exact_l(ones_bf16, x):
    hi, mid, lo = _split3(x)
    f = lambda p: jnp.dot(ones_bf16, p, preferred_element_type=F32)
    return f(hi) + f(mid) + f(lo)


def _dot_exact_r(x, ones_bf16):
    hi, mid, lo = _split3(x)
    f = lambda p: jnp.dot(p, ones_bf16, preferred_element_type=F32)
    return f(hi) + f(mid) + f(lo)


def _dot_x3(a, b):
    ah = a.astype(BF16)
    al = (a - ah.astype(F32)).astype(BF16)
    bh = b.astype(BF16)
    bl = (b - bh.astype(F32)).astype(BF16)
    f = lambda p, q: jnp.dot(p, q, preferred_element_type=F32)
    return f(ah, bh) + f(ah, bl) + f(al, bh)


def _rms(x, g):
    return x * lax.rsqrt(jnp.mean(x * x, axis=-1, keepdims=True) + RMS_EPS) * g


def _sigmoid(x):
    return 1.0 / (1.0 + jnp.exp(-x))


def _softplus(x):
    return jnp.maximum(x, 0.0) + jnp.log(1.0 + jnp.exp(-jnp.abs(x)))


def _tri(q, lower):
    r = lax.broadcasted_iota(jnp.int32, (q, q), 0)
    c = lax.broadcasted_iota(jnp.int32, (q, q), 1)
    return (r >= c) if lower else (r <= c)


def _inproj_kernel(x_ref, g_ref, wa_ref, wb_ref, wc_ref, wd_ref, ws_ref, wst_ref,
                   oa_ref, ob_ref, oc_ref, od_ref, os_ref, ost_ref):
    h = _rms(x_ref[...], g_ref[...]).astype(BF16)
    oa_ref[...] = jnp.dot(h, wa_ref[...], preferred_element_type=F32)
    ob_ref[...] = jnp.dot(h, wb_ref[...], preferred_element_type=F32)
    oc_ref[...] = jnp.dot(h, wc_ref[...], preferred_element_type=F32)
    od_ref[...] = jnp.dot(h, wd_ref[...], preferred_element_type=F32)
    os_ref[...] = jnp.dot(h, ws_ref[...], preferred_element_type=F32)
    ost_ref[0] = lax.dot_general(wst_ref[...], h, (((1,), (1,)), ((), ())), preferred_element_type=F32)


def _inproj(x, g, wa, wb, wc, wd, ws, wst):
    ta = x.shape[0]
    tm = TOKEN_TILE
    widths = (wa.shape[1], wb.shape[1], wc.shape[1], wd.shape[1], ws.shape[1])
    row = lambda n: pl.BlockSpec((tm, n), lambda i: (i, 0))
    return pl.pallas_call(
        _inproj_kernel,
        grid=(ta // tm,),
        in_specs=[row(D_MODEL), _full(g.shape)] + [_full(w.shape) for w in (wa, wb, wc, wd, ws, wst)],
        out_specs=[row(n) for n in widths] + [pl.BlockSpec((1, SMALL_ROWS, tm), lambda i: (0, 0, i))],
        out_shape=[jax.ShapeDtypeStruct((ta, n), F32) for n in widths]
        + [jax.ShapeDtypeStruct((1, SMALL_ROWS, ta), F32)],
        compiler_params=_params(1),
        name="inproj",
    )(x, g, wa, wb, wc, wd, ws, wst)


def _ssd_kernel(p_ref, sm_ref, smt_ref, cinit_ref, sinit_ref, cw_ref, cb_ref, dtb_c_ref, dtb_r_ref,
                alog_c_ref, alog_r_ref, dskip_ref, g_ref, y_ref, sout_ref, ext_sc, st_sc, *, q, valid):
    c = pl.program_id(1)

    @pl.when(c == 0)
    def _():
        ext_sc[0:8, :] = cinit_ref[0]
        st_sc[...] = sinit_ref[0]

    ext_sc[8:8 + q, :] = p_ref[:, SSD_D_INNER:SSD_SEG]
    conv = cb_ref[...] + cw_ref[0:1, :] * ext_sc[5:5 + q, :]
    for j in range(1, SSD_CONV):
        conv = conv + cw_ref[j:j + 1, :] * ext_sc[5 + j:5 + j + q, :]
    ext_sc[0:8, :] = ext_sc[q:q + 8, :]
    xbc = conv * _sigmoid(conv)
    xs = xbc[:, :SSD_D_INNER]
    z = p_ref[:, :SSD_D_INNER]

    dt_c = _softplus(sm_ref[:, 0:SSD_HEADS] + dtb_c_ref[...])
    dt_r = _softplus(smt_ref[0, 0:SSD_HEADS, :] + dtb_r_ref[...])
    if valid < q:
        dt_c = jnp.where(lax.broadcasted_iota(jnp.int32, dt_c.shape, 0) < valid, dt_c, 0.0)
        dt_r = jnp.where(lax.broadcasted_iota(jnp.int32, dt_r.shape, 1) < valid, dt_r, 0.0)
    da_c = dt_c * (-jnp.exp(alog_c_ref[...]))
    da_r = dt_r * (-jnp.exp(alog_r_ref[...]))
    causal = _tri(q, True)
    cum_c = _dot_exact_l(jnp.where(causal, 1.0, 0.0).astype(BF16), da_c)
    cum_r = _dot_exact_r(da_r, jnp.where(_tri(q, False), 1.0, 0.0).astype(BF16))

    rep = SSD_HEADS // SSD_GROUPS
    ys = []
    for h in range(SSD_HEADS):
        grp = h // rep
        b_g = xbc[:, SSD_D_INNER + grp * SSD_STATE:SSD_D_INNER + (grp + 1) * SSD_STATE]
        c_off = SSD_D_INNER + SSD_GROUPS * SSD_STATE
        c_g = xbc[:, c_off + grp * SSD_STATE:c_off + (grp + 1) * SSD_STATE]
        cb = _dot_nt(c_g, b_g)
        x_h = xs[:, h * SSD_HEAD_DIM:(h + 1) * SSD_HEAD_DIM]
        cc = cum_c[:, h:h + 1]
        seg = jnp.where(causal, cc - cum_r[h:h + 1, :], MASKED)
        w = cb * jnp.exp(seg) * dt_r[h:h + 1, :]
        s_h = st_sc[h]
        y_h = _dot(w, x_h) + jnp.exp(cc) * _dot_nt(c_g, s_h)
        last = cum_c[q - 1:q, h:h + 1]
        tail = jnp.exp(last - cc) * dt_c[:, h:h + 1]
        st_sc[h] = s_h * jnp.exp(last) + _dot_tn(x_h * tail, b_g)
        ys.append(y_h)
    y = jnp.concatenate(ys, axis=1)
    y = (y + dskip_ref[...] * xs) * (z * _sigmoid(z))
    y_ref[...] = _rms(y, g_ref[...])

    @pl.when(c == pl.num_programs(1) - 1)
    def _():
        sout_ref[0] = st_sc[...]


def _ssd_call(p_ssd, p_small, p_small_t, conv_init, state_init, lw, *, nb, nc, q, valid, row0, prev):
    ta = p_ssd.shape[0]
    blk0 = row0 // q
    rows = lambda n: pl.BlockSpec((q, n), lambda b, c: (blk0 + b * nc + c, 0))
    if p_small_t.shape[0] == 1:
        smt_spec = pl.BlockSpec((1, SMALL_ROWS, q), lambda b, c: (0, 0, blk0 + b * nc + c))
    else:
        smt_spec = pl.BlockSpec((1, SMALL_ROWS, q), lambda b, c: (b, 0, 0))
    params = (lw["ssd_conv_w"], lw["ssd_conv_b"], lw["dtb_c"], lw["dtb_r"], lw["alog_c"], lw["alog_r"],
              lw["ssd_dskip"], lw["ssd_norm"])
    in_specs = [rows(SSD_SEG), rows(SMALL_COLS), smt_spec,
                pl.BlockSpec((1, 8, SSD_CONV_DIM), lambda b, c: (b, 0, 0)),
                pl.BlockSpec((1, SSD_HEADS, SSD_HEAD_DIM, SSD_STATE), lambda b, c: (b, 0, 0, 0))]
    in_specs += [_full(p.shape) for p in params]
    args = [p_ssd, p_small, p_small_t, conv_init, state_init, *params]
    aliases = {}
    if prev is not None:
        in_specs.append(pl.BlockSpec(memory_space=pl.ANY))
        args.append(prev)
        aliases = {len(args) - 1: 0}
    kern = functools.partial(_ssd_kernel, q=q, valid=valid)
    if prev is not None:
        kern = _drop_arg(kern, len(args) - 1)
    return pl.pallas_call(
        kern,
        grid=(nb, nc),
        in_specs=in_specs,
        out_specs=[rows(SSD_D_INNER),
                   pl.BlockSpec((1, SSD_HEADS, SSD_HEAD_DIM, SSD_STATE), lambda b, c: (b, 0, 0, 0))],
        out_shape=[jax.ShapeDtypeStruct((ta, SSD_D_INNER), F32),
                   jax.ShapeDtypeStruct((nb, SSD_HEADS, SSD_HEAD_DIM, SSD_STATE), F32)],
        scratch_shapes=[pltpu.VMEM((q + 8, SSD_CONV_DIM), F32),
                        pltpu.VMEM((SSD_HEADS, SSD_HEAD_DIM, SSD_STATE), F32)],
        input_output_aliases=aliases,
        compiler_params=_params(2),
        name="ssd_q%d" % q,
    )(*args)


def _drop_arg(kern, idx):
    def wrapped(*refs):
        return kern(*refs[:idx], *refs[idx + 1:])
    return wrapped


def _mlstm_kernel(p_ref, sm_ref, smt_ref, cinit_ref, ninit_ref, minit_ref, ib_c_ref, ib_r_ref, fb_c_ref, fb_r_ref,
                  g_ref, y_ref, cout_ref, nout_ref, mout_ref, c_sc, n_sc, m_sc, *, q, valid):
    c = pl.program_id(1)

    @pl.when(c == 0)
    def _():
        c_sc[...] = cinit_ref[0]
        n_sc[...] = ninit_ref[0]
        m_sc[...] = minit_ref[0]

    w = ML_WIDTH
    i0 = SSD_HEADS
    f0 = SSD_HEADS + ML_HEADS
    ig_c = sm_ref[:, i0:i0 + ML_HEADS] + ib_c_ref[...]
    fg_c = sm_ref[:, f0:f0 + ML_HEADS] + fb_c_ref[...]
    ig_r = smt_ref[0, i0:i0 + ML_HEADS, :] + ib_r_ref[...]
    fg_r = smt_ref[0, f0:f0 + ML_HEADS, :] + fb_r_ref[...]
    lf_c = -_softplus(-fg_c)
    lf_r = -_softplus(-fg_r)
    if valid < q:
        ok_c = lax.broadcasted_iota(jnp.int32, ig_c.shape, 0) < valid
        ok_r = lax.broadcasted_iota(jnp.int32, ig_r.shape, 1) < valid
        ig_c = jnp.where(ok_c, ig_c, MASKED)
        ig_r = jnp.where(ok_r, ig_r, MASKED)
        lf_c = jnp.where(ok_c, lf_c, 0.0)
        lf_r = jnp.where(ok_r, lf_r, 0.0)
    causal = _tri(q, True)
    b_c = _dot_exact_l(jnp.where(causal, 1.0, 0.0).astype(BF16), lf_c)
    b_r = _dot_exact_r(lf_r, jnp.where(_tri(q, False), 1.0, 0.0).astype(BF16))

    scale = ML_HEAD_DIM ** -0.5
    ys = []
    for h in range(ML_HEADS):
        hs = slice(h * ML_HEAD_DIM, (h + 1) * ML_HEAD_DIM)
        q_h = p_ref[:, hs]
        k_h = p_ref[:, w + h * ML_HEAD_DIM:w + (h + 1) * ML_HEAD_DIM] * scale
        v_h = p_ref[:, 2 * w + h * ML_HEAD_DIM:2 * w + (h + 1) * ML_HEAD_DIM]
        o_h = p_ref[:, 3 * w + h * ML_HEAD_DIM:3 * w + (h + 1) * ML_HEAD_DIM]
        m_prev = m_sc[h][:, 0:1]
        bc = b_c[:, h:h + 1]
        intra = jnp.where(causal, bc - b_r[h:h + 1, :] + ig_r[h:h + 1, :], MASKED)
        inter = bc + m_prev
        m_t = jnp.maximum(inter, jnp.max(intra, axis=1, keepdims=True))
        wgt = _dot_nt(q_h, k_h) * jnp.exp(intra - m_t)
        w_inter = jnp.exp(inter - m_t)
        c_h = c_sc[h]
        n_h = n_sc[h]
        num = _dot(wgt, v_h) + w_inter * _dot(q_h, c_h)
        den = jnp.sum(wgt, axis=1, keepdims=True) + w_inter * jnp.sum(q_h * n_h, axis=1, keepdims=True)
        hh = num / jnp.maximum(jnp.abs(den), jnp.exp(-m_t))
        m_new = m_t[q - 1:q, :]
        b_last = b_c[q - 1:q, h:h + 1]
        wk = jnp.exp(b_last - bc + ig_c[:, h:h + 1] - m_new)
        decay = jnp.exp(b_last + m_prev - m_new)
        kw = k_h * wk
        c_sc[h] = decay * c_h + _dot_tn(kw, v_h)
        n_sc[h] = decay * n_h + jnp.sum(kw, axis=0, keepdims=True)
        m_sc[h] = jnp.broadcast_to(m_new, (1, ML_HEAD_DIM))
        hn = hh * lax.rsqrt(jnp.mean(hh * hh, axis=-1, keepdims=True) + RMS_EPS) * g_ref[:, hs]
        ys.append(_sigmoid(o_h) * hn)
    y_ref[...] = jnp.concatenate(ys, axis=1)

    @pl.when(c == pl.num_programs(1) - 1)
    def _():
        cout_ref[0] = c_sc[...]
        nout_ref[0] = n_sc[...]
        mout_ref[0] = m_sc[...]


def _mlstm_call(p_ml, p_small, p_small_t, c_init, n_init, m_init, lw, *, nb, nc, q, valid, row0, prev):
    ta = p_ml.shape[0]
    blk0 = row0 // q
    rows = lambda n: pl.BlockSpec((q, n), lambda b, c: (blk0 + b * nc + c, 0))
    if p_small_t.shape[0] == 1:
        smt_spec = pl.BlockSpec((1, SMALL_ROWS, q), lambda b, c: (0, 0, blk0 + b * nc + c))
    else:
        smt_spec = pl.BlockSpec((1, SMALL_ROWS, q), lambda b, c: (b, 0, 0))
    params = (lw["ib_c"], lw["ib_r"], lw["fb_c"], lw["fb_r"], lw["mlstm_norm"])
    c_spec = pl.BlockSpec((1, ML_HEADS, ML_HEAD_DIM, ML_HEAD_DIM), lambda b, c: (b, 0, 0, 0))
    v_spec = pl.BlockSpec((1, ML_HEADS, 1, ML_HEAD_DIM), lambda b, c: (b, 0, 0, 0))
    in_specs = [rows(4 * ML_WIDTH), rows(SMALL_COLS), smt_spec, c_spec, v_spec, v_spec]
    in_specs += [_full(p.shape) for p in params]
    args = [p_ml, p_small, p_small_t, c_init, n_init, m_init, *params]
    aliases = {}
    kern = functools.partial(_mlstm_kernel, q=q, valid=valid)
    if prev is not None:
        in_specs.append(pl.BlockSpec(memory_space=pl.ANY))
        args.append(prev)
        aliases = {len(args) - 1: 0}
        kern = _drop_arg(kern, len(args) - 1)
    vec = jax.ShapeDtypeStruct((nb, ML_HEADS, 1, ML_HEAD_DIM), F32)
    return pl.pallas_call(
        kern,
        grid=(nb, nc),
        in_specs=in_specs,
        out_specs=[rows(ML_WIDTH), c_spec, v_spec, v_spec],
        out_shape=[jax.ShapeDtypeStruct((ta, ML_WIDTH), F32),
                   jax.ShapeDtypeStruct((nb, ML_HEADS, ML_HEAD_DIM, ML_HEAD_DIM), F32), vec, vec],
        scratch_shapes=[pltpu.VMEM((ML_HEADS, ML_HEAD_DIM, ML_HEAD_DIM), F32),
                        pltpu.VMEM((ML_HEADS, 1, ML_HEAD_DIM), F32),
                        pltpu.VMEM((ML_HEADS, 1, ML_HEAD_DIM), F32)],
        input_output_aliases=aliases,
        compiler_params=_params(2),
        name="mlstm_q%d" % q,
    )(*args)


def _s5_kernel(u_ref, h0_ref, tz_ref, sm_ref, cm_ref, pa_ref, pb_ref, va_ref, vb_ref, y_ref, hout_ref, *, rows, scan):
    u = u_ref[0, 0]
    h0 = h0_ref[0, 0]
    half = S5_STATE

    def cmul(a, b, x):
        return a * x + b * pltpu.roll(x, half, axis=1)

    contrib = _dot_x3(u, sm_ref[0])
    carried = cmul(va_ref[0], vb_ref[0], h0)
    if scan:
        ridx = lax.broadcasted_iota(jnp.int32, (rows, 2 * half), 0)
        x = contrib + jnp.where(ridx == 0, carried, 0.0)
        k = 0
        while (1 << k) < rows:
            s = 1 << k
            shifted = jnp.where(ridx >= s, pltpu.roll(x, s, axis=0), 0.0)
            x = x + cmul(pa_ref[0, k:k + 1, :], pb_ref[0, k:k + 1, :], shifted)
            k += 1
        hprev = jnp.where(ridx == 0, h0, pltpu.roll(x, 1, axis=0))
        hout_ref[0, 0] = x[rows - 1:rows, :]
    else:
        x = contrib + carried
        hprev = h0
        hout_ref[0, 0] = x
    y_ref[0, 0] = _dot(u, tz_ref[0]) + _dot(hprev, cm_ref[0])


def _s5_call(u, h0, tab, *, scan):
    nb, ng, rows, width = u.shape
    hrows = h0.shape[2]
    va, vb = (tab["va16"], tab["vb16"]) if scan else (tab["va"], tab["vb"])
    per_group = lambda a: pl.BlockSpec((1,) + a.shape[1:], lambda b, g: (g,) + (0,) * (a.ndim - 1))
    tabs = (tab["tz"], tab["sm16"] if scan else tab["sm"], tab["cm"], tab["pa"], tab["pb"], va, vb)
    return pl.pallas_call(
        functools.partial(_s5_kernel, rows=rows, scan=scan),
        grid=(nb, ng),
        in_specs=[pl.BlockSpec((1, 1, rows, width), lambda b, g: (b, g, 0, 0)),
                  pl.BlockSpec((1, 1, hrows, 2 * S5_STATE), lambda b, g: (b, g, 0, 0))]
        + [per_group(a) for a in tabs],
        out_specs=[pl.BlockSpec((1, 1, rows, width), lambda b, g: (b, g, 0, 0)),
                   pl.BlockSpec((1, 1, hrows, 2 * S5_STATE), lambda b, g: (b, g, 0, 0))],
        out_shape=[jax.ShapeDtypeStruct(u.shape, F32), jax.ShapeDtypeStruct(h0.shape, F32)],
        compiler_params=_params(2),
        name="s5_scan" if scan else "s5_step",
    )(u, h0, *tabs)


def _s5_tables(a_re, a_im, log_dt, b_re, b_im, c_re, c_im, valid):
    hp = lax.Precision.HIGHEST
    q = S5_CHUNK
    dt = jnp.exp(log_dt)[:, None]
    mag = jnp.exp(a_re * dt)
    ab_re = mag * jnp.cos(a_im * dt)
    ab_im = mag * jnp.sin(a_im * dt)
    inv = 1.0 / (a_re * a_re + a_im * a_im)
    co_re = ((ab_re - 1.0) * a_re + ab_im * a_im) * inv
    co_im = (ab_im * a_re - (ab_re - 1.0) * a_im) * inv

    def cmul(x, y):
        return x[0] * y[0] - x[1] * y[1], x[0] * y[1] + x[1] * y[0]

    pw = [(jnp.ones_like(ab_re), jnp.zeros_like(ab_re))]
    for _ in range(q):
        pw.append(cmul(pw[-1], (ab_re, ab_im)))
    pw_re = jnp.stack([p[0] for p in pw])
    pw_im = jnp.stack([p[1] for p in pw])
    bt_re = co_re[..., None] * b_re - co_im[..., None] * b_im
    bt_im = co_re[..., None] * b_im + co_im[..., None] * b_re
    cp_re = c_re[None] * pw_re[:, :, None, :] - c_im[None] * pw_im[:, :, None, :]
    cp_im = c_re[None] * pw_im[:, :, None, :] + c_im[None] * pw_re[:, :, None, :]
    kern = (jnp.einsum("jgcn,gnk->jgck", cp_re, bt_re, precision=hp)
            - jnp.einsum("jgcn,gnk->jgck", cp_im, bt_im, precision=hp))
    lag = np.arange(q)[None, :] - np.arange(q)[:, None]
    kg = kern[np.clip(lag, 0, q)]
    kg = jnp.where((lag >= 0)[:, :, None, None, None], kg, 0.0)
    tz = jnp.transpose(kg, (2, 0, 4, 1, 3)).reshape(S5_GROUPS, q * S5_GROUP, q * S5_GROUP)

    def state_in(nvalid):
        e = np.clip(nvalid - 1 - np.arange(q), 0, q)
        pr = pw_re[e][:, :, :, None]
        pi = pw_im[e][:, :, :, None]
        sr = pr * bt_re[None] - pi * bt_im[None]
        si = pr * bt_im[None] + pi * bt_re[None]
        m = jnp.concatenate([sr, si], axis=2)
        m = jnp.where((np.arange(q) < nvalid)[:, None, None, None], m, 0.0)
        return jnp.transpose(m, (1, 0, 3, 2)).reshape(S5_GROUPS, q * S5_GROUP, 2 * S5_STATE)

    def packed(p):
        return (jnp.concatenate([p[0], p[0]], axis=-1), jnp.concatenate([-p[1], p[1]], axis=-1))

    cm_re = jnp.transpose(cp_re[1:], (1, 3, 0, 2)).reshape(S5_GROUPS, S5_STATE, q * S5_GROUP)
    cm_im = jnp.transpose(cp_im[1:], (1, 3, 0, 2)).reshape(S5_GROUPS, S5_STATE, q * S5_GROUP)
    cm = jnp.concatenate([cm_re, -cm_im], axis=1)
    doubling = [pw[q]]
    for _ in range(11):
        doubling.append(cmul(doubling[-1], doubling[-1]))
    pa = jnp.stack([packed(p)[0] for p in doubling], axis=1)
    pb = jnp.stack([packed(p)[1] for p in doubling], axis=1)
    va16, vb16 = packed(pw[q])
    va, vb = packed(pw[valid])
    return {"tz": tz, "sm16": state_in(q), "sm": state_in(valid), "cm": cm, "pa": pa, "pb": pb,
            "va16": va16[:, None, :], "vb16": vb16[:, None, :], "va": va[:, None, :], "vb": vb[:, None, :]}


def _t5_bucket(dist):
    dist = np.asarray(dist)
    large = REL_MAX_EXACT + (np.log(np.maximum(dist, 1) / REL_MAX_EXACT)
                             / math.log(REL_MAX_DIST / REL_MAX_EXACT)
                             * (REL_BUCKETS - REL_MAX_EXACT)).astype(np.int32)
    large = np.minimum(large, REL_BUCKETS - 1)
    return np.where(dist < REL_MAX_EXACT, dist, large).astype(np.int32)


def _attend(q, pieces):
    logits = [_dot_nt(q, k) + bias for k, _, bias in pieces]
    m = logits[0].max(axis=1, keepdims=True)
    for s in logits[1:]:
        m = jnp.maximum(m, s.max(axis=1, keepdims=True))
    ps = [jnp.exp(s - m) for s in logits]
    den = ps[0].sum(axis=1, keepdims=True)
    for p in ps[1:]:
        den = den + p.sum(axis=1, keepdims=True)
    o = _dot(ps[0], pieces[0][1])
    for p, (_, v, _) in zip(ps[1:], pieces[1:]):
        o = o + _dot(p, v)
    return o / den, m + jnp.log(den)


def _swa_prompt_kernel(q_ref, k_ref, v_ref, kp_ref, vp_ref, bias_ref, o_ref, lse_ref, *, dil):
    n = SWA_BLOCK
    first = pl.program_id(1) == 0
    pair = pl.program_id(2)
    scale = SWA_HEAD_DIM ** -0.5

    def residue(r):
        if dil == 1:
            sl = slice(None)
        else:
            sl = pl.ds(r, n, stride=dil)
        qq, kk, vv, kp, vp = q_ref[sl, :], k_ref[sl, :], v_ref[sl, :], kp_ref[sl, :], vp_ref[sl, :]
        outs, lses = [], []
        for j in range(2):
            hs = slice(j * SWA_HEAD_DIM, (j + 1) * SWA_HEAD_DIM)
            bias = bias_ref[2 * pair + j]
            bias_prev = jnp.where(first, MASKED, bias[:, 0:n])
            o, lse = _attend(qq[:, hs] * scale,
                             [(kk[:, hs], vv[:, hs], bias[:, n:2 * n]), (kp[:, hs], vp[:, hs], bias_prev)])
            outs.append(o)
            lses.append(jnp.broadcast_to(lse, o.shape))
        o_ref[sl, :] = jnp.concatenate(outs, axis=1)
        lse_ref[sl, :] = jnp.concatenate(lses, axis=1)

    if dil == 1:
        residue(0)
    else:
        def body(r, carry):
            residue(r)
            return carry
        lax.fori_loop(0, dil, body, 0)


def _swa_prompt_call(p_swa, bias, *, grp, dil, nb, seq):
    ta = p_swa.shape[0]
    sb = SWA_BLOCK * dil
    nsb = seq // sb
    lanes = 2 * SWA_HEAD_DIM
    cur = lambda col: pl.BlockSpec((sb, lanes), lambda b, c, p: (b * nsb + c, 2 * col + p))
    prv = lambda col: pl.BlockSpec((sb, lanes), lambda b, c, p: (b * nsb + jnp.maximum(c - 1, 0), 2 * col + p))
    out = pl.BlockSpec((sb, lanes), lambda b, c, p: (b * nsb + c, p))
    shape = jax.ShapeDtypeStruct((ta, SWA_GROUP_WIDTH), F32)
    return pl.pallas_call(
        functools.partial(_swa_prompt_kernel, dil=dil),
        grid=(nb, nsb, SWA_GROUP_HEADS // 2),
        in_specs=[cur(grp), cur(3 + grp), cur(6 + grp), prv(3 + grp), prv(6 + grp), _full(bias.shape)],
        out_specs=[out, out],
        out_shape=[shape, shape],
        compiler_params=_params(3),
        name="swa_prompt_d%d" % dil,
    )(p_swa, p_swa, p_swa, p_swa, p_swa, bias)


def _swa_prompt_bias(rel_bias, grp, dil):
    n = SWA_BLOCK
    heads = slice(grp * SWA_GROUP_HEADS, (grp + 1) * SWA_GROUP_HEADS)
    steps = rel_bias[_t5_bucket(dil * np.arange(n + 1))][:, heads]
    dist = np.arange(n)[:, None] + n - np.arange(2 * n)[None, :]
    band = (dist >= 0) & (dist <= n)
    bias = jnp.transpose(steps[np.clip(dist, 0, n)], (2, 0, 1))
    return jnp.where(band[None], bias, MASKED).astype(F32)


def _swa_sample_kernel(q_ref, k_ref, v_ref, buf_ref, bias_buf_ref, bias_new_ref, o_ref, lse_ref, cache_ref, *,
                       width, t_new):
    scale = SWA_HEAD_DIM ** -0.5
    w = SWA_GROUP_WIDTH
    qq, kn, vn = q_ref[...], k_ref[...], v_ref[...]
    buf = buf_ref[0]
    outs, lses = [], []
    for h in range(SWA_GROUP_HEADS):
        hs = slice(h * SWA_HEAD_DIM, (h + 1) * SWA_HEAD_DIM)
        vs = slice(w + h * SWA_HEAD_DIM, w + (h + 1) * SWA_HEAD_DIM)
        o, lse = _attend(qq[:, hs] * scale,
                         [(buf[:, hs], buf[:, vs], bias_buf_ref[h]), (kn[:, hs], vn[:, hs], bias_new_ref[h])])
        outs.append(o)
        lses.append(jnp.broadcast_to(lse, o.shape))
    o_ref[...] = jnp.concatenate(outs, axis=1)
    lse_ref[...] = jnp.concatenate(lses, axis=1)
    cache_ref[0, 0:width - t_new, :] = buf_ref[0, t_new:width, :]
    cache_ref[0, width - t_new:width, :] = jnp.concatenate([kn[0:t_new], vn[0:t_new]], axis=1)


def _swa_sample_call(p_swa, cache, bias_buf, bias_new, prev_o, prev_lse, *, grp, nb, row0, t_new):
    ta = p_swa.shape[0]
    width = cache.shape[1]
    w = SWA_GROUP_WIDTH
    blk0 = row0 // SAMPLE_ROWS
    tok = lambda col: pl.BlockSpec((SAMPLE_ROWS, w), lambda b: (blk0 + b, col))
    cache_spec = pl.BlockSpec((1, width, 2 * w), lambda b: (b, 0, 0))
    kern = functools.partial(_swa_sample_kernel, width=width, t_new=t_new)
    kern = _drop_arg(_drop_arg(kern, 6), 6)
    return pl.pallas_call(
        kern,
        grid=(nb,),
        in_specs=[tok(grp), tok(3 + grp), tok(6 + grp), cache_spec, _full(bias_buf.shape), _full(bias_new.shape),
                  pl.BlockSpec(memory_space=pl.ANY), pl.BlockSpec(memory_space=pl.ANY)],
        out_specs=[tok(0), tok(0), cache_spec],
        out_shape=[jax.ShapeDtypeStruct((ta, w), F32), jax.ShapeDtypeStruct((ta, w), F32),
                   jax.ShapeDtypeStruct(cache.shape, F32)],
        input_output_aliases={6: 0, 7: 1},
        compiler_params=_params(1),
        name="swa_sample_w%d" % width,
    )(p_swa, p_swa, p_swa, cache, bias_buf, bias_new, prev_o, prev_lse)


def _swa_sample_bias(rel_bias, grp, window, dil, width, t_new):
    n = window // dil
    heads = slice(grp * SWA_GROUP_HEADS, (grp + 1) * SWA_GROUP_HEADS)
    steps = rel_bias[_t5_bucket(dil * np.arange(n + 1))][:, heads]
    t = np.arange(SAMPLE_ROWS)[:, None]

    def table(delta):
        ok = (delta >= 0) & (delta % dil == 0) & (delta // dil <= n) & (t < t_new)
        j = np.clip(delta // dil, 0, n)
        return jnp.where(ok[None], jnp.transpose(steps[j], (2, 0, 1)), MASKED).astype(F32)

    b_buf = table(width + t - np.arange(width)[None, :])
    s = np.arange(SAMPLE_ROWS)[None, :]
    b_new = table(np.where(s < t_new, t - s, -1))
    pad_self = (t >= t_new) & (s == t)
    b_new = jnp.where(pad_self[None], 0.0, b_new)
    return b_buf, b_new


def _merge_kernel(x_ref, yssd_ref, o0_ref, l0_ref, o1_ref, l1_ref, o2_ref, l2_ref, ys5_ref, u_ref, yml_ref,
                  g_ref, wg_ref, wssd_ref, wswa_ref, ws5_ref, wml_ref, wglu_ref, wout_ref, d_ref, out_ref):
    x = x_ref[...]
    h = _rms(x, g_ref[...]).astype(BF16)
    l0, l1, l2 = l0_ref[...], l1_ref[...], l2_ref[...]
    m = jnp.maximum(jnp.maximum(l0, l1), l2)
    e0, e1, e2 = jnp.exp(l0 - m), jnp.exp(l1 - m), jnp.exp(l2 - m)
    y_swa = (e0 * o0_ref[...] + e1 * o1_ref[...] + e2 * o2_ref[...]) / (e0 + e1 + e2)
    y5 = ys5_ref[...] + d_ref[...] * u_ref[...]
    y_s5 = y5 * _sigmoid(_dot(y5, wglu_ref[...]))
    branches = ((yssd_ref[...], wssd_ref), (y_swa, wswa_ref), (y_s5, ws5_ref), (yml_ref[...], wml_ref))
    merged = None
    for i, (y, w_ref) in enumerate(branches):
        gate = _sigmoid(jnp.dot(h, wg_ref[:, i * D_MODEL:(i + 1) * D_MODEL], preferred_element_type=F32))
        term = gate * _dot(y, w_ref[...])
        merged = term if merged is None else merged + term
    out_ref[...] = x + _dot(merged, wout_ref[...])


def _merge_call(x, acts, lw):
    ta = x.shape[0]
    tm = TOKEN_TILE
    row = lambda a: pl.BlockSpec((tm, a.shape[1]), lambda i: (i, 0))
    weights = (lw["norm_mix"], lw["w_gate"], lw["w_br_ssd"], lw["w_br_swa"], lw["w_br_s5"], lw["w_br_mlstm"],
               lw["s5_w_glu"], lw["w_out"], lw["s5_d"])
    return pl.pallas_call(
        _merge_kernel,
        grid=(ta // tm,),
        in_specs=[row(a) for a in (x, *acts)] + [_full(w.shape) for w in weights],
        out_specs=row(x),
        out_shape=jax.ShapeDtypeStruct(x.shape, F32),
        compiler_params=_params(1),
        name="merge",
    )(x, *acts, *weights)


def _norm_matmul_kernel(x_ref, g_ref, w_ref, o_ref):
    o_ref[...] = jnp.dot(_rms(x_ref[...], g_ref[...]).astype(BF16), w_ref[...], preferred_element_type=F32)


def _norm_matmul(x, g, w, tm):
    rows = x.shape[0]
    return pl.pallas_call(
        _norm_matmul_kernel,
        grid=(rows // tm,),
        in_specs=[pl.BlockSpec((tm, x.shape[1]), lambda i: (i, 0)), _full(g.shape), _full(w.shape)],
        out_specs=pl.BlockSpec((tm, w.shape[1]), lambda i: (i, 0)),
        out_shape=jax.ShapeDtypeStruct((rows, w.shape[1]), F32),
        compiler_params=_params(1),
        name="norm_matmul",
    )(x, g, w)


def _rms_kernel(x_ref, g_ref, o_ref):
    o_ref[...] = _rms(x_ref[...], g_ref[...])


def _rmsnorm_call(x, g, tm):
    rows = x.shape[0]
    return pl.pallas_call(
        _rms_kernel,
        grid=(rows // tm,),
        in_specs=[pl.BlockSpec((tm, x.shape[1]), lambda i: (i, 0)), _full(g.shape)],
        out_specs=pl.BlockSpec((tm, x.shape[1]), lambda i: (i, 0)),
        out_shape=jax.ShapeDtypeStruct(x.shape, F32),
        compiler_params=_params(1),
        name="final_norm",
    )(x, g)


def _xattn_heads(q, kv):
    scale = XA_HEAD_DIM ** -0.5
    outs = []
    for h in range(XA_HEADS):
        hs = slice(h * XA_HEAD_DIM, (h + 1) * XA_HEAD_DIM)
        vs = slice(D_MODEL + h * XA_HEAD_DIM, D_MODEL + (h + 1) * XA_HEAD_DIM)
        logits = _dot_nt(q[:, hs], kv[:, hs]) * scale
        m = logits.max(axis=1, keepdims=True)
        p = jnp.exp(logits - m)
        outs.append(_dot(p / p.sum(axis=1, keepdims=True), kv[:, vs]))
    return jnp.concatenate(outs, axis=1)


def _xattn_prompt_kernel(x_ref, kv_ref, g_ref, wq_ref, wo_ref, out_ref):
    x = x_ref[...]
    q = jnp.dot(_rms(x, g_ref[...]).astype(BF16), wq_ref[...], preferred_element_type=F32)
    out_ref[...] = x + _dot(_xattn_heads(q, kv_ref[0]), wo_ref[...])


def _xattn_sample_kernel(x_ref, kv_ref, g_ref, wq_ref, wo_ref, out_ref):
    _xattn_prompt_kernel(x_ref, kv_ref, g_ref, wq_ref, wo_ref, out_ref)


def _xattn_call(x, kv, lw, *, tm, n_tiles, tiles_per_seq, row0, prev):
    ta = x.shape[0]
    blk0 = row0 // tm
    row = pl.BlockSpec((tm, D_MODEL), lambda i: (blk0 + i, 0))
    weights = (lw["norm_xa"], lw["xa_wq"], lw["xa_wo"])
    in_specs = [row, pl.BlockSpec((1,) + kv.shape[1:], lambda i: (i // tiles_per_seq, 0, 0))]
    in_specs += [_full(w.shape) for w in weights]
    args = [x, kv, *weights]
    kern = _xattn_prompt_kernel
    aliases = {}
    if prev is not None:
        in_specs.append(pl.BlockSpec(memory_space=pl.ANY))
        args.append(prev)
        aliases = {len(args) - 1: 0}
        kern = _drop_arg(kern, len(args) - 1)
    return pl.pallas_call(
        kern,
        grid=(n_tiles,),
        in_specs=in_specs,
        out_specs=row,
        out_shape=jax.ShapeDtypeStruct((ta, D_MODEL), F32),
        input_output_aliases=aliases,
        compiler_params=_params(1),
        name="xattn_tm%d" % tm,
    )(*args)


def _ffn_kernel(x_ref, g_ref, w1_ref, w2_ref, out_ref):
    x = x_ref[...]
    a = jnp.dot(_rms(x, g_ref[...]).astype(BF16), w1_ref[...], preferred_element_type=F32)
    a = jnp.square(jnp.maximum(a, 0.0))
    out_ref[...] = x + _dot(a, w2_ref[...])


def _ffn_call(x, lw):
    ta = x.shape[0]
    tm = TOKEN_TILE
    row = pl.BlockSpec((tm, D_MODEL), lambda i: (i, 0))
    weights = (lw["norm_mlp"], lw["w_ff1"], lw["w_ff2"])
    return pl.pallas_call(
        _ffn_kernel,
        grid=(ta // tm,),
        in_specs=[row] + [_full(w.shape) for w in weights],
        out_specs=row,
        out_shape=jax.ShapeDtypeStruct(x.shape, F32),
        compiler_params=_params(1),
        name="ffn",
    )(x, *weights)


def _layer_weights(l, w):
    o = IN_OFFS
    w_in = w["w_in"][l]
    cut = lambda a, b: w_in[:, o[a]:o[b]].astype(BF16)
    small = jnp.concatenate([w_in[:, o[2]:o[3]], w_in[:, o[11]:o[13]]], axis=1)
    row = lambda v: v[None, :].astype(F32)
    col = lambda v: v[:, None].astype(F32)
    return {
        "norm_mix": row(w["norm_mix"][l]),
        "w_ssd": cut(0, 2), "w_swa": cut(3, 6), "w_s5": cut(6, 7), "w_ml": cut(7, 11), "w_gate": cut(13, 14),
        "w_small": jnp.pad(small, ((0, 0), (0, SMALL_COLS - small.shape[1]))).astype(BF16),
        "w_small_t": jnp.transpose(small).astype(BF16),
        "ssd_conv_w": w["ssd_conv_w"][l], "ssd_conv_b": row(w["ssd_conv_b"][l]),
        "dtb_c": row(w["ssd_dt_bias"][l]), "dtb_r": col(w["ssd_dt_bias"][l]),
        "alog_c": row(w["ssd_a_log"][l]), "alog_r": col(w["ssd_a_log"][l]),
        "ssd_dskip": row(jnp.repeat(w["ssd_d"][l], SSD_HEAD_DIM)), "ssd_norm": row(w["ssd_norm"][l]),
        "ib_c": row(w["mlstm_i_bias"][l]), "ib_r": col(w["mlstm_i_bias"][l]),
        "fb_c": row(w["mlstm_f_bias"][l]), "fb_r": col(w["mlstm_f_bias"][l]),
        "mlstm_norm": row(w["mlstm_norm"][l]),
        "s5_d": row(w["s5_d"][l]), "s5_w_glu": w["s5_w_glu"][l].astype(BF16),
        "w_br_ssd": w["w_br_ssd"][l].astype(BF16), "w_br_swa": w["w_br_swa"][l].astype(BF16),
        "w_br_s5": w["w_br_s5"][l].astype(BF16), "w_br_mlstm": w["w_br_mlstm"][l].astype(BF16),
        "w_out": w["w_out"][l].astype(BF16),
        "norm_xa": row(w["norm_xa"][l]), "xa_wq": w["xa_wq"][l].astype(BF16), "xa_wo": w["xa_wo"][l].astype(BF16),
        "norm_mem": row(w["norm_mem"][l]),
        "xa_wkv": jnp.concatenate([w["xa_wk"][l], w["xa_wv"][l]], axis=1).astype(BF16),
        "norm_mlp": row(w["norm_mlp"][l]), "w_ff1": w["w_ff1"][l].astype(BF16), "w_ff2": w["w_ff2"][l].astype(BF16),
    }


def kernel(x_prompt, x_sample, state_ssd, state_ssd_conv, cache_swa_w128, cache_swa_w512, cache_swa_w2048, state_s5, state_mlstm_c, state_mlstm_n, state_mlstm_m, cache_mem_kv, mem_prompt, norm_mix, w_in, ssd_conv_w, ssd_conv_b, ssd_dt_bias, ssd_a_log, ssd_d, ssd_norm, rel_bias, s5_a_re, s5_a_im, s5_log_dt, s5_b_re, s5_b_im, s5_c_re, s5_c_im, s5_d, s5_w_glu, mlstm_i_bias, mlstm_f_bias, mlstm_norm, w_br_ssd, w_br_swa, w_br_s5, w_br_mlstm, w_out, norm_xa, norm_mem, xa_wq, xa_wk, xa_wv, xa_wo, norm_mlp, w_ff1, w_ff2, norm_final):
    weights = dict(norm_mix=norm_mix, w_in=w_in, ssd_conv_w=ssd_conv_w, ssd_conv_b=ssd_conv_b,
                   ssd_dt_bias=ssd_dt_bias, ssd_a_log=ssd_a_log, ssd_d=ssd_d, ssd_norm=ssd_norm,
                   s5_d=s5_d, s5_w_glu=s5_w_glu, mlstm_i_bias=mlstm_i_bias, mlstm_f_bias=mlstm_f_bias,
                   mlstm_norm=mlstm_norm, w_br_ssd=w_br_ssd, w_br_swa=w_br_swa, w_br_s5=w_br_s5,
                   w_br_mlstm=w_br_mlstm, w_out=w_out, norm_xa=norm_xa, norm_mem=norm_mem, xa_wq=xa_wq,
                   xa_wk=xa_wk, xa_wv=xa_wv, xa_wo=xa_wo, norm_mlp=norm_mlp, w_ff1=w_ff1, w_ff2=w_ff2)
    nb_p, seq, _ = x_prompt.shape
    nb_s, t_new, _ = x_sample.shape
    depth = w_in.shape[0]
    n_mem = mem_prompt.shape[1]
    rows_s = SAMPLE_ROWS
    n_p = nb_p * seq
    n_s = nb_s * rows_s
    ta = n_p + n_s
    assert seq % (SWA_BLOCK * SWA_PATTERN[-1][1]) == 0 and ta % TOKEN_TILE == 0 and t_new <= rows_s
    caches = (cache_swa_w128, cache_swa_w512, cache_swa_w2048)

    xs_pad = jnp.pad(x_sample, ((0, 0), (0, rows_s - t_new), (0, 0))).reshape(n_s, D_MODEL)
    x = jnp.concatenate([x_prompt.reshape(n_p, D_MODEL), xs_pad], axis=0)

    prompt_bias = [_swa_prompt_bias(rel_bias, g, dil) for g, (_, dil) in enumerate(SWA_PATTERN)]
    sample_bias = [_swa_sample_bias(rel_bias, g, win, dil, caches[g].shape[2], t_new)
                   for g, (win, dil) in enumerate(SWA_PATTERN)]

    zeros = lambda *s: jnp.zeros(s, F32)
    outs = {k: [] for k in ("ssd_p", "ssd_s", "conv_p", "conv_s", "swa0_p", "swa0_s", "swa1_p", "swa1_s",
                            "swa2_p", "swa2_s", "s5_p", "s5_s", "c_p", "c_s", "n_p", "n_s", "m_p", "m_s", "kv_p")}
    ssd_q, ml_q = 128, 128
    nchunk = seq // S5_CHUNK

    for l in range(depth):
        lw = _layer_weights(l, weights)
        p_ssd, p_swa, p_s5, p_ml, p_small, p_small_t = _inproj(
            x, lw["norm_mix"], lw["w_ssd"], lw["w_swa"], lw["w_s5"], lw["w_ml"], lw["w_small"], lw["w_small_t"])
        small_t_s = jnp.transpose(p_small[n_p:, :SMALL_ROWS].reshape(nb_s, rows_s, SMALL_ROWS), (0, 2, 1))

        y_ssd, st_p = _ssd_call(p_ssd, p_small, p_small_t, zeros(nb_p, 8, SSD_CONV_DIM),
                                zeros(nb_p, SSD_HEADS, SSD_HEAD_DIM, SSD_STATE), lw,
                                nb=nb_p, nc=seq // ssd_q, q=ssd_q, valid=ssd_q, row0=0, prev=None)
        conv_init_s = jnp.pad(state_ssd_conv[l], ((0, 0), (8 - (SSD_CONV - 1), 0), (0, 0)))
        y_ssd, st_s = _ssd_call(p_ssd, p_small, small_t_s, conv_init_s, state_ssd[l], lw,
                                nb=nb_s, nc=1, q=rows_s, valid=t_new, row0=n_p, prev=y_ssd)
        outs["ssd_p"].append(st_p)
        outs["ssd_s"].append(st_s)
        xbc_p = p_ssd[:n_p, SSD_D_INNER:].reshape(nb_p, seq, SSD_CONV_DIM)
        xbc_s = p_ssd[n_p:, SSD_D_INNER:].reshape(nb_s, rows_s, SSD_CONV_DIM)[:, :t_new]
        outs["conv_p"].append(xbc_p[:, seq - (SSD_CONV - 1):])
        outs["conv_s"].append(jnp.concatenate([state_ssd_conv[l], xbc_s], axis=1)[:, -(SSD_CONV - 1):])

        swa_acts = []
        for g, (win, dil) in enumerate(SWA_PATTERN):
            o_g, lse_g = _swa_prompt_call(p_swa, prompt_bias[g], grp=g, dil=dil, nb=nb_p, seq=seq)
            cache = caches[g][l]
            width = cache.shape[1]
            o_g, lse_g, new_cache = _swa_sample_call(
                p_swa, cache.reshape(nb_s, width, 2 * SWA_GROUP_WIDTH), sample_bias[g][0], sample_bias[g][1],
                o_g, lse_g, grp=g, nb=nb_s, row0=n_p, t_new=t_new)
            swa_acts += [o_g, lse_g]
            keep = min(win, seq)
            kcol = SWA_WIDTH + g * SWA_GROUP_WIDTH
            vcol = 2 * SWA_WIDTH + g * SWA_GROUP_WIDTH
            pk = p_swa[:n_p, kcol:kcol + SWA_GROUP_WIDTH].reshape(nb_p, seq, SWA_GROUP_HEADS, SWA_HEAD_DIM)
            pv = p_swa[:n_p, vcol:vcol + SWA_GROUP_WIDTH].reshape(nb_p, seq, SWA_GROUP_HEADS, SWA_HEAD_DIM)
            outs["swa%d_p" % g].append(jnp.stack([pk[:, seq - keep:], pv[:, seq - keep:]], axis=2))
            outs["swa%d_s" % g].append(new_cache.reshape(cache.shape))

        tab = _s5_tables(s5_a_re[l], s5_a_im[l], s5_log_dt[l], s5_b_re[l], s5_b_im[l], s5_c_re[l], s5_c_im[l], t_new)
        u_p = jnp.transpose(p_s5[:n_p].reshape(nb_p, nchunk, S5_CHUNK, S5_GROUPS, S5_GROUP), (0, 3, 1, 2, 4))
        u_p = u_p.reshape(nb_p, S5_GROUPS, nchunk, S5_CHUNK * S5_GROUP)
        y5_p, h_p = _s5_call(u_p, zeros(nb_p, S5_GROUPS, 1, 2 * S5_STATE), tab, scan=True)
        y5_p = jnp.transpose(y5_p.reshape(nb_p, S5_GROUPS, nchunk, S5_CHUNK, S5_GROUP), (0, 2, 3, 1, 4))
        u_s = jnp.transpose(p_s5[n_p:].reshape(nb_s, rows_s, S5_GROUPS, S5_GROUP), (2, 0, 1, 3))
        u_s = u_s.reshape(1, S5_GROUPS, nb_s, rows_s * S5_GROUP)
        h0_s = jnp.transpose(state_s5[l], (1, 0, 3, 2)).reshape(1, S5_GROUPS, nb_s, 2 * S5_STATE)
        y5_s, h_s = _s5_call(u_s, h0_s, tab, scan=False)
        y5_s = jnp.transpose(y5_s.reshape(S5_GROUPS, nb_s, rows_s, S5_GROUP), (1, 2, 0, 3))
        y_s5 = jnp.concatenate([y5_p.reshape(n_p, S5_WIDTH), y5_s.reshape(n_s, S5_WIDTH)], axis=0)
        outs["s5_p"].append(jnp.transpose(h_p.reshape(nb_p, S5_GROUPS, 2, S5_STATE), (0, 1, 3, 2)))
        outs["s5_s"].append(jnp.transpose(h_s.reshape(S5_GROUPS, nb_s, 2, S5_STATE), (1, 0, 3, 2)))

        vec0 = zeros(nb_p, ML_HEADS, 1, ML_HEAD_DIM)
        y_ml, c_p, nn_p, m_p = _mlstm_call(p_ml, p_small, p_small_t, zeros(nb_p, ML_HEADS, ML_HEAD_DIM, ML_HEAD_DIM),
                                           vec0, vec0, lw, nb=nb_p, nc=seq // ml_q, q=ml_q, valid=ml_q, row0=0,
                                           prev=None)
        m_init = jnp.broadcast_to(state_mlstm_m[l][:, :, None, None], (nb_s, ML_HEADS, 1, ML_HEAD_DIM))
        y_ml, c_s, nn_s, m_s = _mlstm_call(p_ml, p_small, small_t_s, state_mlstm_c[l],
                                           state_mlstm_n[l][:, :, None, :], m_init, lw,
                                           nb=nb_s, nc=1, q=rows_s, valid=t_new, row0=n_p, prev=y_ml)
        for key, val in (("c_p", c_p), ("c_s", c_s), ("n_p", nn_p[:, :, 0]), ("n_s", nn_s[:, :, 0]),
                         ("m_p", m_p[:, :, 0, 0]), ("m_s", m_s[:, :, 0, 0])):
            outs[key].append(val)

        x = _merge_call(x, (y_ssd, *swa_acts, y_s5, p_s5, y_ml), lw)

        kv_p = _norm_matmul(mem_prompt.reshape(nb_p * n_mem, D_MODEL), lw["norm_mem"], lw["xa_wkv"], n_mem)
        kv_p = kv_p.reshape(nb_p, n_mem, 2 * D_MODEL)
        outs["kv_p"].append(kv_p.reshape(nb_p, n_mem, 2, XA_HEADS, XA_HEAD_DIM))
        x_new = _xattn_call(x, kv_p, lw, tm=TOKEN_TILE, n_tiles=n_p // TOKEN_TILE,
                            tiles_per_seq=seq // TOKEN_TILE, row0=0, prev=None)
        x = _xattn_call(x, cache_mem_kv[l].reshape(nb_s, n_mem, 2 * D_MODEL), lw, tm=rows_s, n_tiles=nb_s,
                        tiles_per_seq=1, row0=n_p, prev=x_new)

        x = _ffn_call(x, lw)

    y = _rmsnorm_call(x, norm_final[None, :], TOKEN_TILE)
    y_prompt = y[:n_p].reshape(nb_p, seq, D_MODEL)
    y_sample = y[n_p:].reshape(nb_s, rows_s, D_MODEL)[:, :t_new]
    st = lambda k: jnp.stack(outs[k])
    return (y_prompt, y_sample, st("ssd_p"), st("ssd_s"), st("conv_p"), st("conv_s"), st("swa0_p"), st("swa0_s"),
            st("swa1_p"), st("swa1_s"), st("swa2_p"), st("swa2_s"), st("s5_p"), st("s5_s"), st("c_p"), st("c_s"),
            st("n_p"), st("n_s"), st("m_p"), st("m_s"), st("kv_p"))
```

```python
import functools
import math

import numpy as np
import jax
import jax.numpy as jnp
from jax import lax
from jax.experimental import pallas as pl
from jax.experimental.pallas import tpu as pltpu

F32 = jnp.float32
BF16 = jnp.bfloat16

D_MODEL = 1024
RMS_EPS = 1e-6
N_BRANCH = 4

SSD_D_INNER = 512
SSD_HEAD_DIM = 64
SSD_HEADS = 8
SSD_GROUPS = 2
SSD_STATE = 64
SSD_CONV = 4
SSD_CONV_DIM = SSD_D_INNER + 2 * SSD_GROUPS * SSD_STATE
SSD_SEG = SSD_D_INNER + SSD_CONV_DIM

SWA_PATTERN = ((128, 1), (512, 4), (2048, 16))
SWA_GROUP_HEADS = 4
SWA_HEAD_DIM = 64
SWA_GROUP_WIDTH = SWA_GROUP_HEADS * SWA_HEAD_DIM
SWA_WIDTH = 3 * SWA_GROUP_WIDTH
SWA_BLOCK = 128
REL_BUCKETS = 32
REL_MAX_EXACT = 16
REL_MAX_DIST = 2048

S5_WIDTH = 512
S5_GROUP = 16
S5_GROUPS = 32
S5_STATE = 64
S5_CHUNK = 16
S5_LANES = 128
S5_SUPER = S5_WIDTH // S5_LANES
S5_SUPER_GROUPS = S5_GROUPS // S5_SUPER
S5_SUPER_STATE = 2 * S5_SUPER_GROUPS * S5_STATE
S5_DOUBLINGS = 12

ML_WIDTH = 512
ML_HEADS = 4
ML_HEAD_DIM = 128

XA_HEADS = 4
XA_HEAD_DIM = 256
D_FF = 4096

MAIN_SSD = (0, SSD_SEG)
MAIN_SWA = (SSD_SEG, SSD_SEG + 3 * SWA_WIDTH)
MAIN_S5 = (MAIN_SWA[1], MAIN_SWA[1] + S5_WIDTH)
MAIN_ML = (MAIN_S5[1], MAIN_S5[1] + 4 * ML_WIDTH)
MAIN_COLS = MAIN_ML[1]
SMALL_COLS = 128
SMALL_ROWS = 16

SAMPLE_ROWS = 16
TOKEN_TILE = 256
MASKED = -1e30

VMEM_LIMIT = 56 * 1024 * 1024

IN_SIZES = (SSD_D_INNER, SSD_CONV_DIM, SSD_HEADS, SWA_WIDTH, SWA_WIDTH, SWA_WIDTH, S5_WIDTH,
            ML_WIDTH, ML_WIDTH, ML_WIDTH, ML_WIDTH, ML_HEADS, ML_HEADS, N_BRANCH * D_MODEL)
IN_OFFS = tuple(int(v) for v in np.concatenate([[0], np.cumsum(IN_SIZES)]))


def _params(n_axes):
    return pltpu.CompilerParams(dimension_semantics=("arbitrary",) * n_axes, vmem_limit_bytes=VMEM_LIMIT)


def _full(shape):
    nd = len(shape)
    return pl.BlockSpec(shape, lambda *_: (0,) * nd)


def _layer(a, l):
    nd = a.ndim
    return pl.BlockSpec((None,) + a.shape[1:], lambda *_: (l,) + (0,) * (nd - 1))


def _with_aliased(kern, in_specs, args, aliased):
    n = len(args)
    aliases = {}
    for arr, out_idx in aliased:
        in_specs.append(pl.BlockSpec(memory_space=pl.ANY))
        aliases[len(args)] = out_idx
        args.append(arr)
    k = len(aliased)
    if k == 0:
        return kern, aliases

    def wrapped(*refs):
        return kern(*refs[:n], *refs[n + k:])
    return wrapped, aliases


def _dot(a, b):
    return jnp.dot(a.astype(BF16), b.astype(BF16), preferred_element_type=F32)


def _dot_nt(a, b):
    return lax.dot_general(a.astype(BF16), b.astype(BF16), (((1,), (1,)), ((), ())), preferred_element_type=F32)


def _dot_tn(a, b):
    return lax.dot_general(a.astype(BF16), b.astype(BF16), (((0,), (0,)), ((), ())), preferred_element_type=F32)


def _split3(x):
    hi = x.astype(BF16)
    r = x - hi.astype(F32)
    mid = r.astype(BF16)
    lo = (r - mid.astype(F32)).astype(BF16)
    return hi, mid, lo


def _dot_exact_l(ones_bf16, x):
    hi, mid, lo = _split3(x)
    f = lambda p: jnp.dot(ones_bf16, p, preferred_element_type=F32)
    return f(hi) + f(mid) + f(lo)


def _dot_exact_r(x, ones_bf16):
    hi, mid, lo = _split3(x)
    f = lambda p: jnp.dot(p, ones_bf16, preferred_element_type=F32)
    return f(hi) + f(mid) + f(lo)


def _rms(x, g):
    return x * lax.rsqrt(jnp.mean(x * x, axis=-1, keepdims=True) + RMS_EPS) * g


def _sigmoid(x):
    return 1.0 / (1.0 + jnp.exp(-x))


def _softplus(x):
    return jnp.maximum(x, 0.0) + jnp.log(1.0 + jnp.exp(-jnp.abs(x)))


def _tri(q, lower):
    r = lax.broadcasted_iota(jnp.int32, (q, q), 0)
    c = lax.broadcasted_iota(jnp.int32, (q, q), 1)
    return (r >= c) if lower else (r <= c)


def _inproj_kernel(x_ref, g_ref, wm_ref, ws_ref, wst_ref, oa_ref, ob_ref, oc_ref, od_ref, os_ref, ost_ref):
    h = _rms(x_ref[...], g_ref[...]).astype(BF16)
    for o_ref, (lo, hi) in ((oa_ref, MAIN_SSD), (ob_ref, MAIN_SWA), (oc_ref, MAIN_S5), (od_ref, MAIN_ML)):
        o_ref[...] = jnp.dot(h, wm_ref[:, lo:hi], preferred_element_type=F32)
    os_ref[...] = jnp.dot(h, ws_ref[...], preferred_element_type=F32)
    ost_ref[0] = lax.dot_general(wst_ref[...], h, (((1,), (1,)), ((), ())), preferred_element_type=F32)


def _inproj(x, wts, l):
    ta = x.shape[0]
    tm = TOKEN_TILE
    widths = [hi - lo for lo, hi in (MAIN_SSD, MAIN_SWA, MAIN_S5, MAIN_ML)] + [SMALL_COLS]
    row = lambda n: pl.BlockSpec((tm, n), lambda i: (i, 0))
    ws = (wts["norm_mix"], wts["w_main"], wts["w_small"], wts["w_small_t"])
    return pl.pallas_call(
        _inproj_kernel,
        grid=(ta // tm,),
        in_specs=[row(D_MODEL)] + [_layer(w, l) for w in ws],
        out_specs=[row(n) for n in widths] + [pl.BlockSpec((1, SMALL_ROWS, tm), lambda i: (0, 0, i))],
        out_shape=[jax.ShapeDtypeStruct((ta, n), F32) for n in widths]
        + [jax.ShapeDtypeStruct((1, SMALL_ROWS, ta), F32)],
        compiler_params=_params(1),
        name="inproj",
    )(x, *ws)


def _ssd_kernel(p_ref, sm_ref, smt_ref, cinit_ref, sinit_ref, cw_ref, cb_ref, dtb_c_ref, dtb_r_ref,
                alog_c_ref, alog_r_ref, dskip_ref, g_ref, y_ref, sout_ref, ext_sc, st_sc, *, q, valid):
    c = pl.program_id(1)

    @pl.when(c == 0)
    def _():
        ext_sc[0:8, :] = cinit_ref[...]
        st_sc[...] = sinit_ref[...]

    ext_sc[8:8 + q, :] = p_ref[:, SSD_D_INNER:SSD_SEG]
    conv = cb_ref[...] + cw_ref[0:1, :] * ext_sc[5:5 + q, :]
    for j in range(1, SSD_CONV):
        conv = conv + cw_ref[j:j + 1, :] * ext_sc[5 + j:5 + j + q, :]
    ext_sc[0:8, :] = ext_sc[q:q + 8, :]
    xbc = conv * _sigmoid(conv)
    xs = xbc[:, :SSD_D_INNER]
    z = p_ref[:, :SSD_D_INNER]

    dt_c = _softplus(sm_ref[:, 0:SSD_HEADS] + dtb_c_ref[...])
    dt_r = _softplus(smt_ref[0, 0:SSD_HEADS, :] + dtb_r_ref[...])
    if valid < q:
        dt_c = jnp.where(lax.broadcasted_iota(jnp.int32, dt_c.shape, 0) < valid, dt_c, 0.0)
        dt_r = jnp.where(lax.broadcasted_iota(jnp.int32, dt_r.shape, 1) < valid, dt_r, 0.0)
    da_c = dt_c * (-jnp.exp(alog_c_ref[...]))
    da_r = dt_r * (-jnp.exp(alog_r_ref[...]))
    causal = _tri(q, True)
    cum_c = _dot_exact_l(jnp.where(causal, 1.0, 0.0).astype(BF16), da_c)
    cum_r = _dot_exact_r(da_r, jnp.where(_tri(q, False), 1.0, 0.0).astype(BF16))

    rep = SSD_HEADS // SSD_GROUPS
    ys = []
    for h in range(SSD_HEADS):
        grp = h // rep
        b_g = xbc[:, SSD_D_INNER + grp * SSD_STATE:SSD_D_INNER + (grp + 1) * SSD_STATE]
        c_off = SSD_D_INNER + SSD_GROUPS * SSD_STATE
        c_g = xbc[:, c_off + grp * SSD_STATE:c_off + (grp + 1) * SSD_STATE]
        cb = _dot_nt(c_g, b_g)
        x_h = xs[:, h * SSD_HEAD_DIM:(h + 1) * SSD_HEAD_DIM]
        cc = cum_c[:, h:h + 1]
        seg = jnp.where(causal, cc - cum_r[h:h + 1, :], MASKED)
        w = cb * jnp.exp(seg) * dt_r[h:h + 1, :]
        s_h = st_sc[h]
        y_h = _dot(w, x_h) + jnp.exp(cc) * _dot_nt(c_g, s_h)
        last = cum_c[q - 1:q, h:h + 1]
        tail = jnp.exp(last - cc) * dt_c[:, h:h + 1]
        st_sc[h] = s_h * jnp.exp(last) + _dot_tn(x_h * tail, b_g)
        ys.append(y_h)
    y = jnp.concatenate(ys, axis=1)
    y = (y + dskip_ref[...] * xs) * (z * _sigmoid(z))
    y_ref[...] = _rms(y, g_ref[...])

    @pl.when(c == pl.num_programs(1) - 1)
    def _():
        sout_ref[0] = st_sc[...]


def _seq_specs(q, nc, blk0, p_small_t):
    rows = lambda n: pl.BlockSpec((q, n), lambda b, c: (blk0 + b * nc + c, 0))
    if p_small_t.shape[0] == 1:
        smt_spec = pl.BlockSpec((1, SMALL_ROWS, q), lambda b, c: (0, 0, blk0 + b * nc + c))
    else:
        smt_spec = pl.BlockSpec((1, SMALL_ROWS, q), lambda b, c: (b, 0, 0))
    return rows, smt_spec


def _state_spec(a, sl):
    nd = a.ndim
    return pl.BlockSpec((None, None) + a.shape[2:], lambda b, c: (sl, b) + (0,) * (nd - 2))


def _ssd_call(p_ssd, p_small, p_small_t, conv_init, state_init, sl, wts, l, *, nb, nc, q, valid, row0, prev):
    ta = p_ssd.shape[0]
    rows, smt_spec = _seq_specs(q, nc, row0 // q, p_small_t)
    ws = tuple(wts[k] for k in ("ssd_conv_w", "ssd_conv_b", "dtb_c", "dtb_r", "alog_c", "alog_r", "ssd_dskip",
                                "ssd_norm"))
    in_specs = [rows(SSD_SEG), rows(SMALL_COLS), smt_spec, _state_spec(conv_init, sl), _state_spec(state_init, sl)]
    in_specs += [_layer(w, l) for w in ws]
    args = [p_ssd, p_small, p_small_t, conv_init, state_init, *ws]
    kern, aliases = _with_aliased(functools.partial(_ssd_kernel, q=q, valid=valid), in_specs, args,
                                  [] if prev is None else [(prev, 0)])
    return pl.pallas_call(
        kern,
        grid=(nb, nc),
        in_specs=in_specs,
        out_specs=[rows(SSD_D_INNER),
                   pl.BlockSpec((1, SSD_HEADS, SSD_HEAD_DIM, SSD_STATE), lambda b, c: (b, 0, 0, 0))],
        out_shape=[jax.ShapeDtypeStruct((ta, SSD_D_INNER), F32),
                   jax.ShapeDtypeStruct((nb, SSD_HEADS, SSD_HEAD_DIM, SSD_STATE), F32)],
        scratch_shapes=[pltpu.VMEM((q + 8, SSD_CONV_DIM), F32),
                        pltpu.VMEM((SSD_HEADS, SSD_HEAD_DIM, SSD_STATE), F32)],
        input_output_aliases=aliases,
        compiler_params=_params(2),
        name="ssd_q%d" % q,
    )(*args)


def _mlstm_kernel(p_ref, sm_ref, smt_ref, cinit_ref, ninit_ref, minit_ref, ib_c_ref, ib_r_ref, fb_c_ref, fb_r_ref,
                  g_ref, y_ref, cout_ref, nout_ref, mout_ref, c_sc, n_sc, m_sc, *, q, valid):
    c = pl.program_id(1)

    @pl.when(c == 0)
    def _():
        c_sc[...] = cinit_ref[...]
        n_sc[...] = ninit_ref[...]
        m_sc[...] = minit_ref[...]

    w = ML_WIDTH
    i0 = SSD_HEADS
    f0 = SSD_HEADS + ML_HEADS
    ig_c = sm_ref[:, i0:i0 + ML_HEADS] + ib_c_ref[...]
    fg_c = sm_ref[:, f0:f0 + ML_HEADS] + fb_c_ref[...]
    ig_r = smt_ref[0, i0:i0 + ML_HEADS, :] + ib_r_ref[...]
    fg_r = smt_ref[0, f0:f0 + ML_HEADS, :] + fb_r_ref[...]
    lf_c = -_softplus(-fg_c)
    lf_r = -_softplus(-fg_r)
    if valid < q:
        ok_c = lax.broadcasted_iota(jnp.int32, ig_c.shape, 0) < valid
        ok_r = lax.broadcasted_iota(jnp.int32, ig_r.shape, 1) < valid
        ig_c = jnp.where(ok_c, ig_c, MASKED)
        ig_r = jnp.where(ok_r, ig_r, MASKED)
        lf_c = jnp.where(ok_c, lf_c, 0.0)
        lf_r = jnp.where(ok_r, lf_r, 0.0)
    causal = _tri(q, True)
    b_c = _dot_exact_l(jnp.where(causal, 1.0, 0.0).astype(BF16), lf_c)
    b_r = _dot_exact_r(lf_r, jnp.where(_tri(q, False), 1.0, 0.0).astype(BF16))

    scale = ML_HEAD_DIM ** -0.5
    ys = []
    for h in range(ML_HEADS):
        hs = slice(h * ML_HEAD_DIM, (h + 1) * ML_HEAD_DIM)
        q_h = p_ref[:, hs]
        k_h = p_ref[:, w + h * ML_HEAD_DIM:w + (h + 1) * ML_HEAD_DIM] * scale
        v_h = p_ref[:, 2 * w + h * ML_HEAD_DIM:2 * w + (h + 1) * ML_HEAD_DIM]
        o_h = p_ref[:, 3 * w + h * ML_HEAD_DIM:3 * w + (h + 1) * ML_HEAD_DIM]
        m_prev = m_sc[h][:, 0:1]
        bc = b_c[:, h:h + 1]
        intra = jnp.where(causal, bc - b_r[h:h + 1, :] + ig_r[h:h + 1, :], MASKED)
        inter = bc + m_prev
        m_t = jnp.maximum(inter, jnp.max(intra, axis=1, keepdims=True))
        wgt = _dot_nt(q_h, k_h) * jnp.exp(intra - m_t)
        w_inter = jnp.exp(inter - m_t)
        c_h = c_sc[h]
        n_h = n_sc[h]
        num = _dot(wgt, v_h) + w_inter * _dot(q_h, c_h)
        den = jnp.sum(wgt, axis=1, keepdims=True) + w_inter * jnp.sum(q_h * n_h, axis=1, keepdims=True)
        hh = num / jnp.maximum(jnp.abs(den), jnp.exp(-m_t))
        m_new = m_t[q - 1:q, :]
        b_last = b_c[q - 1:q, h:h + 1]
        wk = jnp.exp(b_last - bc + ig_c[:, h:h + 1] - m_new)
        decay = jnp.exp(b_last + m_prev - m_new)
        kw = k_h * wk
        c_sc[h] = decay * c_h + _dot_tn(kw, v_h)
        n_sc[h] = decay * n_h + jnp.sum(kw, axis=0, keepdims=True)
        m_sc[h] = jnp.broadcast_to(m_new, (1, ML_HEAD_DIM))
        hn = hh * lax.rsqrt(jnp.mean(hh * hh, axis=-1, keepdims=True) + RMS_EPS) * g_ref[:, hs]
        ys.append(_sigmoid(o_h) * hn)
    y_ref[...] = jnp.concatenate(ys, axis=1)

    @pl.when(c == pl.num_programs(1) - 1)
    def _():
        cout_ref[0] = c_sc[...]
        nout_ref[0] = n_sc[...]
        mout_ref[0] = m_sc[...]


def _mlstm_call(p_ml, p_small, p_small_t, c_init, n_init, m_init, sl, wts, l, *, nb, nc, q, valid, row0, prev):
    ta = p_ml.shape[0]
    rows, smt_spec = _seq_specs(q, nc, row0 // q, p_small_t)
    ws = tuple(wts[k] for k in ("ib_c", "ib_r", "fb_c", "fb_r", "mlstm_norm"))
    c_spec = pl.BlockSpec((1, ML_HEADS, ML_HEAD_DIM, ML_HEAD_DIM), lambda b, c: (b, 0, 0, 0))
    v_spec = pl.BlockSpec((1, ML_HEADS, 1, ML_HEAD_DIM), lambda b, c: (b, 0, 0, 0))
    in_specs = [rows(4 * ML_WIDTH), rows(SMALL_COLS), smt_spec, _state_spec(c_init, sl), _state_spec(n_init, sl),
                _state_spec(m_init, sl)]
    in_specs += [_layer(w, l) for w in ws]
    args = [p_ml, p_small, p_small_t, c_init, n_init, m_init, *ws]
    kern, aliases = _with_aliased(functools.partial(_mlstm_kernel, q=q, valid=valid), in_specs, args,
                                  [] if prev is None else [(prev, 0)])
    vec = jax.ShapeDtypeStruct((nb, ML_HEADS, 1, ML_HEAD_DIM), F32)
    return pl.pallas_call(
        kern,
        grid=(nb, nc),
        in_specs=in_specs,
        out_specs=[rows(ML_WIDTH), c_spec, v_spec, v_spec],
        out_shape=[jax.ShapeDtypeStruct((ta, ML_WIDTH), F32),
                   jax.ShapeDtypeStruct((nb, ML_HEADS, ML_HEAD_DIM, ML_HEAD_DIM), F32), vec, vec],
        scratch_shapes=[pltpu.VMEM((ML_HEADS, ML_HEAD_DIM, ML_HEAD_DIM), F32),
                        pltpu.VMEM((ML_HEADS, 1, ML_HEAD_DIM), F32),
                        pltpu.VMEM((ML_HEADS, 1, ML_HEAD_DIM), F32)],
        input_output_aliases=aliases,
        compiler_params=_params(2),
        name="mlstm_q%d" % q,
    )(*args)


def _s5_kernel(u_ref, h0_ref, tz_ref, sm_ref, cm_ref, pa_ref, pb_ref, va_ref, vb_ref, y_ref, hout_ref, *, rows, scan):
    half = S5_SUPER_STATE // 2
    ucat = jnp.concatenate([u_ref[pl.ds(s, rows, stride=S5_CHUNK), :] for s in range(S5_CHUNK)], axis=1)
    ub = ucat.astype(BF16)

    def cmul(a, b, x):
        return a * x + b * pltpu.roll(x, half, axis=1)

    contrib = jnp.dot(ub, sm_ref[...], preferred_element_type=F32)
    h0 = h0_ref[0]
    carried = cmul(va_ref[...], vb_ref[...], h0)
    if scan:
        ridx = lax.broadcasted_iota(jnp.int32, (rows, S5_SUPER_STATE), 0)
        x = contrib + jnp.where(ridx == 0, carried, 0.0)
        k = 0
        while (1 << k) < rows:
            s = 1 << k
            shifted = jnp.where(ridx >= s, pltpu.roll(x, s, axis=0), 0.0)
            x = x + cmul(pa_ref[k:k + 1, :], pb_ref[k:k + 1, :], shifted)
            k += 1
        hprev = jnp.where(ridx == 0, h0, pltpu.roll(x, 1, axis=0))
        hout_ref[0] = x[rows - 1:rows, :]
    else:
        x = contrib + carried
        hprev = h0
        hout_ref[0] = x
    y = jnp.dot(ub, tz_ref[...], preferred_element_type=F32) + _dot(hprev, cm_ref[...])
    for t in range(S5_CHUNK):
        y_ref[pl.ds(t, rows, stride=S5_CHUNK), :] = y[:, t * S5_LANES:(t + 1) * S5_LANES]


def _s5_call(p_s5, h0, tab, l, *, nseq, rows, scan, row0, prev):
    ta = p_s5.shape[0]
    blk_rows = rows * S5_CHUNK
    blk0 = row0 // blk_rows
    hrows = h0.shape[2]
    tok = pl.BlockSpec((blk_rows, S5_LANES), lambda sb, b: (blk0 + b, sb))
    hspec = pl.BlockSpec((None, 1, hrows, S5_SUPER_STATE), lambda sb, b: (sb, b, 0, 0))
    va, vb = (tab["va16"], tab["vb16"]) if scan else (tab["va"], tab["vb"])
    tabs = (tab["tz"], tab["sm16"] if scan else tab["sm"], tab["cm"], tab["pa"], tab["pb"], va, vb)
    per_sb = lambda a: pl.BlockSpec((None, None) + a.shape[2:], lambda sb, b: (l, sb) + (0,) * (a.ndim - 2))
    in_specs = [tok, hspec] + [per_sb(a) for a in tabs]
    args = [p_s5, h0, *tabs]
    kern, aliases = _with_aliased(functools.partial(_s5_kernel, rows=rows, scan=scan), in_specs, args,
                                  [] if prev is None else [(prev, 0)])
    return pl.pallas_call(
        kern,
        grid=(S5_SUPER, nseq),
        in_specs=in_specs,
        out_specs=[tok, hspec],
        out_shape=[jax.ShapeDtypeStruct((ta, S5_WIDTH), F32), jax.ShapeDtypeStruct(h0.shape, F32)],
        input_output_aliases=aliases,
        compiler_params=_params(2),
        name="s5_scan" if scan else "s5_step",
    )(*args)


def _s5_tables(a_re, a_im, log_dt, b_re, b_im, c_re, c_im, valid):
    hp = lax.Precision.HIGHEST
    q = S5_CHUNK
    depth = a_re.shape[0]
    nsb, ng = S5_SUPER, S5_SUPER_GROUPS
    dt = jnp.exp(log_dt)[..., None]
    mag = jnp.exp(a_re * dt)
    ab_re = mag * jnp.cos(a_im * dt)
    ab_im = mag * jnp.sin(a_im * dt)
    inv = 1.0 / (a_re * a_re + a_im * a_im)
    co_re = ((ab_re - 1.0) * a_re + ab_im * a_im) * inv
    co_im = (ab_im * a_re - (ab_re - 1.0) * a_im) * inv

    def cmul(x, y):
        return x[0] * y[0] - x[1] * y[1], x[0] * y[1] + x[1] * y[0]

    pw = [(jnp.ones_like(ab_re), jnp.zeros_like(ab_re))]
    for _ in range(q):
        pw.append(cmul(pw[-1], (ab_re, ab_im)))
    bt_re = co_re[..., None] * b_re - co_im[..., None] * b_im
    bt_im = co_re[..., None] * b_im + co_im[..., None] * b_re
    cp_re = jnp.stack([c_re * p[0][:, :, None, :] - c_im * p[1][:, :, None, :] for p in pw], axis=1)
    cp_im = jnp.stack([c_re * p[1][:, :, None, :] + c_im * p[0][:, :, None, :] for p in pw], axis=1)
    kern = (jnp.einsum("ljgcn,lgnk->ljgck", cp_re, bt_re, precision=hp)
            - jnp.einsum("ljgcn,lgnk->ljgck", cp_im, bt_im, precision=hp))
    onehot = np.zeros((q, q, q + 1), np.float32)
    for s in range(q):
        for t in range(s, q):
            onehot[s, t, t - s] = 1.0
    tz = jnp.einsum("stj,ljgck->lgsktc", onehot, kern, precision=hp)
    eye = np.eye(ng, dtype=np.float32)

    def blockdiag(x, n_in, n_out):
        x = x.reshape((depth, nsb, ng) + x.shape[2:])
        in_ax = "abcd"[:n_in]
        out_ax = "wxyz"[:n_out]
        spec = "lsg%s%s,gh->ls%sg%s%sh%s" % (in_ax, out_ax, in_ax[0], in_ax[1:], out_ax[0], out_ax[1:])
        y = jnp.einsum(spec, x, eye)
        rows = int(np.prod(x.shape[3:3 + n_in])) * ng
        cols = int(np.prod(x.shape[3 + n_in:])) * ng
        return y.reshape(depth, nsb, rows, cols)

    tz_b = blockdiag(tz, 2, 2).astype(BF16)

    def state_in(nvalid):
        parts = []
        for s in range(q):
            if s < nvalid:
                pr, pi = pw[nvalid - 1 - s]
                sr = pr[..., None] * bt_re - pi[..., None] * bt_im
                si = pr[..., None] * bt_im + pi[..., None] * bt_re
                parts.append(jnp.stack([sr, si], axis=2))
            else:
                parts.append(jnp.zeros((depth, S5_GROUPS, 2, S5_STATE, S5_GROUP), F32))
        m = jnp.stack(parts, axis=2)
        m = jnp.transpose(m, (0, 1, 2, 5, 3, 4))
        return blockdiag(m, 2, 2).astype(BF16)

    cm = jnp.stack([jnp.transpose(cp_re[:, 1:], (0, 2, 4, 1, 3)),
                    -jnp.transpose(cp_im[:, 1:], (0, 2, 4, 1, 3))], axis=2)
    cm_b = blockdiag(cm, 2, 2).astype(BF16)

    def packed(p):
        pr = p[0].reshape(depth, nsb, ng * S5_STATE)
        pi = p[1].reshape(depth, nsb, ng * S5_STATE)
        return jnp.concatenate([pr, pr], axis=-1), jnp.concatenate([-pi, pi], axis=-1)

    doubling = [pw[q]]
    for _ in range(S5_DOUBLINGS - 1):
        doubling.append(cmul(doubling[-1], doubling[-1]))
    pa = jnp.stack([packed(p)[0] for p in doubling], axis=2)
    pb = jnp.stack([packed(p)[1] for p in doubling], axis=2)
    va16, vb16 = packed(pw[q])
    va, vb = packed(pw[valid])
    ex = lambda a: a[:, :, None, :]
    return {"tz": tz_b, "sm16": state_in(q), "sm": state_in(valid), "cm": cm_b, "pa": pa, "pb": pb,
            "va16": ex(va16), "vb16": ex(vb16), "va": ex(va), "vb": ex(vb)}


def _t5_bucket(dist):
    dist = np.asarray(dist)
    large = REL_MAX_EXACT + (np.log(np.maximum(dist, 1) / REL_MAX_EXACT)
                             / math.log(REL_MAX_DIST / REL_MAX_EXACT)
                             * (REL_BUCKETS - REL_MAX_EXACT)).astype(np.int32)
    large = np.minimum(large, REL_BUCKETS - 1)
    return np.where(dist < REL_MAX_EXACT, dist, large).astype(np.int32)


def _bias_steps(rel_bias, grp, dil):
    heads = slice(grp * SWA_GROUP_HEADS, (grp + 1) * SWA_GROUP_HEADS)
    buckets = _t5_bucket(dil * np.arange(SWA_BLOCK + 1))
    onehot = np.zeros((SWA_BLOCK + 1, REL_BUCKETS), np.float32)
    onehot[np.arange(SWA_BLOCK + 1), buckets] = 1.0
    steps = jnp.einsum("jb,bh->hj", onehot, rel_bias[:, heads], precision=lax.Precision.HIGHEST)
    return steps.astype(F32)


def _softmax_pieces(logits):
    m = logits[0].max(axis=1, keepdims=True)
    for s in logits[1:]:
        m = jnp.maximum(m, s.max(axis=1, keepdims=True))
    ps = [jnp.exp(s - m) for s in logits]
    den = ps[0].sum(axis=1, keepdims=True)
    for p in ps[1:]:
        den = den + p.sum(axis=1, keepdims=True)
    return ps, den, m + jnp.log(den)


def _swa_prompt_kernel(q_ref, k_ref, v_ref, kp_ref, vp_ref, bias_ref, o_ref, lse_ref, kvt_ref, *, dil):
    n = SWA_BLOCK
    pair = pl.program_id(1)
    first = pl.program_id(2) == 0
    scale = SWA_HEAD_DIM ** -0.5

    def residue(r):
        if dil == 1:
            sl = slice(None)
        else:
            sl = pl.ds(r, n, stride=dil)
        qq, kk, vv, kp, vp = q_ref[sl, :], k_ref[sl, :], v_ref[sl, :], kp_ref[sl, :], vp_ref[sl, :]
        outs, lses = [], []
        for j in range(2):
            hs = slice(j * SWA_HEAD_DIM, (j + 1) * SWA_HEAD_DIM)
            bias = bias_ref[2 * pair + j]
            bias_prev = jnp.where(first, MASKED, bias[:, 0:n])
            q_h = qq[:, hs] * scale
            ps, den, lse = _softmax_pieces([_dot_nt(q_h, kk[:, hs]) + bias[:, n:2 * n],
                                            _dot_nt(q_h, kp[:, hs]) + bias_prev])
            outs.append((_dot(ps[0], vv[:, hs]) + _dot(ps[1], vp[:, hs])) / den)
            lses.append(jnp.broadcast_to(lse, (n, SWA_HEAD_DIM)))
        o_ref[sl, :] = jnp.concatenate(outs, axis=1)
        lse_ref[sl, :] = jnp.concatenate(lses, axis=1)

    if dil == 1:
        residue(0)
    else:
        def body(r, carry):
            residue(r)
            return carry
        lax.fori_loop(0, dil, body, 0)
    kvt_ref[0] = k_ref[...].T
    kvt_ref[1] = v_ref[...].T


def _swa_prompt_call(p_swa, bias, kvt_prev, l, depth, *, grp, dil, nb, seq):
    ta = p_swa.shape[0]
    sb = SWA_BLOCK * dil
    nsb = seq // sb
    lanes = 2 * SWA_HEAD_DIM
    npair = SWA_GROUP_HEADS // 2
    cur = lambda col: pl.BlockSpec((sb, lanes), lambda b, p, c: (b * nsb + c, 2 * col + p))
    prv = lambda col: pl.BlockSpec((sb, lanes), lambda b, p, c: (b * nsb + jnp.maximum(c - 1, 0), 2 * col + p))
    out = pl.BlockSpec((sb, lanes), lambda b, p, c: (b * nsb + c, p))
    kvt_spec = pl.BlockSpec((None, None, 2, None, lanes, sb), lambda b, p, c: (l, b, 0, p, 0, 0))
    shape = jax.ShapeDtypeStruct((ta, SWA_GROUP_WIDTH), F32)
    in_specs = [cur(grp), cur(3 + grp), cur(6 + grp), prv(3 + grp), prv(6 + grp), _full(bias.shape)]
    args = [p_swa, p_swa, p_swa, p_swa, p_swa, bias]
    kern, aliases = _with_aliased(functools.partial(_swa_prompt_kernel, dil=dil), in_specs, args,
                                  [] if kvt_prev is None else [(kvt_prev, 2)])
    return pl.pallas_call(
        kern,
        grid=(nb, npair, nsb),
        in_specs=in_specs,
        out_specs=[out, out, kvt_spec],
        out_shape=[shape, shape, jax.ShapeDtypeStruct((depth, nb, 2, npair, lanes, sb), F32)],
        input_output_aliases=aliases,
        compiler_params=_params(3),
        name="swa_prompt_d%d" % dil,
    )(*args)


def _swa_prompt_bias(steps):
    n = SWA_BLOCK
    period = 3 * n + 1
    f = jnp.concatenate([steps[:, ::-1], jnp.full((steps.shape[0], period - (n + 1)), MASKED, F32)], axis=1)
    tiled = jnp.tile(f, (1, n))[:, :n * (period - 1)]
    return tiled.reshape(steps.shape[0], n, period - 1)[:, :, :2 * n]


def _swa_sample_kernel(q_ref, k_ref, v_ref, buf_ref, bias_buf_ref, bias_new_ref, o_ref, lse_ref, cache_ref, *,
                       width, t_new):
    scale = SWA_HEAD_DIM ** -0.5
    w = SWA_GROUP_WIDTH
    qq, kn, vn = q_ref[...], k_ref[...], v_ref[...]
    outs, lses = [], []
    for h in range(SWA_GROUP_HEADS):
        hs = slice(h * SWA_HEAD_DIM, (h + 1) * SWA_HEAD_DIM)
        k_t = buf_ref[h * SWA_HEAD_DIM:(h + 1) * SWA_HEAD_DIM, :]
        v_t = buf_ref[w + h * SWA_HEAD_DIM:w + (h + 1) * SWA_HEAD_DIM, :]
        q_h = qq[:, hs] * scale
        ps, den, lse = _softmax_pieces([_dot(q_h, k_t) + bias_buf_ref[h], _dot_nt(q_h, kn[:, hs]) + bias_new_ref[h]])
        outs.append((_dot_nt(ps[0], v_t) + _dot(ps[1], vn[:, hs])) / den)
        lses.append(jnp.broadcast_to(lse, (SAMPLE_ROWS, SWA_HEAD_DIM)))
    o_ref[...] = jnp.concatenate(outs, axis=1)
    lse_ref[...] = jnp.concatenate(lses, axis=1)
    new_t = jnp.concatenate([kn, vn], axis=1).T
    cache_ref[...] = pltpu.roll(buf_ref[...], width - t_new, axis=1)
    cache_ref[:, width - t_new:width] = new_t[:, 0:t_new]


def _swa_sample_call(p_swa, cache_t, bias_buf, bias_new, prev_o, prev_lse, prev_cache, l, *, grp, nb, row0, t_new):
    ta = p_swa.shape[0]
    width = cache_t.shape[3]
    w = SWA_GROUP_WIDTH
    blk0 = row0 // SAMPLE_ROWS
    tok = lambda col: pl.BlockSpec((SAMPLE_ROWS, w), lambda b: (blk0 + b, col))
    cache_spec = pl.BlockSpec((None, None, 2 * w, width), lambda b: (l, b, 0, 0))
    in_specs = [tok(grp), tok(3 + grp), tok(6 + grp), cache_spec, _full(bias_buf.shape), _full(bias_new.shape)]
    args = [p_swa, p_swa, p_swa, cache_t, bias_buf, bias_new]
    aliased = [(prev_o, 0), (prev_lse, 1)] + ([] if prev_cache is None else [(prev_cache, 2)])
    kern, aliases = _with_aliased(functools.partial(_swa_sample_kernel, width=width, t_new=t_new), in_specs, args,
                                  aliased)
    return pl.pallas_call(
        kern,
        grid=(nb,),
        in_specs=in_specs,
        out_specs=[tok(0), tok(0), cache_spec],
        out_shape=[jax.ShapeDtypeStruct((ta, w), F32), jax.ShapeDtypeStruct((ta, w), F32),
                   jax.ShapeDtypeStruct(cache_t.shape, F32)],
        input_output_aliases=aliases,
        compiler_params=_params(1),
        name="swa_sample_w%d" % width,
    )(*args)


def _swa_sample_bias(steps, window, dil, width, t_new):
    n = window // dil
    nh = steps.shape[0]
    g = jnp.pad(steps[:, :, None], ((0, 0), (0, 0), (0, dil - 1)), constant_values=MASKED).reshape(nh, (n + 1) * dil)
    g = jnp.pad(g, ((0, 0), (0, width + SAMPLE_ROWS)), constant_values=MASKED)
    masked_row = jnp.full((nh, 1, width), MASKED, F32)
    rows = [g[:, t + 1:t + 1 + width][:, None, ::-1] if t < t_new else masked_row for t in range(SAMPLE_ROWS)]
    b_buf = jnp.concatenate(rows, axis=1)
    new_rows = []
    for t in range(SAMPLE_ROWS):
        if t < t_new:
            row = jnp.concatenate([g[:, 0:t + 1][:, ::-1], jnp.full((nh, SAMPLE_ROWS - t - 1), MASKED, F32)], axis=1)
        else:
            row = jnp.full((nh, SAMPLE_ROWS), MASKED, F32)
        new_rows.append(row[:, None, :])
    return b_buf, jnp.concatenate(new_rows, axis=1)


def _merge_kernel(x_ref, yssd_ref, o0_ref, l0_ref, o1_ref, l1_ref, o2_ref, l2_ref, ys5_ref, u_ref, yml_ref,
                  g_ref, wg_ref, wssd_ref, wswa_ref, ws5_ref, wml_ref, wglu_ref, wout_ref, d_ref, out_ref):
    x = x_ref[...]
    h = _rms(x, g_ref[...]).astype(BF16)
    l0, l1, l2 = l0_ref[...], l1_ref[...], l2_ref[...]
    m = jnp.maximum(jnp.maximum(l0, l1), l2)
    e0, e1, e2 = jnp.exp(l0 - m), jnp.exp(l1 - m), jnp.exp(l2 - m)
    y_swa = (e0 * o0_ref[...] + e1 * o1_ref[...] + e2 * o2_ref[...]) / (e0 + e1 + e2)
    y5 = ys5_ref[...] + d_ref[...] * u_ref[...]
    y_s5 = y5 * _sigmoid(_dot(y5, wglu_ref[...]))
    branches = ((yssd_ref[...], wssd_ref), (y_swa, wswa_ref), (y_s5, ws5_ref), (yml_ref[...], wml_ref))
    merged = None
    for i, (y, w_ref) in enumerate(branches):
        gate = _sigmoid(jnp.dot(h, wg_ref[:, i * D_MODEL:(i + 1) * D_MODEL], preferred_element_type=F32))
        term = gate * _dot(y, w_ref[...])
        merged = term if merged is None else merged + term
    out_ref[...] = x + _dot(merged, wout_ref[...])


def _merge_call(x, acts, wts, l):
    ta = x.shape[0]
    tm = TOKEN_TILE
    row = lambda a: pl.BlockSpec((tm, a.shape[1]), lambda i: (i, 0))
    ws = tuple(wts[k] for k in ("norm_mix", "w_gate", "w_br_ssd", "w_br_swa", "w_br_s5", "w_br_mlstm", "s5_w_glu",
                                "w_out", "s5_d"))
    return pl.pallas_call(
        _merge_kernel,
        grid=(ta // tm,),
        in_specs=[row(a) for a in (x, *acts)] + [_layer(w, l) for w in ws],
        out_specs=row(x),
        out_shape=jax.ShapeDtypeStruct(x.shape, F32),
        compiler_params=_params(1),
        name="merge",
    )(x, *acts, *ws)


def _norm_matmul_kernel(x_ref, g_ref, w_ref, o_ref):
    o_ref[...] = jnp.dot(_rms(x_ref[...], g_ref[...]).astype(BF16), w_ref[...], preferred_element_type=F32)


def _norm_matmul(x, g, w, l, tm):
    rows = x.shape[0]
    return pl.pallas_call(
        _norm_matmul_kernel,
        grid=(rows // tm,),
        in_specs=[pl.BlockSpec((tm, x.shape[1]), lambda i: (i, 0)), _layer(g, l), _layer(w, l)],
        out_specs=pl.BlockSpec((tm, w.shape[2]), lambda i: (i, 0)),
        out_shape=jax.ShapeDtypeStruct((rows, w.shape[2]), F32),
        compiler_params=_params(1),
        name="norm_matmul",
    )(x, g, w)


def _rms_kernel(x_ref, g_ref, o_ref):
    o_ref[...] = _rms(x_ref[...], g_ref[...])


def _rmsnorm_call(x, g, tm):
    rows = x.shape[0]
    return pl.pallas_call(
        _rms_kernel,
        grid=(rows // tm,),
        in_specs=[pl.BlockSpec((tm, x.shape[1]), lambda i: (i, 0)), _full(g.shape)],
        out_specs=pl.BlockSpec((tm, x.shape[1]), lambda i: (i, 0)),
        out_shape=jax.ShapeDtypeStruct(x.shape, F32),
        compiler_params=_params(1),
        name="final_norm",
    )(x, g)


def _xattn_kernel(x_ref, kv_ref, g_ref, wq_ref, wo_ref, out_ref, *, kv_rows):
    x = x_ref[...]
    q = jnp.dot(_rms(x, g_ref[...]).astype(BF16), wq_ref[...], preferred_element_type=F32)
    scale = XA_HEAD_DIM ** -0.5
    n_mem = kv_ref.shape[0] // (2 * XA_HEADS) if kv_rows else kv_ref.shape[0]
    outs = []
    for h in range(XA_HEADS):
        hs = slice(h * XA_HEAD_DIM, (h + 1) * XA_HEAD_DIM)
        if kv_rows:
            k_h = kv_ref[h * n_mem:(h + 1) * n_mem, :]
            v_h = kv_ref[(XA_HEADS + h) * n_mem:(XA_HEADS + h + 1) * n_mem, :]
        else:
            k_h = kv_ref[:, hs]
            v_h = kv_ref[:, D_MODEL + h * XA_HEAD_DIM:D_MODEL + (h + 1) * XA_HEAD_DIM]
        logits = _dot_nt(q[:, hs], k_h) * scale
        m = logits.max(axis=1, keepdims=True)
        p = jnp.exp(logits - m)
        outs.append(_dot(p / p.sum(axis=1, keepdims=True), v_h))
    out_ref[...] = x + _dot(jnp.concatenate(outs, axis=1), wo_ref[...])


def _xattn_call(x, kv, kv_layer, wts, l, *, tm, n_tiles, tiles_per_seq, row0, kv_rows, prev):
    ta = x.shape[0]
    blk0 = row0 // tm
    row = pl.BlockSpec((tm, D_MODEL), lambda i: (blk0 + i, 0))
    ws = tuple(wts[k] for k in ("norm_xa", "xa_wq", "xa_wo"))
    in_specs = [row, pl.BlockSpec((None, None) + kv.shape[2:], lambda i: (kv_layer, i // tiles_per_seq, 0, 0))]
    in_specs += [_layer(w, l) for w in ws]
    args = [x, kv, *ws]
    kern, aliases = _with_aliased(functools.partial(_xattn_kernel, kv_rows=kv_rows), in_specs, args,
                                  [] if prev is None else [(prev, 0)])
    return pl.pallas_call(
        kern,
        grid=(n_tiles,),
        in_specs=in_specs,
        out_specs=row,
        out_shape=jax.ShapeDtypeStruct((ta, D_MODEL), F32),
        input_output_aliases=aliases,
        compiler_params=_params(1),
        name="xattn_tm%d" % tm,
    )(*args)


def _ffn_kernel(x_ref, g_ref, w1_ref, w2_ref, out_ref):
    x = x_ref[...]
    a = jnp.dot(_rms(x, g_ref[...]).astype(BF16), w1_ref[...], preferred_element_type=F32)
    a = jnp.square(jnp.maximum(a, 0.0))
    out_ref[...] = x + _dot(a, w2_ref[...])


def _ffn_call(x, wts, l):
    ta = x.shape[0]
    tm = TOKEN_TILE
    row = pl.BlockSpec((tm, D_MODEL), lambda i: (i, 0))
    ws = tuple(wts[k] for k in ("norm_mlp", "w_ff1", "w_ff2"))
    return pl.pallas_call(
        _ffn_kernel,
        grid=(ta // tm,),
        in_specs=[row] + [_layer(w, l) for w in ws],
        out_specs=row,
        out_shape=jax.ShapeDtypeStruct(x.shape, F32),
        compiler_params=_params(1),
        name="ffn",
    )(x, *ws)


def _prep_weights(w):
    o = IN_OFFS
    w_in = w["w_in"]
    small = jnp.concatenate([w_in[:, :, o[2]:o[3]], w_in[:, :, o[11]:o[13]]], axis=2)
    row = lambda v: v[:, None, :].astype(F32)
    col = lambda v: v[:, :, None].astype(F32)
    bf = lambda v: v.astype(BF16)
    return {
        "norm_mix": row(w["norm_mix"]),
        "w_main": jnp.concatenate([w_in[:, :, o[0]:o[2]], w_in[:, :, o[3]:o[11]]], axis=2).astype(BF16),
        "w_gate": bf(w_in[:, :, o[13]:o[14]]),
        "w_small": jnp.pad(small, ((0, 0), (0, 0), (0, SMALL_COLS - small.shape[2]))).astype(BF16),
        "w_small_t": jnp.transpose(small, (0, 2, 1)).astype(BF16),
        "ssd_conv_w": w["ssd_conv_w"], "ssd_conv_b": row(w["ssd_conv_b"]),
        "dtb_c": row(w["ssd_dt_bias"]), "dtb_r": col(w["ssd_dt_bias"]),
        "alog_c": row(w["ssd_a_log"]), "alog_r": col(w["ssd_a_log"]),
        "ssd_dskip": row(jnp.repeat(w["ssd_d"], SSD_HEAD_DIM, axis=1)), "ssd_norm": row(w["ssd_norm"]),
        "ib_c": row(w["mlstm_i_bias"]), "ib_r": col(w["mlstm_i_bias"]),
        "fb_c": row(w["mlstm_f_bias"]), "fb_r": col(w["mlstm_f_bias"]),
        "mlstm_norm": row(w["mlstm_norm"]),
        "s5_d": row(w["s5_d"]), "s5_w_glu": bf(w["s5_w_glu"]),
        "w_br_ssd": bf(w["w_br_ssd"]), "w_br_swa": bf(w["w_br_swa"]), "w_br_s5": bf(w["w_br_s5"]),
        "w_br_mlstm": bf(w["w_br_mlstm"]), "w_out": bf(w["w_out"]),
        "norm_xa": row(w["norm_xa"]), "xa_wq": bf(w["xa_wq"]), "xa_wo": bf(w["xa_wo"]),
        "norm_mem": row(w["norm_mem"]),
        "xa_wkv": jnp.concatenate([w["xa_wk"], w["xa_wv"]], axis=2).astype(BF16),
        "norm_mlp": row(w["norm_mlp"]), "w_ff1": bf(w["w_ff1"]), "w_ff2": bf(w["w_ff2"]),
    }


def kernel(x_prompt, x_sample, state_ssd, state_ssd_conv, cache_swa_w128, cache_swa_w512, cache_swa_w2048, state_s5, state_mlstm_c, state_mlstm_n, state_mlstm_m, cache_mem_kv, mem_prompt, norm_mix, w_in, ssd_conv_w, ssd_conv_b, ssd_dt_bias, ssd_a_log, ssd_d, ssd_norm, rel_bias, s5_a_re, s5_a_im, s5_log_dt, s5_b_re, s5_b_im, s5_c_re, s5_c_im, s5_d, s5_w_glu, mlstm_i_bias, mlstm_f_bias, mlstm_norm, w_br_ssd, w_br_swa, w_br_s5, w_br_mlstm, w_out, norm_xa, norm_mem, xa_wq, xa_wk, xa_wv, xa_wo, norm_mlp, w_ff1, w_ff2, norm_final):
    wts = _prep_weights(dict(
        norm_mix=norm_mix, w_in=w_in, ssd_conv_w=ssd_conv_w, ssd_conv_b=ssd_conv_b, ssd_dt_bias=ssd_dt_bias,
        ssd_a_log=ssd_a_log, ssd_d=ssd_d, ssd_norm=ssd_norm, s5_d=s5_d, s5_w_glu=s5_w_glu,
        mlstm_i_bias=mlstm_i_bias, mlstm_f_bias=mlstm_f_bias, mlstm_norm=mlstm_norm, w_br_ssd=w_br_ssd,
        w_br_swa=w_br_swa, w_br_s5=w_br_s5, w_br_mlstm=w_br_mlstm, w_out=w_out, norm_xa=norm_xa, norm_mem=norm_mem,
        xa_wq=xa_wq, xa_wk=xa_wk, xa_wv=xa_wv, xa_wo=xa_wo, norm_mlp=norm_mlp, w_ff1=w_ff1, w_ff2=w_ff2))
    nb_p, seq, _ = x_prompt.shape
    nb_s, t_new, _ = x_sample.shape
    depth = w_in.shape[0]
    n_mem = mem_prompt.shape[1]
    rows_s = SAMPLE_ROWS
    n_p = nb_p * seq
    n_s = nb_s * rows_s
    ta = n_p + n_s
    assert seq % (SWA_BLOCK * SWA_PATTERN[-1][1]) == 0 and ta % TOKEN_TILE == 0 and t_new <= rows_s
    assert n_p % n_s == 0 and (seq // S5_CHUNK) & (seq // S5_CHUNK - 1) == 0
    caches = (cache_swa_w128, cache_swa_w512, cache_swa_w2048)

    xs_pad = jnp.pad(x_sample, ((0, 0), (0, rows_s - t_new), (0, 0))).reshape(n_s, D_MODEL)
    x = jnp.concatenate([x_prompt.reshape(n_p, D_MODEL), xs_pad], axis=0)

    steps = [_bias_steps(rel_bias, g, dil) for g, (_, dil) in enumerate(SWA_PATTERN)]
    prompt_bias = [_swa_prompt_bias(s) for s in steps]
    sample_bias = [_swa_sample_bias(steps[g], win, dil, caches[g].shape[2], t_new)
                   for g, (win, dil) in enumerate(SWA_PATTERN)]
    caches_t = [jnp.transpose(c, (0, 1, 3, 4, 5, 2)).reshape(depth, nb_s, 2 * SWA_GROUP_WIDTH, c.shape[2])
                for c in caches]
    mem_kv_s = jnp.transpose(cache_mem_kv, (0, 1, 3, 4, 2, 5)).reshape(depth, nb_s, 2 * XA_HEADS * n_mem, XA_HEAD_DIM)
    s5_tab = _s5_tables(s5_a_re, s5_a_im, s5_log_dt, s5_b_re, s5_b_im, s5_c_re, s5_c_im, t_new)
    s5_h0_s = jnp.transpose(state_s5.reshape(depth, nb_s, S5_SUPER, S5_SUPER_GROUPS, S5_STATE, 2),
                            (0, 2, 1, 5, 3, 4)).reshape(depth, S5_SUPER, 1, nb_s, S5_SUPER_STATE)

    zeros = lambda *s: jnp.zeros(s, F32)
    zero_conv = zeros(1, nb_p, 8, SSD_CONV_DIM)
    zero_ssd = zeros(1, nb_p, SSD_HEADS, SSD_HEAD_DIM, SSD_STATE)
    zero_c = zeros(1, nb_p, ML_HEADS, ML_HEAD_DIM, ML_HEAD_DIM)
    zero_vec = zeros(1, nb_p, ML_HEADS, 1, ML_HEAD_DIM)
    zero_s5 = zeros(S5_SUPER, nb_p, 1, S5_SUPER_STATE)
    conv_init_s = jnp.pad(state_ssd_conv, ((0, 0), (0, 0), (8 - (SSD_CONV - 1), 0), (0, 0)))
    n_init_s = state_mlstm_n[:, :, :, None, :]
    m_init_s = jnp.broadcast_to(state_mlstm_m[:, :, :, None, None], (depth, nb_s, ML_HEADS, 1, ML_HEAD_DIM))

    outs = {k: [] for k in ("ssd_p", "ssd_s", "conv_p", "conv_s", "s5_p", "s5_s", "c_p", "c_s", "n_p", "n_s",
                            "m_p", "m_s", "kv_p")}
    kvt_p = [None] * len(SWA_PATTERN)
    cache_out = [None] * len(SWA_PATTERN)
    ssd_q, ml_q = 128, 128
    nchunk = seq // S5_CHUNK

    for l in range(depth):
        p_ssd, p_swa, p_s5, p_ml, p_small, p_small_t = _inproj(x, wts, l)
        small_t_s = jnp.transpose(p_small[n_p:, :SMALL_ROWS].reshape(nb_s, rows_s, SMALL_ROWS), (0, 2, 1))

        y_ssd, st_p = _ssd_call(p_ssd, p_small, p_small_t, zero_conv, zero_ssd, 0, wts, l,
                                nb=nb_p, nc=seq // ssd_q, q=ssd_q, valid=ssd_q, row0=0, prev=None)
        y_ssd, st_s = _ssd_call(p_ssd, p_small, small_t_s, conv_init_s, state_ssd, l, wts, l,
                                nb=nb_s, nc=1, q=rows_s, valid=t_new, row0=n_p, prev=y_ssd)
        outs["ssd_p"].append(st_p)
        outs["ssd_s"].append(st_s)
        xbc_p = p_ssd[:n_p, SSD_D_INNER:].reshape(nb_p, seq, SSD_CONV_DIM)
        xbc_s = p_ssd[n_p:, SSD_D_INNER:].reshape(nb_s, rows_s, SSD_CONV_DIM)[:, :t_new]
        outs["conv_p"].append(xbc_p[:, seq - (SSD_CONV - 1):])
        outs["conv_s"].append(jnp.concatenate([state_ssd_conv[l], xbc_s], axis=1)[:, -(SSD_CONV - 1):])

        swa_acts = []
        for g, (win, dil) in enumerate(SWA_PATTERN):
            o_g, lse_g, kvt_p[g] = _swa_prompt_call(p_swa, prompt_bias[g], kvt_p[g], l, depth, grp=g, dil=dil,
                                                    nb=nb_p, seq=seq)
            o_g, lse_g, cache_out[g] = _swa_sample_call(p_swa, caches_t[g], sample_bias[g][0], sample_bias[g][1],
                                                        o_g, lse_g, cache_out[g], l, grp=g, nb=nb_s, row0=n_p,
                                                        t_new=t_new)
            swa_acts += [o_g, lse_g]

        y_s5, h_p = _s5_call(p_s5, zero_s5, s5_tab, l, nseq=nb_p, rows=nchunk, scan=True, row0=0, prev=None)
        y_s5, h_s = _s5_call(p_s5, s5_h0_s[l], s5_tab, l, nseq=1, rows=nb_s, scan=False, row0=n_p, prev=y_s5)
        outs["s5_p"].append(h_p)
        outs["s5_s"].append(h_s)

        y_ml, c_p, nn_p, m_p = _mlstm_call(p_ml, p_small, p_small_t, zero_c, zero_vec, zero_vec, 0, wts, l,
                                           nb=nb_p, nc=seq // ml_q, q=ml_q, valid=ml_q, row0=0, prev=None)
        y_ml, c_s, nn_s, m_s = _mlstm_call(p_ml, p_small, small_t_s, state_mlstm_c, n_init_s, m_init_s, l, wts, l,
                                           nb=nb_s, nc=1, q=rows_s, valid=t_new, row0=n_p, prev=y_ml)
        for key, val in (("c_p", c_p), ("c_s", c_s), ("n_p", nn_p[:, :, 0]), ("n_s", nn_s[:, :, 0]),
                         ("m_p", m_p[:, :, 0, 0]), ("m_s", m_s[:, :, 0, 0])):
            outs[key].append(val)

        x = _merge_call(x, (y_ssd, *swa_acts, y_s5, p_s5, y_ml), wts, l)

        kv_p = _norm_matmul(mem_prompt.reshape(nb_p * n_mem, D_MODEL), wts["norm_mem"], wts["xa_wkv"], l, n_mem)
        outs["kv_p"].append(kv_p.reshape(nb_p, n_mem, 2, XA_HEADS, XA_HEAD_DIM))
        x_new = _xattn_call(x, kv_p.reshape(1, nb_p, n_mem, 2 * D_MODEL), 0, wts, l, tm=TOKEN_TILE,
                            n_tiles=n_p // TOKEN_TILE, tiles_per_seq=seq // TOKEN_TILE, row0=0, kv_rows=False,
                            prev=None)
        x = _xattn_call(x, mem_kv_s, l, wts, l, tm=rows_s, n_tiles=nb_s, tiles_per_seq=1, row0=n_p, kv_rows=True,
                        prev=x_new)

        x = _ffn_call(x, wts, l)

    y = _rmsnorm_call(x, norm_final[None, :], TOKEN_TILE)
    y_prompt = y[:n_p].reshape(nb_p, seq, D_MODEL)
    y_sample = y[n_p:].reshape(nb_s, rows_s, D_MODEL)[:, :t_new]
    st = lambda k: jnp.stack(outs[k])

    def s5_state(k, nb):
        h = st(k).reshape(depth, S5_SUPER, nb, 2, S5_SUPER_GROUPS, S5_STATE)
        return jnp.transpose(h, (0, 2, 1, 4, 5, 3)).reshape(depth, nb, S5_GROUPS, S5_STATE, 2)

    swa_out = []
    for g in range(len(SWA_PATTERN)):
        width_p = kvt_p[g].shape[-1]
        kp = kvt_p[g].reshape(depth, nb_p, 2, SWA_GROUP_HEADS, SWA_HEAD_DIM, width_p)
        swa_out.append(jnp.transpose(kp, (0, 1, 5, 2, 3, 4)))
        cs = cache_out[g].reshape(depth, nb_s, 2, SWA_GROUP_HEADS, SWA_HEAD_DIM, caches[g].shape[2])
        swa_out.append(jnp.transpose(cs, (0, 1, 5, 2, 3, 4)))
    return (y_prompt, y_sample, st("ssd_p"), st("ssd_s"), st("conv_p"), st("conv_s"), *swa_out,
            s5_state("s5_p", nb_p), s5_state("s5_s", nb_s), st("c_p"), st("c_s"),
            st("n_p"), st("n_s"), st("m_p"), st("m_s"), st("kv_p"))
```

```python
import functools
import math

import numpy as np
import jax
import jax.numpy as jnp
from jax import lax
from jax.experimental import pallas as pl
from jax.experimental.pallas import tpu as pltpu

F32 = jnp.float32
BF16 = jnp.bfloat16

D_MODEL = 1024
RMS_EPS = 1e-6
N_BRANCH = 4

SSD_D_INNER = 512
SSD_HEAD_DIM = 64
SSD_HEADS = 8
SSD_GROUPS = 2
SSD_STATE = 64
SSD_CONV = 4
SSD_CONV_DIM = SSD_D_INNER + 2 * SSD_GROUPS * SSD_STATE
SSD_SEG = SSD_D_INNER + SSD_CONV_DIM

SWA_PATTERN = ((128, 1), (512, 4), (2048, 16))
SWA_GROUP_HEADS = 4
SWA_HEAD_DIM = 64
SWA_GROUP_WIDTH = SWA_GROUP_HEADS * SWA_HEAD_DIM
SWA_WIDTH = 3 * SWA_GROUP_WIDTH
SWA_BLOCK = 128
REL_BUCKETS = 32
REL_MAX_EXACT = 16
REL_MAX_DIST = 2048

S5_WIDTH = 512
S5_GROUP = 16
S5_GROUPS = 32
S5_STATE = 64
S5_CHUNK = 16
S5_LANES = 128
S5_SUPER = S5_WIDTH // S5_LANES
S5_SUPER_GROUPS = S5_GROUPS // S5_SUPER
S5_SUPER_STATE = 2 * S5_SUPER_GROUPS * S5_STATE
S5_DOUBLINGS = 12

ML_WIDTH = 512
ML_HEADS = 4
ML_HEAD_DIM = 128

XA_HEADS = 4
XA_HEAD_DIM = 256
D_FF = 4096

MAIN_SSD = (0, SSD_SEG)
MAIN_SWA = (SSD_SEG, SSD_SEG + 3 * SWA_WIDTH)
MAIN_S5 = (MAIN_SWA[1], MAIN_SWA[1] + S5_WIDTH)
MAIN_ML = (MAIN_S5[1], MAIN_S5[1] + 4 * ML_WIDTH)
MAIN_COLS = MAIN_ML[1]
SMALL_COLS = 128
SMALL_ROWS = 16

SAMPLE_ROWS = 16
TOKEN_TILE = 256
MASKED = -1e30

VMEM_LIMIT = 56 * 1024 * 1024

IN_SIZES = (SSD_D_INNER, SSD_CONV_DIM, SSD_HEADS, SWA_WIDTH, SWA_WIDTH, SWA_WIDTH, S5_WIDTH,
            ML_WIDTH, ML_WIDTH, ML_WIDTH, ML_WIDTH, ML_HEADS, ML_HEADS, N_BRANCH * D_MODEL)
IN_OFFS = tuple(int(v) for v in np.concatenate([[0], np.cumsum(IN_SIZES)]))


def _params(n_axes):
    return pltpu.CompilerParams(dimension_semantics=("arbitrary",) * n_axes, vmem_limit_bytes=VMEM_LIMIT)


def _full(shape):
    nd = len(shape)
    return pl.BlockSpec(shape, lambda *_: (0,) * nd)


def _layer(a, l):
    nd = a.ndim
    return pl.BlockSpec((None,) + a.shape[1:], lambda *_: (l,) + (0,) * (nd - 1))


def _with_aliased(kern, in_specs, args, aliased):
    n = len(args)
    aliases = {}
    for arr, out_idx in aliased:
        in_specs.append(pl.BlockSpec(memory_space=pl.ANY))
        aliases[len(args)] = out_idx
        args.append(arr)
    k = len(aliased)
    if k == 0:
        return kern, aliases

    def wrapped(*refs):
        return kern(*refs[:n], *refs[n + k:])
    return wrapped, aliases


def _dot(a, b):
    return jnp.dot(a.astype(BF16), b.astype(BF16), preferred_element_type=F32)


def _dot_nt(a, b):
    return lax.dot_general(a.astype(BF16), b.astype(BF16), (((1,), (1,)), ((), ())), preferred_element_type=F32)


def _dot_tn(a, b):
    return lax.dot_general(a.astype(BF16), b.astype(BF16), (((0,), (0,)), ((), ())), preferred_element_type=F32)


def _split3(x):
    hi = x.astype(BF16)
    r = x - hi.astype(F32)
    mid = r.astype(BF16)
    lo = (r - mid.astype(F32)).astype(BF16)
    return hi, mid, lo


def _dot_exact_l(ones_bf16, x):
    hi, mid, lo = _split3(x)
    f = lambda p: jnp.dot(ones_bf16, p, preferred_element_type=F32)
    return f(hi) + f(mid) + f(lo)


def _dot_exact_r(x, ones_bf16):
    hi, mid, lo = _split3(x)
    f = lambda p: jnp.dot(p, ones_bf16, preferred_element_type=F32)
    return f(hi) + f(mid) + f(lo)


def _rms(x, g):
    return x * lax.rsqrt(jnp.mean(x * x, axis=-1, keepdims=True) + RMS_EPS) * g


def _sigmoid(x):
    return 1.0 / (1.0 + jnp.exp(-x))


def _softplus(x):
    return jnp.maximum(x, 0.0) + jnp.log(1.0 + jnp.exp(-jnp.abs(x)))


def _tri(q, lower):
    r = lax.broadcasted_iota(jnp.int32, (q, q), 0)
    c = lax.broadcasted_iota(jnp.int32, (q, q), 1)
    return (r >= c) if lower else (r <= c)


def _inproj_kernel(x_ref, g_ref, wm_ref, ws_ref, wst_ref, oa_ref, ob_ref, oc_ref, od_ref, os_ref, ost_ref):
    h = _rms(x_ref[...], g_ref[...]).astype(BF16)
    for o_ref, (lo, hi) in ((oa_ref, MAIN_SSD), (ob_ref, MAIN_SWA), (oc_ref, MAIN_S5), (od_ref, MAIN_ML)):
        o_ref[...] = jnp.dot(h, wm_ref[:, lo:hi], preferred_element_type=F32)
    os_ref[...] = jnp.dot(h, ws_ref[...], preferred_element_type=F32)
    ost_ref[0] = lax.dot_general(wst_ref[...], h, (((1,), (1,)), ((), ())), preferred_element_type=F32)


def _inproj(x, wts, l):
    ta = x.shape[0]
    tm = TOKEN_TILE
    widths = [hi - lo for lo, hi in (MAIN_SSD, MAIN_SWA, MAIN_S5, MAIN_ML)] + [SMALL_COLS]
    row = lambda n: pl.BlockSpec((tm, n), lambda i: (i, 0))
    ws = (wts["norm_mix"], wts["w_main"], wts["w_small"], wts["w_small_t"])
    return pl.pallas_call(
        _inproj_kernel,
        grid=(ta // tm,),
        in_specs=[row(D_MODEL)] + [_layer(w, l) for w in ws],
        out_specs=[row(n) for n in widths] + [pl.BlockSpec((1, SMALL_ROWS, tm), lambda i: (0, 0, i))],
        out_shape=[jax.ShapeDtypeStruct((ta, n), F32) for n in widths]
        + [jax.ShapeDtypeStruct((1, SMALL_ROWS, ta), F32)],
        compiler_params=_params(1),
        name="inproj",
    )(x, *ws)


def _ssd_kernel(p_ref, sm_ref, smt_ref, cinit_ref, sinit_ref, cw_ref, cb_ref, dtb_c_ref, dtb_r_ref,
                alog_c_ref, alog_r_ref, dskip_ref, g_ref, y_ref, sout_ref, ext_sc, st_sc, *, q, nsub, valid):
    c = pl.program_id(1)
    rows = q * nsub

    @pl.when(c == 0)
    def _():
        ext_sc[0:8, :] = cinit_ref[...]
        st_sc[...] = sinit_ref[...]

    ext_sc[8:8 + rows, :] = p_ref[:, SSD_D_INNER:SSD_SEG]
    conv = cb_ref[...] + cw_ref[0:1, :] * ext_sc[5:5 + rows, :]
    for j in range(1, SSD_CONV):
        conv = conv + cw_ref[j:j + 1, :] * ext_sc[5 + j:5 + j + rows, :]
    ext_sc[0:8, :] = ext_sc[rows:rows + 8, :]
    xbc_all = conv * _sigmoid(conv)
    z_all = p_ref[:, :SSD_D_INNER]

    dt_c_all = _softplus(sm_ref[:, 0:SSD_HEADS] + dtb_c_ref[...])
    dt_r_all = _softplus(smt_ref[0, 0:SSD_HEADS, :] + dtb_r_ref[...])
    if valid < q:
        dt_c_all = jnp.where(lax.broadcasted_iota(jnp.int32, dt_c_all.shape, 0) < valid, dt_c_all, 0.0)
        dt_r_all = jnp.where(lax.broadcasted_iota(jnp.int32, dt_r_all.shape, 1) < valid, dt_r_all, 0.0)
    da_c_all = dt_c_all * (-jnp.exp(alog_c_ref[...]))
    da_r_all = dt_r_all * (-jnp.exp(alog_r_ref[...]))
    causal = _tri(q, True)
    lower = jnp.where(causal, 1.0, 0.0).astype(BF16)
    upper = jnp.where(_tri(q, False), 1.0, 0.0).astype(BF16)

    rep = SSD_HEADS // SSD_GROUPS
    c_off = SSD_D_INNER + SSD_GROUPS * SSD_STATE
    states = [st_sc[h] for h in range(SSD_HEADS)]
    for j in range(nsub):
        rs = slice(j * q, (j + 1) * q)
        xbc, dt_c, dt_r = xbc_all[rs], dt_c_all[rs], dt_r_all[:, rs]
        xs = xbc[:, :SSD_D_INNER]
        cum_c = _dot_exact_l(lower, da_c_all[rs])
        cum_r = _dot_exact_r(da_r_all[:, rs], upper)
        ys = []
        for h in range(SSD_HEADS):
            grp = h // rep
            b_g = xbc[:, SSD_D_INNER + grp * SSD_STATE:SSD_D_INNER + (grp + 1) * SSD_STATE]
            c_g = xbc[:, c_off + grp * SSD_STATE:c_off + (grp + 1) * SSD_STATE]
            cb = _dot_nt(c_g, b_g)
            x_h = xs[:, h * SSD_HEAD_DIM:(h + 1) * SSD_HEAD_DIM]
            cc = cum_c[:, h:h + 1]
            seg = jnp.where(causal, cc - cum_r[h:h + 1, :], MASKED)
            w = cb * jnp.exp(seg) * dt_r[h:h + 1, :]
            s_h = states[h]
            ys.append(_dot(w, x_h) + jnp.exp(cc) * _dot_nt(c_g, s_h))
            last = cum_c[q - 1:q, h:h + 1]
            tail = jnp.exp(last - cc) * dt_c[:, h:h + 1]
            states[h] = s_h * jnp.exp(last) + _dot_tn(x_h * tail, b_g)
        z = z_all[rs]
        y = (jnp.concatenate(ys, axis=1) + dskip_ref[...] * xs) * (z * _sigmoid(z))
        y_ref[rs, :] = _rms(y, g_ref[...])
    for h in range(SSD_HEADS):
        st_sc[h] = states[h]

    @pl.when(c == pl.num_programs(1) - 1)
    def _():
        sout_ref[0] = st_sc[...]


def _seq_specs(rows_blk, nc, blk0, p_small_t):
    rows = lambda n: pl.BlockSpec((rows_blk, n), lambda b, c: (blk0 + b * nc + c, 0))
    if p_small_t.shape[0] == 1:
        smt_spec = pl.BlockSpec((1, SMALL_ROWS, rows_blk), lambda b, c: (0, 0, blk0 + b * nc + c))
    else:
        smt_spec = pl.BlockSpec((1, SMALL_ROWS, rows_blk), lambda b, c: (b, 0, 0))
    return rows, smt_spec


def _state_spec(a, sl):
    nd = a.ndim
    return pl.BlockSpec((None, None) + a.shape[2:], lambda b, c: (sl, b) + (0,) * (nd - 2))


def _ssd_call(p_ssd, p_small, p_small_t, conv_init, state_init, sl, wts, l, *, nb, nc, q, nsub, valid, row0, prev):
    ta = p_ssd.shape[0]
    blk = q * nsub
    assert row0 % blk == 0 and (valid == q or nsub == 1)
    rows, smt_spec = _seq_specs(blk, nc, row0 // blk, p_small_t)
    ws = tuple(wts[k] for k in ("ssd_conv_w", "ssd_conv_b", "dtb_c", "dtb_r", "alog_c", "alog_r", "ssd_dskip",
                                "ssd_norm"))
    in_specs = [rows(SSD_SEG), rows(SMALL_COLS), smt_spec, _state_spec(conv_init, sl), _state_spec(state_init, sl)]
    in_specs += [_layer(w, l) for w in ws]
    args = [p_ssd, p_small, p_small_t, conv_init, state_init, *ws]
    kern, aliases = _with_aliased(functools.partial(_ssd_kernel, q=q, nsub=nsub, valid=valid), in_specs, args,
                                  [] if prev is None else [(prev, 0)])
    return pl.pallas_call(
        kern,
        grid=(nb, nc),
        in_specs=in_specs,
        out_specs=[rows(SSD_D_INNER),
                   pl.BlockSpec((1, SSD_HEADS, SSD_HEAD_DIM, SSD_STATE), lambda b, c: (b, 0, 0, 0))],
        out_shape=[jax.ShapeDtypeStruct((ta, SSD_D_INNER), F32),
                   jax.ShapeDtypeStruct((nb, SSD_HEADS, SSD_HEAD_DIM, SSD_STATE), F32)],
        scratch_shapes=[pltpu.VMEM((blk + 8, SSD_CONV_DIM), F32),
                        pltpu.VMEM((SSD_HEADS, SSD_HEAD_DIM, SSD_STATE), F32)],
        input_output_aliases=aliases,
        compiler_params=_params(2),
        name="ssd_q%d" % q,
    )(*args)


def _mlstm_kernel(p_ref, sm_ref, smt_ref, cinit_ref, ninit_ref, minit_ref, ib_c_ref, ib_r_ref, fb_c_ref, fb_r_ref,
                  g_ref, y_ref, cout_ref, nout_ref, mout_ref, c_sc, n_sc, m_sc, *, q, nsub, valid):
    c = pl.program_id(1)

    @pl.when(c == 0)
    def _():
        c_sc[...] = cinit_ref[...]
        n_sc[...] = ninit_ref[...]
        m_sc[...] = minit_ref[...]

    w = ML_WIDTH
    i0 = SSD_HEADS
    f0 = SSD_HEADS + ML_HEADS
    ig_c_all = sm_ref[:, i0:i0 + ML_HEADS] + ib_c_ref[...]
    fg_c = sm_ref[:, f0:f0 + ML_HEADS] + fb_c_ref[...]
    ig_r_all = smt_ref[0, i0:i0 + ML_HEADS, :] + ib_r_ref[...]
    fg_r = smt_ref[0, f0:f0 + ML_HEADS, :] + fb_r_ref[...]
    lf_c_all = -_softplus(-fg_c)
    lf_r_all = -_softplus(-fg_r)
    if valid < q:
        ok_c = lax.broadcasted_iota(jnp.int32, ig_c_all.shape, 0) < valid
        ok_r = lax.broadcasted_iota(jnp.int32, ig_r_all.shape, 1) < valid
        ig_c_all = jnp.where(ok_c, ig_c_all, MASKED)
        ig_r_all = jnp.where(ok_r, ig_r_all, MASKED)
        lf_c_all = jnp.where(ok_c, lf_c_all, 0.0)
        lf_r_all = jnp.where(ok_r, lf_r_all, 0.0)
    causal = _tri(q, True)
    lower = jnp.where(causal, 1.0, 0.0).astype(BF16)
    upper = jnp.where(_tri(q, False), 1.0, 0.0).astype(BF16)

    scale = ML_HEAD_DIM ** -0.5
    c_st = [c_sc[h] for h in range(ML_HEADS)]
    n_st = [n_sc[h] for h in range(ML_HEADS)]
    m_st = [m_sc[h][:, 0:1] for h in range(ML_HEADS)]
    for j in range(nsub):
        rs = slice(j * q, (j + 1) * q)
        ig_c, ig_r = ig_c_all[rs], ig_r_all[:, rs]
        b_c = _dot_exact_l(lower, lf_c_all[rs])
        b_r = _dot_exact_r(lf_r_all[:, rs], upper)
        ys = []
        for h in range(ML_HEADS):
            hs = slice(h * ML_HEAD_DIM, (h + 1) * ML_HEAD_DIM)
            q_h = p_ref[rs, hs]
            k_h = p_ref[rs, w + h * ML_HEAD_DIM:w + (h + 1) * ML_HEAD_DIM] * scale
            v_h = p_ref[rs, 2 * w + h * ML_HEAD_DIM:2 * w + (h + 1) * ML_HEAD_DIM]
            o_h = p_ref[rs, 3 * w + h * ML_HEAD_DIM:3 * w + (h + 1) * ML_HEAD_DIM]
            m_prev, c_h, n_h = m_st[h], c_st[h], n_st[h]
            bc = b_c[:, h:h + 1]
            intra = jnp.where(causal, bc - b_r[h:h + 1, :] + ig_r[h:h + 1, :], MASKED)
            inter = bc + m_prev
            m_t = jnp.maximum(inter, jnp.max(intra, axis=1, keepdims=True))
            wgt = _dot_nt(q_h, k_h) * jnp.exp(intra - m_t)
            w_inter = jnp.exp(inter - m_t)
            num = _dot(wgt, v_h) + w_inter * _dot(q_h, c_h)
            den = jnp.sum(wgt, axis=1, keepdims=True) + w_inter * jnp.sum(q_h * n_h, axis=1, keepdims=True)
            hh = num / jnp.maximum(jnp.abs(den), jnp.exp(-m_t))
            m_new = m_t[q - 1:q, :]
            b_last = b_c[q - 1:q, h:h + 1]
            wk = jnp.exp(b_last - bc + ig_c[:, h:h + 1] - m_new)
            decay = jnp.exp(b_last + m_prev - m_new)
            kw = k_h * wk
            c_st[h] = decay * c_h + _dot_tn(kw, v_h)
            n_st[h] = decay * n_h + jnp.sum(kw, axis=0, keepdims=True)
            m_st[h] = m_new
            hn = hh * lax.rsqrt(jnp.mean(hh * hh, axis=-1, keepdims=True) + RMS_EPS) * g_ref[:, hs]
            ys.append(_sigmoid(o_h) * hn)
        y_ref[rs, :] = jnp.concatenate(ys, axis=1)
    for h in range(ML_HEADS):
        c_sc[h] = c_st[h]
        n_sc[h] = n_st[h]
        m_sc[h] = jnp.broadcast_to(m_st[h], (1, ML_HEAD_DIM))

    @pl.when(c == pl.num_programs(1) - 1)
    def _():
        cout_ref[0] = c_sc[...]
        nout_ref[0] = n_sc[...]
        mout_ref[0] = m_sc[...]


def _mlstm_call(p_ml, p_small, p_small_t, c_init, n_init, m_init, sl, wts, l, *, nb, nc, q, nsub, valid, row0,
                prev):
    ta = p_ml.shape[0]
    blk = q * nsub
    assert row0 % blk == 0 and (valid == q or nsub == 1)
    rows, smt_spec = _seq_specs(blk, nc, row0 // blk, p_small_t)
    ws = tuple(wts[k] for k in ("ib_c", "ib_r", "fb_c", "fb_r", "mlstm_norm"))
    c_spec = pl.BlockSpec((1, ML_HEADS, ML_HEAD_DIM, ML_HEAD_DIM), lambda b, c: (b, 0, 0, 0))
    v_spec = pl.BlockSpec((1, ML_HEADS, 1, ML_HEAD_DIM), lambda b, c: (b, 0, 0, 0))
    in_specs = [rows(4 * ML_WIDTH), rows(SMALL_COLS), smt_spec, _state_spec(c_init, sl), _state_spec(n_init, sl),
                _state_spec(m_init, sl)]
    in_specs += [_layer(w, l) for w in ws]
    args = [p_ml, p_small, p_small_t, c_init, n_init, m_init, *ws]
    kern, aliases = _with_aliased(functools.partial(_mlstm_kernel, q=q, nsub=nsub, valid=valid), in_specs, args,
                                  [] if prev is None else [(prev, 0)])
    vec = jax.ShapeDtypeStruct((nb, ML_HEADS, 1, ML_HEAD_DIM), F32)
    return pl.pallas_call(
        kern,
        grid=(nb, nc),
        in_specs=in_specs,
        out_specs=[rows(ML_WIDTH), c_spec, v_spec, v_spec],
        out_shape=[jax.ShapeDtypeStruct((ta, ML_WIDTH), F32),
                   jax.ShapeDtypeStruct((nb, ML_HEADS, ML_HEAD_DIM, ML_HEAD_DIM), F32), vec, vec],
        scratch_shapes=[pltpu.VMEM((ML_HEADS, ML_HEAD_DIM, ML_HEAD_DIM), F32),
                        pltpu.VMEM((ML_HEADS, 1, ML_HEAD_DIM), F32),
                        pltpu.VMEM((ML_HEADS, 1, ML_HEAD_DIM), F32)],
        input_output_aliases=aliases,
        compiler_params=_params(2),
        name="mlstm_q%d" % q,
    )(*args)


def _s5_kernel(u_ref, h0_ref, bd_ref, smc_ref, cmc_ref, esm_ref, ecm_ref, pa_ref, pb_ref, va_ref, vb_ref,
               y_ref, hout_ref, tz_ref, sm_ref, cm_ref, *, rows, scan):
    half = S5_SUPER_STATE // 2
    q, lanes, ng = S5_CHUNK, S5_LANES, S5_SUPER_GROUPS

    @pl.when(pl.program_id(1) == 0)
    def _():
        zero = jnp.zeros((lanes, lanes), BF16)
        for s in range(q):
            for t in range(q):
                tz_ref[s * lanes:(s + 1) * lanes, t * lanes:(t + 1) * lanes] = bd_ref[t - s] if t >= s else zero

        def expand(dst_ref, compact_ref, e_ref, row_shift, col_shift, step):
            for r0 in range(0, dst_ref.shape[0], step):
                full = jnp.dot(compact_ref[r0:r0 + step, :], e_ref[...], preferred_element_type=F32)
                rg = ((lax.broadcasted_iota(jnp.int32, full.shape, 0) + r0) >> row_shift) & (ng - 1)
                cg = (lax.broadcasted_iota(jnp.int32, full.shape, 1) >> col_shift) & (ng - 1)
                dst_ref[r0:r0 + step, :] = jnp.where(rg == cg, full, 0.0).astype(BF16)

        expand(sm_ref, smc_ref, esm_ref, 4, 6, 512)
        expand(cm_ref, cmc_ref, ecm_ref, 6, 4, 256)

    ucat = jnp.concatenate([u_ref[pl.ds(s, rows, stride=S5_CHUNK), :] for s in range(S5_CHUNK)], axis=1)
    ub = ucat.astype(BF16)

    def cmul(a, b, x):
        return a * x + b * pltpu.roll(x, half, axis=1)

    contrib = jnp.dot(ub, sm_ref[...], preferred_element_type=F32)
    h0 = h0_ref[0]
    carried = cmul(va_ref[...], vb_ref[...], h0)
    if scan:
        ridx = lax.broadcasted_iota(jnp.int32, (rows, S5_SUPER_STATE), 0)
        x = contrib + jnp.where(ridx == 0, carried, 0.0)
        k = 0
        while (1 << k) < rows:
            s = 1 << k
            shifted = jnp.where(ridx >= s, pltpu.roll(x, s, axis=0), 0.0)
            x = x + cmul(pa_ref[k:k + 1, :], pb_ref[k:k + 1, :], shifted)
            k += 1
        hprev = jnp.where(ridx == 0, h0, pltpu.roll(x, 1, axis=0))
        hout_ref[0] = x[rows - 1:rows, :]
    else:
        x = contrib + carried
        hprev = h0
        hout_ref[0] = x
    y = jnp.dot(ub, tz_ref[...], preferred_element_type=F32) + _dot(hprev, cm_ref[...])
    for t in range(S5_CHUNK):
        y_ref[pl.ds(t, rows, stride=S5_CHUNK), :] = y[:, t * S5_LANES:(t + 1) * S5_LANES]


def _s5_call(p_s5, h0, tab, l, *, nseq, rows, scan, row0, prev):
    ta = p_s5.shape[0]
    blk_rows = rows * S5_CHUNK
    blk0 = row0 // blk_rows
    hrows = h0.shape[2]
    tok = pl.BlockSpec((blk_rows, S5_LANES), lambda sb, b: (blk0 + b, sb))
    hspec = pl.BlockSpec((None, 1, hrows, S5_SUPER_STATE), lambda sb, b: (sb, b, 0, 0))
    va, vb = (tab["va16"], tab["vb16"]) if scan else (tab["va"], tab["vb"])
    per_sb = lambda a: pl.BlockSpec((None, None) + a.shape[2:], lambda sb, b: (l, sb) + (0,) * (a.ndim - 2))
    tabs = [(tab["bd"], per_sb), (tab["smc16"] if scan else tab["smc"], per_sb), (tab["cmc"], per_sb),
            (tab["e_sm"], lambda a: _full(a.shape)), (tab["e_cm"], lambda a: _full(a.shape)),
            (tab["pa"], per_sb), (tab["pb"], per_sb), (va, per_sb), (vb, per_sb)]
    in_specs = [tok, hspec] + [mk(a) for a, mk in tabs]
    args = [p_s5, h0] + [a for a, _ in tabs]
    kern, aliases = _with_aliased(functools.partial(_s5_kernel, rows=rows, scan=scan), in_specs, args,
                                  [] if prev is None else [(prev, 0)])
    folded = S5_CHUNK * S5_LANES
    return pl.pallas_call(
        kern,
        grid=(S5_SUPER, nseq),
        in_specs=in_specs,
        out_specs=[tok, hspec],
        out_shape=[jax.ShapeDtypeStruct((ta, S5_WIDTH), F32), jax.ShapeDtypeStruct(h0.shape, F32)],
        scratch_shapes=[pltpu.VMEM((folded, folded), BF16), pltpu.VMEM((folded, S5_SUPER_STATE), BF16),
                        pltpu.VMEM((S5_SUPER_STATE, folded), BF16)],
        input_output_aliases=aliases,
        compiler_params=_params(2),
        name="s5_scan" if scan else "s5_step",
    )(*args)


def _s5_tables(a_re, a_im, log_dt, b_re, b_im, c_re, c_im, valid):
    hp = lax.Precision.HIGHEST
    q = S5_CHUNK
    depth = a_re.shape[0]
    nsb, ng = S5_SUPER, S5_SUPER_GROUPS
    dt = jnp.exp(log_dt)[..., None]
    mag = jnp.exp(a_re * dt)
    ab_re = mag * jnp.cos(a_im * dt)
    ab_im = mag * jnp.sin(a_im * dt)
    inv = 1.0 / (a_re * a_re + a_im * a_im)
    co_re = ((ab_re - 1.0) * a_re + ab_im * a_im) * inv
    co_im = (ab_im * a_re - (ab_re - 1.0) * a_im) * inv

    def cmul(x, y):
        return x[0] * y[0] - x[1] * y[1], x[0] * y[1] + x[1] * y[0]

    pw = [(jnp.ones_like(ab_re), jnp.zeros_like(ab_re))]
    for _ in range(q):
        pw.append(cmul(pw[-1], (ab_re, ab_im)))
    bt_re = co_re[..., None] * b_re - co_im[..., None] * b_im
    bt_im = co_re[..., None] * b_im + co_im[..., None] * b_re
    cp_re = jnp.stack([c_re * p[0][:, :, None, :] - c_im * p[1][:, :, None, :] for p in pw], axis=1)
    cp_im = jnp.stack([c_re * p[1][:, :, None, :] + c_im * p[0][:, :, None, :] for p in pw], axis=1)
    kern = (jnp.einsum("ljgcn,lgnk->ljgck", cp_re, bt_re, precision=hp)
            - jnp.einsum("ljgcn,lgnk->ljgck", cp_im, bt_im, precision=hp))
    eye = np.eye(ng, dtype=np.float32)
    bd = jnp.einsum("ljsgck,gh->lsjgkhc", kern.reshape(depth, q + 1, nsb, ng, S5_GROUP, S5_GROUP), eye)
    bd = bd.reshape(depth, nsb, q + 1, S5_LANES, S5_LANES).astype(BF16)

    def compact(x, rows):
        x = x.reshape((depth, nsb, ng) + x.shape[2:])
        x = jnp.transpose(x, (0, 1, 3, 2) + tuple(range(4, x.ndim)))
        return x.reshape(depth, nsb, rows, -1).astype(BF16)

    def state_in(nvalid):
        parts = []
        for s in range(q):
            if s < nvalid:
                pr, pi = pw[nvalid - 1 - s]
                sr = pr[..., None] * bt_re - pi[..., None] * bt_im
                si = pr[..., None] * bt_im + pi[..., None] * bt_re
                parts.append(jnp.stack([sr, si], axis=2))
            else:
                parts.append(jnp.zeros((depth, S5_GROUPS, 2, S5_STATE, S5_GROUP), F32))
        m = jnp.stack(parts, axis=2)
        m = jnp.transpose(m, (0, 1, 2, 5, 3, 4))
        return compact(m, q * S5_LANES)

    cm = jnp.stack([jnp.transpose(cp_re[:, 1:], (0, 2, 4, 1, 3)),
                    -jnp.transpose(cp_im[:, 1:], (0, 2, 4, 1, 3))], axis=2)
    cmc = compact(cm, S5_SUPER_STATE)
    e_sm = np.zeros((2, S5_STATE, 2, ng, S5_STATE), np.float32)
    e_cm = np.zeros((q, S5_GROUP, q, ng, S5_GROUP), np.float32)
    for g in range(ng):
        e_sm[:, :, :, g, :] = np.eye(2 * S5_STATE).reshape(2, S5_STATE, 2, S5_STATE)
        e_cm[:, :, :, g, :] = np.eye(q * S5_GROUP).reshape(q, S5_GROUP, q, S5_GROUP)
    e_sm = jnp.asarray(e_sm.reshape(2 * S5_STATE, S5_SUPER_STATE), BF16)
    e_cm = jnp.asarray(e_cm.reshape(q * S5_GROUP, q * S5_LANES), BF16)

    def packed(p):
        pr = p[0].reshape(depth, nsb, ng * S5_STATE)
        pi = p[1].reshape(depth, nsb, ng * S5_STATE)
        return jnp.concatenate([pr, pr], axis=-1), jnp.concatenate([-pi, pi], axis=-1)

    doubling = [pw[q]]
    for _ in range(S5_DOUBLINGS - 1):
        doubling.append(cmul(doubling[-1], doubling[-1]))
    pa = jnp.stack([packed(p)[0] for p in doubling], axis=2)
    pb = jnp.stack([packed(p)[1] for p in doubling], axis=2)
    va16, vb16 = packed(pw[q])
    va, vb = packed(pw[valid])
    ex = lambda a: a[:, :, None, :]
    return {"bd": bd, "smc16": state_in(q), "smc": state_in(valid), "cmc": cmc, "e_sm": e_sm, "e_cm": e_cm,
            "pa": pa, "pb": pb, "va16": ex(va16), "vb16": ex(vb16), "va": ex(va), "vb": ex(vb)}


def _t5_bucket(dist):
    dist = np.asarray(dist)
    large = REL_MAX_EXACT + (np.log(np.maximum(dist, 1) / REL_MAX_EXACT)
                             / math.log(REL_MAX_DIST / REL_MAX_EXACT)
                             * (REL_BUCKETS - REL_MAX_EXACT)).astype(np.int32)
    large = np.minimum(large, REL_BUCKETS - 1)
    return np.where(dist < REL_MAX_EXACT, dist, large).astype(np.int32)


def _bias_steps(rel_bias, grp, dil):
    heads = slice(grp * SWA_GROUP_HEADS, (grp + 1) * SWA_GROUP_HEADS)
    buckets = _t5_bucket(dil * np.arange(SWA_BLOCK + 1))
    onehot = np.zeros((SWA_BLOCK + 1, REL_BUCKETS), np.float32)
    onehot[np.arange(SWA_BLOCK + 1), buckets] = 1.0
    steps = jnp.einsum("jb,bh->hj", onehot, rel_bias[:, heads], precision=lax.Precision.HIGHEST)
    return steps.astype(F32)


def _softmax_pieces(logits):
    m = logits[0].max(axis=1, keepdims=True)
    for s in logits[1:]:
        m = jnp.maximum(m, s.max(axis=1, keepdims=True))
    ps = [jnp.exp(s - m) for s in logits]
    den = ps[0].sum(axis=1, keepdims=True)
    for p in ps[1:]:
        den = den + p.sum(axis=1, keepdims=True)
    return ps, den, m + jnp.log(den)


SWA_UNROLL = 4
SWA_DENSE_BLOCKS = 4


def _swa_prompt_kernel(q_ref, k_ref, v_ref, kp_ref, vp_ref, bias_ref, o_ref, lse_ref, kvt_ref, *, dil):
    n = SWA_BLOCK
    pair = pl.program_id(1)
    first = pl.program_id(2) == 0
    scale = SWA_HEAD_DIM ** -0.5

    def tile(sl, kp, vp, mask_prev):
        qq, kk, vv = q_ref[sl, :], k_ref[sl, :], v_ref[sl, :]
        outs, lses = [], []
        for j in range(2):
            hs = slice(j * SWA_HEAD_DIM, (j + 1) * SWA_HEAD_DIM)
            bias = bias_ref[2 * pair + j]
            bias_prev = bias[:, 0:n] if mask_prev is None else jnp.where(mask_prev, MASKED, bias[:, 0:n])
            q_h = qq[:, hs] * scale
            ps, den, lse = _softmax_pieces([_dot_nt(q_h, kk[:, hs]) + bias[:, n:2 * n],
                                            _dot_nt(q_h, kp[:, hs]) + bias_prev])
            outs.append((_dot(ps[0], vv[:, hs]) + _dot(ps[1], vp[:, hs])) / den)
            lses.append(jnp.broadcast_to(lse, (n, SWA_HEAD_DIM)))
        o_ref[sl, :] = jnp.concatenate(outs, axis=1)
        lse_ref[sl, :] = jnp.concatenate(lses, axis=1)

    if dil == 1:
        for j in range(SWA_DENSE_BLOCKS):
            if j == 0:
                tile(slice(0, n), kp_ref[...], vp_ref[...], first)
            else:
                before = slice((j - 1) * n, j * n)
                tile(slice(j * n, (j + 1) * n), k_ref[before, :], v_ref[before, :], None)
    else:
        def group(i, carry):
            for u in range(SWA_UNROLL):
                sl = pl.ds(i * SWA_UNROLL + u, n, stride=dil)
                tile(sl, kp_ref[sl, :], vp_ref[sl, :], first)
            return carry
        if dil == SWA_UNROLL:
            group(0, 0)
        else:
            lax.fori_loop(0, dil // SWA_UNROLL, group, 0)
    rows = k_ref.shape[0]
    kvt_ref[0] = k_ref[rows - n * dil:rows, :].T
    kvt_ref[1] = v_ref[rows - n * dil:rows, :].T


def _swa_prompt_call(p_swa, bias, kvt_prev, l, depth, *, grp, dil, nb, seq):
    ta = p_swa.shape[0]
    win = SWA_BLOCK * dil
    sb = SWA_BLOCK * SWA_DENSE_BLOCKS if dil == 1 else win
    assert dil == 1 or dil % SWA_UNROLL == 0
    nsb = seq // sb
    lanes = 2 * SWA_HEAD_DIM
    npair = SWA_GROUP_HEADS // 2
    cur = lambda col: pl.BlockSpec((sb, lanes), lambda b, p, c: (b * nsb + c, 2 * col + p))
    if dil == 1:
        per = sb // SWA_BLOCK
        prv = lambda col: pl.BlockSpec(
            (SWA_BLOCK, lanes), lambda b, p, c: (jnp.maximum((b * nsb + c) * per - 1, 0), 2 * col + p))
    else:
        prv = lambda col: pl.BlockSpec((sb, lanes), lambda b, p, c: (b * nsb + jnp.maximum(c - 1, 0), 2 * col + p))
    out = pl.BlockSpec((sb, lanes), lambda b, p, c: (b * nsb + c, p))
    kvt_spec = pl.BlockSpec((None, None, 2, None, lanes, win), lambda b, p, c: (l, b, 0, p, 0, 0))
    shape = jax.ShapeDtypeStruct((ta, SWA_GROUP_WIDTH), F32)
    in_specs = [cur(grp), cur(3 + grp), cur(6 + grp), prv(3 + grp), prv(6 + grp), _full(bias.shape)]
    args = [p_swa, p_swa, p_swa, p_swa, p_swa, bias]
    kern, aliases = _with_aliased(functools.partial(_swa_prompt_kernel, dil=dil), in_specs, args,
                                  [] if kvt_prev is None else [(kvt_prev, 2)])
    return pl.pallas_call(
        kern,
        grid=(nb, npair, nsb),
        in_specs=in_specs,
        out_specs=[out, out, kvt_spec],
        out_shape=[shape, shape, jax.ShapeDtypeStruct((depth, nb, 2, npair, lanes, win), F32)],
        input_output_aliases=aliases,
        compiler_params=_params(3),
        name="swa_prompt_d%d" % dil,
    )(*args)


def _swa_prompt_bias(steps):
    n = SWA_BLOCK
    period = 3 * n + 1
    f = jnp.concatenate([steps[:, ::-1], jnp.full((steps.shape[0], period - (n + 1)), MASKED, F32)], axis=1)
    tiled = jnp.tile(f, (1, n))[:, :n * (period - 1)]
    return tiled.reshape(steps.shape[0], n, period - 1)[:, :, :2 * n]


def _swa_sample_kernel(q_ref, k_ref, v_ref, buf_ref, bias_buf_ref, bias_new_ref, o_ref, lse_ref, cache_ref, *,
                       width, t_new):
    scale = SWA_HEAD_DIM ** -0.5
    w = SWA_GROUP_WIDTH
    qq, kn, vn = q_ref[...], k_ref[...], v_ref[...]
    outs, lses = [], []
    for h in range(SWA_GROUP_HEADS):
        hs = slice(h * SWA_HEAD_DIM, (h + 1) * SWA_HEAD_DIM)
        k_t = buf_ref[h * SWA_HEAD_DIM:(h + 1) * SWA_HEAD_DIM, :]
        v_t = buf_ref[w + h * SWA_HEAD_DIM:w + (h + 1) * SWA_HEAD_DIM, :]
        q_h = qq[:, hs] * scale
        ps, den, lse = _softmax_pieces([_dot(q_h, k_t) + bias_buf_ref[h], _dot_nt(q_h, kn[:, hs]) + bias_new_ref[h]])
        outs.append((_dot_nt(ps[0], v_t) + _dot(ps[1], vn[:, hs])) / den)
        lses.append(jnp.broadcast_to(lse, (SAMPLE_ROWS, SWA_HEAD_DIM)))
    o_ref[...] = jnp.concatenate(outs, axis=1)
    lse_ref[...] = jnp.concatenate(lses, axis=1)
    new_t = jnp.concatenate([kn, vn], axis=1).T
    cache_ref[...] = pltpu.roll(buf_ref[...], width - t_new, axis=1)
    cache_ref[:, width - t_new:width] = new_t[:, 0:t_new]


def _swa_sample_call(p_swa, cache_t, bias_buf, bias_new, prev_o, prev_lse, prev_cache, l, *, grp, nb, row0, t_new):
    ta = p_swa.shape[0]
    width = cache_t.shape[3]
    w = SWA_GROUP_WIDTH
    blk0 = row0 // SAMPLE_ROWS
    tok = lambda col: pl.BlockSpec((SAMPLE_ROWS, w), lambda b: (blk0 + b, col))
    cache_spec = pl.BlockSpec((None, None, 2 * w, width), lambda b: (l, b, 0, 0))
    in_specs = [tok(grp), tok(3 + grp), tok(6 + grp), cache_spec, _full(bias_buf.shape), _full(bias_new.shape)]
    args = [p_swa, p_swa, p_swa, cache_t, bias_buf, bias_new]
    aliased = [(prev_o, 0), (prev_lse, 1)] + ([] if prev_cache is None else [(prev_cache, 2)])
    kern, aliases = _with_aliased(functools.partial(_swa_sample_kernel, width=width, t_new=t_new), in_specs, args,
                                  aliased)
    return pl.pallas_call(
        kern,
        grid=(nb,),
        in_specs=in_specs,
        out_specs=[tok(0), tok(0), cache_spec],
        out_shape=[jax.ShapeDtypeStruct((ta, w), F32), jax.ShapeDtypeStruct((ta, w), F32),
                   jax.ShapeDtypeStruct(cache_t.shape, F32)],
        input_output_aliases=aliases,
        compiler_params=_params(1),
        name="swa_sample_w%d" % width,
    )(*args)


def _swa_sample_bias(steps, window, dil, width, t_new):
    n = window // dil
    nh = steps.shape[0]
    g = jnp.pad(steps[:, :, None], ((0, 0), (0, 0), (0, dil - 1)), constant_values=MASKED).reshape(nh, (n + 1) * dil)
    g = jnp.pad(g, ((0, 0), (0, width + SAMPLE_ROWS)), constant_values=MASKED)
    masked_row = jnp.full((nh, 1, width), MASKED, F32)
    rows = [g[:, t + 1:t + 1 + width][:, None, ::-1] if t < t_new else masked_row for t in range(SAMPLE_ROWS)]
    b_buf = jnp.concatenate(rows, axis=1)
    new_rows = []
    for t in range(SAMPLE_ROWS):
        if t < t_new:
            row = jnp.concatenate([g[:, 0:t + 1][:, ::-1], jnp.full((nh, SAMPLE_ROWS - t - 1), MASKED, F32)], axis=1)
        else:
            row = jnp.full((nh, SAMPLE_ROWS), MASKED, F32)
        new_rows.append(row[:, None, :])
    return b_buf, jnp.concatenate(new_rows, axis=1)


def _merge_kernel(x_ref, yssd_ref, o0_ref, l0_ref, o1_ref, l1_ref, o2_ref, l2_ref, ys5_ref, u_ref, yml_ref,
                  g_ref, wg_ref, wssd_ref, wswa_ref, ws5_ref, wml_ref, wglu_ref, wout_ref, d_ref, out_ref):
    x = x_ref[...]
    h = _rms(x, g_ref[...]).astype(BF16)
    l0, l1, l2 = l0_ref[...], l1_ref[...], l2_ref[...]
    m = jnp.maximum(jnp.maximum(l0, l1), l2)
    e0, e1, e2 = jnp.exp(l0 - m), jnp.exp(l1 - m), jnp.exp(l2 - m)
    y_swa = (e0 * o0_ref[...] + e1 * o1_ref[...] + e2 * o2_ref[...]) / (e0 + e1 + e2)
    y5 = ys5_ref[...] + d_ref[...] * u_ref[...]
    y_s5 = y5 * _sigmoid(_dot(y5, wglu_ref[...]))
    branches = ((yssd_ref[...], wssd_ref), (y_swa, wswa_ref), (y_s5, ws5_ref), (yml_ref[...], wml_ref))
    merged = None
    for i, (y, w_ref) in enumerate(branches):
        gate = _sigmoid(jnp.dot(h, wg_ref[:, i * D_MODEL:(i + 1) * D_MODEL], preferred_element_type=F32))
        term = gate * _dot(y, w_ref[...])
        merged = term if merged is None else merged + term
    out_ref[...] = x + _dot(merged, wout_ref[...])


def _merge_call(x, acts, wts, l):
    ta = x.shape[0]
    tm = TOKEN_TILE
    row = lambda a: pl.BlockSpec((tm, a.shape[1]), lambda i: (i, 0))
    ws = tuple(wts[k] for k in ("norm_mix", "w_gate", "w_br_ssd", "w_br_swa", "w_br_s5", "w_br_mlstm", "s5_w_glu",
                                "w_out", "s5_d"))
    return pl.pallas_call(
        _merge_kernel,
        grid=(ta // tm,),
        in_specs=[row(a) for a in (x, *acts)] + [_layer(w, l) for w in ws],
        out_specs=row(x),
        out_shape=jax.ShapeDtypeStruct(x.shape, F32),
        compiler_params=_params(1),
        name="merge",
    )(x, *acts, *ws)


def _norm_matmul_kernel(x_ref, g_ref, w_ref, o_ref):
    o_ref[...] = jnp.dot(_rms(x_ref[...], g_ref[...]).astype(BF16), w_ref[...], preferred_element_type=F32)


def _norm_matmul(x, g, w, l, tm):
    rows = x.shape[0]
    return pl.pallas_call(
        _norm_matmul_kernel,
        grid=(rows // tm,),
        in_specs=[pl.BlockSpec((tm, x.shape[1]), lambda i: (i, 0)), _layer(g, l), _layer(w, l)],
        out_specs=pl.BlockSpec((tm, w.shape[2]), lambda i: (i, 0)),
        out_shape=jax.ShapeDtypeStruct((rows, w.shape[2]), F32),
        compiler_params=_params(1),
        name="norm_matmul",
    )(x, g, w)


def _rms_kernel(x_ref, g_ref, o_ref):
    o_ref[...] = _rms(x_ref[...], g_ref[...])


def _rmsnorm_call(x, g, tm):
    rows = x.shape[0]
    return pl.pallas_call(
        _rms_kernel,
        grid=(rows // tm,),
        in_specs=[pl.BlockSpec((tm, x.shape[1]), lambda i: (i, 0)), _full(g.shape)],
        out_specs=pl.BlockSpec((tm, x.shape[1]), lambda i: (i, 0)),
        out_shape=jax.ShapeDtypeStruct(x.shape, F32),
        compiler_params=_params(1),
        name="final_norm",
    )(x, g)


def _xattn_kernel(x_ref, kv_ref, g_ref, wq_ref, wo_ref, out_ref, *, kv_rows):
    x = x_ref[...]
    q = jnp.dot(_rms(x, g_ref[...]).astype(BF16), wq_ref[...], preferred_element_type=F32)
    scale = XA_HEAD_DIM ** -0.5
    n_mem = kv_ref.shape[0] // (2 * XA_HEADS) if kv_rows else kv_ref.shape[0]
    outs = []
    for h in range(XA_HEADS):
        hs = slice(h * XA_HEAD_DIM, (h + 1) * XA_HEAD_DIM)
        if kv_rows:
            k_h = kv_ref[h * n_mem:(h + 1) * n_mem, :]
            v_h = kv_ref[(XA_HEADS + h) * n_mem:(XA_HEADS + h + 1) * n_mem, :]
        else:
            k_h = kv_ref[:, hs]
            v_h = kv_ref[:, D_MODEL + h * XA_HEAD_DIM:D_MODEL + (h + 1) * XA_HEAD_DIM]
        logits = _dot_nt(q[:, hs], k_h) * scale
        m = logits.max(axis=1, keepdims=True)
        p = jnp.exp(logits - m)
        outs.append(_dot(p / p.sum(axis=1, keepdims=True), v_h))
    out_ref[...] = x + _dot(jnp.concatenate(outs, axis=1), wo_ref[...])


def _xattn_call(x, kv, kv_layer, wts, l, *, tm, n_tiles, tiles_per_seq, row0, kv_rows, prev):
    ta = x.shape[0]
    blk0 = row0 // tm
    row = pl.BlockSpec((tm, D_MODEL), lambda i: (blk0 + i, 0))
    ws = tuple(wts[k] for k in ("norm_xa", "xa_wq", "xa_wo"))
    in_specs = [row, pl.BlockSpec((None, None) + kv.shape[2:], lambda i: (kv_layer, i // tiles_per_seq, 0, 0))]
    in_specs += [_layer(w, l) for w in ws]
    args = [x, kv, *ws]
    kern, aliases = _with_aliased(functools.partial(_xattn_kernel, kv_rows=kv_rows), in_specs, args,
                                  [] if prev is None else [(prev, 0)])
    return pl.pallas_call(
        kern,
        grid=(n_tiles,),
        in_specs=in_specs,
        out_specs=row,
        out_shape=jax.ShapeDtypeStruct((ta, D_MODEL), F32),
        input_output_aliases=aliases,
        compiler_params=_params(1),
        name="xattn_tm%d" % tm,
    )(*args)


def _ffn_kernel(x_ref, g_ref, w1_ref, w2_ref, out_ref):
    x = x_ref[...]
    a = jnp.dot(_rms(x, g_ref[...]).astype(BF16), w1_ref[...], preferred_element_type=F32)
    a = jnp.square(jnp.maximum(a, 0.0))
    out_ref[...] = x + _dot(a, w2_ref[...])


def _ffn_call(x, wts, l):
    ta = x.shape[0]
    tm = TOKEN_TILE
    row = pl.BlockSpec((tm, D_MODEL), lambda i: (i, 0))
    ws = tuple(wts[k] for k in ("norm_mlp", "w_ff1", "w_ff2"))
    return pl.pallas_call(
        _ffn_kernel,
        grid=(ta // tm,),
        in_specs=[row] + [_layer(w, l) for w in ws],
        out_specs=row,
        out_shape=jax.ShapeDtypeStruct(x.shape, F32),
        compiler_params=_params(1),
        name="ffn",
    )(x, *ws)


def _prep_weights(w):
    o = IN_OFFS
    w_in = w["w_in"]
    small = jnp.concatenate([w_in[:, :, o[2]:o[3]], w_in[:, :, o[11]:o[13]]], axis=2)
    row = lambda v: v[:, None, :].astype(F32)
    col = lambda v: v[:, :, None].astype(F32)
    bf = lambda v: v.astype(BF16)
    return {
        "norm_mix": row(w["norm_mix"]),
        "w_main": jnp.concatenate([w_in[:, :, o[0]:o[2]], w_in[:, :, o[3]:o[11]]], axis=2).astype(BF16),
        "w_gate": bf(w_in[:, :, o[13]:o[14]]),
        "w_small": jnp.pad(small, ((0, 0), (0, 0), (0, SMALL_COLS - small.shape[2]))).astype(BF16),
        "w_small_t": jnp.transpose(small, (0, 2, 1)).astype(BF16),
        "ssd_conv_w": w["ssd_conv_w"], "ssd_conv_b": row(w["ssd_conv_b"]),
        "dtb_c": row(w["ssd_dt_bias"]), "dtb_r": col(w["ssd_dt_bias"]),
        "alog_c": row(w["ssd_a_log"]), "alog_r": col(w["ssd_a_log"]),
        "ssd_dskip": row(jnp.repeat(w["ssd_d"], SSD_HEAD_DIM, axis=1)), "ssd_norm": row(w["ssd_norm"]),
        "ib_c": row(w["mlstm_i_bias"]), "ib_r": col(w["mlstm_i_bias"]),
        "fb_c": row(w["mlstm_f_bias"]), "fb_r": col(w["mlstm_f_bias"]),
        "mlstm_norm": row(w["mlstm_norm"]),
        "s5_d": row(w["s5_d"]), "s5_w_glu": bf(w["s5_w_glu"]),
        "w_br_ssd": bf(w["w_br_ssd"]), "w_br_swa": bf(w["w_br_swa"]), "w_br_s5": bf(w["w_br_s5"]),
        "w_br_mlstm": bf(w["w_br_mlstm"]), "w_out": bf(w["w_out"]),
        "norm_xa": row(w["norm_xa"]), "xa_wq": bf(w["xa_wq"]), "xa_wo": bf(w["xa_wo"]),
        "norm_mem": row(w["norm_mem"]),
        "xa_wkv": jnp.concatenate([w["xa_wk"], w["xa_wv"]], axis=2).astype(BF16),
        "norm_mlp": row(w["norm_mlp"]), "w_ff1": bf(w["w_ff1"]), "w_ff2": bf(w["w_ff2"]),
    }


def kernel(x_prompt, x_sample, state_ssd, state_ssd_conv, cache_swa_w128, cache_swa_w512, cache_swa_w2048, state_s5, state_mlstm_c, state_mlstm_n, state_mlstm_m, cache_mem_kv, mem_prompt, norm_mix, w_in, ssd_conv_w, ssd_conv_b, ssd_dt_bias, ssd_a_log, ssd_d, ssd_norm, rel_bias, s5_a_re, s5_a_im, s5_log_dt, s5_b_re, s5_b_im, s5_c_re, s5_c_im, s5_d, s5_w_glu, mlstm_i_bias, mlstm_f_bias, mlstm_norm, w_br_ssd, w_br_swa, w_br_s5, w_br_mlstm, w_out, norm_xa, norm_mem, xa_wq, xa_wk, xa_wv, xa_wo, norm_mlp, w_ff1, w_ff2, norm_final):
    wts = _prep_weights(dict(
        norm_mix=norm_mix, w_in=w_in, ssd_conv_w=ssd_conv_w, ssd_conv_b=ssd_conv_b, ssd_dt_bias=ssd_dt_bias,
        ssd_a_log=ssd_a_log, ssd_d=ssd_d, ssd_norm=ssd_norm, s5_d=s5_d, s5_w_glu=s5_w_glu,
        mlstm_i_bias=mlstm_i_bias, mlstm_f_bias=mlstm_f_bias, mlstm_norm=mlstm_norm, w_br_ssd=w_br_ssd,
        w_br_swa=w_br_swa, w_br_s5=w_br_s5, w_br_mlstm=w_br_mlstm, w_out=w_out, norm_xa=norm_xa, norm_mem=norm_mem,
        xa_wq=xa_wq, xa_wk=xa_wk, xa_wv=xa_wv, xa_wo=xa_wo, norm_mlp=norm_mlp, w_ff1=w_ff1, w_ff2=w_ff2))
    nb_p, seq, _ = x_prompt.shape
    nb_s, t_new, _ = x_sample.shape
    depth = w_in.shape[0]
    n_mem = mem_prompt.shape[1]
    rows_s = SAMPLE_ROWS
    n_p = nb_p * seq
    n_s = nb_s * rows_s
    ta = n_p + n_s
    assert seq % (SWA_BLOCK * SWA_PATTERN[-1][1]) == 0 and ta % TOKEN_TILE == 0 and t_new <= rows_s
    assert n_p % n_s == 0 and (seq // S5_CHUNK) & (seq // S5_CHUNK - 1) == 0
    caches = (cache_swa_w128, cache_swa_w512, cache_swa_w2048)

    xs_pad = jnp.pad(x_sample, ((0, 0), (0, rows_s - t_new), (0, 0))).reshape(n_s, D_MODEL)
    x = jnp.concatenate([x_prompt.reshape(n_p, D_MODEL), xs_pad], axis=0)

    steps = [_bias_steps(rel_bias, g, dil) for g, (_, dil) in enumerate(SWA_PATTERN)]
    prompt_bias = [_swa_prompt_bias(s) for s in steps]
    sample_bias = [_swa_sample_bias(steps[g], win, dil, caches[g].shape[2], t_new)
                   for g, (win, dil) in enumerate(SWA_PATTERN)]
    caches_t = [jnp.transpose(c, (0, 1, 3, 4, 5, 2)).reshape(depth, nb_s, 2 * SWA_GROUP_WIDTH, c.shape[2])
                for c in caches]
    mem_kv_s = jnp.transpose(cache_mem_kv, (0, 1, 3, 4, 2, 5)).reshape(depth, nb_s, 2 * XA_HEADS * n_mem, XA_HEAD_DIM)
    s5_tab = _s5_tables(s5_a_re, s5_a_im, s5_log_dt, s5_b_re, s5_b_im, s5_c_re, s5_c_im, t_new)
    s5_h0_s = jnp.transpose(state_s5.reshape(depth, nb_s, S5_SUPER, S5_SUPER_GROUPS, S5_STATE, 2),
                            (0, 2, 1, 5, 3, 4)).reshape(depth, S5_SUPER, 1, nb_s, S5_SUPER_STATE)

    zeros = lambda *s: jnp.zeros(s, F32)
    zero_conv = zeros(1, nb_p, 8, SSD_CONV_DIM)
    zero_ssd = zeros(1, nb_p, SSD_HEADS, SSD_HEAD_DIM, SSD_STATE)
    zero_c = zeros(1, nb_p, ML_HEADS, ML_HEAD_DIM, ML_HEAD_DIM)
    zero_vec = zeros(1, nb_p, ML_HEADS, 1, ML_HEAD_DIM)
    zero_s5 = zeros(S5_SUPER, nb_p, 1, S5_SUPER_STATE)
    conv_init_s = jnp.pad(state_ssd_conv, ((0, 0), (0, 0), (8 - (SSD_CONV - 1), 0), (0, 0)))
    n_init_s = state_mlstm_n[:, :, :, None, :]
    m_init_s = jnp.broadcast_to(state_mlstm_m[:, :, :, None, None], (depth, nb_s, ML_HEADS, 1, ML_HEAD_DIM))

    outs = {k: [] for k in ("ssd_p", "ssd_s", "conv_p", "conv_s", "s5_p", "s5_s", "c_p", "c_s", "n_p", "n_s",
                            "m_p", "m_s", "kv_p")}
    kvt_p = [None] * len(SWA_PATTERN)
    cache_out = [None] * len(SWA_PATTERN)
    ssd_q, ml_q, nsub = 128, 128, 4
    nchunk = seq // S5_CHUNK

    for l in range(depth):
        p_ssd, p_swa, p_s5, p_ml, p_small, p_small_t = _inproj(x, wts, l)
        small_t_s = jnp.transpose(p_small[n_p:, :SMALL_ROWS].reshape(nb_s, rows_s, SMALL_ROWS), (0, 2, 1))

        y_ssd, st_p = _ssd_call(p_ssd, p_small, p_small_t, zero_conv, zero_ssd, 0, wts, l, nb=nb_p,
                                nc=seq // (ssd_q * nsub), q=ssd_q, nsub=nsub, valid=ssd_q, row0=0, prev=None)
        y_ssd, st_s = _ssd_call(p_ssd, p_small, small_t_s, conv_init_s, state_ssd, l, wts, l,
                                nb=nb_s, nc=1, q=rows_s, nsub=1, valid=t_new, row0=n_p, prev=y_ssd)
        outs["ssd_p"].append(st_p)
        outs["ssd_s"].append(st_s)
        tail = SSD_CONV - 1
        xbc_s = p_ssd[n_p:, SSD_D_INNER:].reshape(nb_s, rows_s, SSD_CONV_DIM)[:, :t_new]
        outs["conv_p"].append(jnp.stack([p_ssd[(b + 1) * seq - tail:(b + 1) * seq, SSD_D_INNER:]
                                         for b in range(nb_p)]))
        outs["conv_s"].append(jnp.concatenate([state_ssd_conv[l], xbc_s], axis=1)[:, -(SSD_CONV - 1):])

        swa_acts = []
        for g, (win, dil) in enumerate(SWA_PATTERN):
            o_g, lse_g, kvt_p[g] = _swa_prompt_call(p_swa, prompt_bias[g], kvt_p[g], l, depth, grp=g, dil=dil,
                                                    nb=nb_p, seq=seq)
            o_g, lse_g, cache_out[g] = _swa_sample_call(p_swa, caches_t[g], sample_bias[g][0], sample_bias[g][1],
                                                        o_g, lse_g, cache_out[g], l, grp=g, nb=nb_s, row0=n_p,
                                                        t_new=t_new)
            swa_acts += [o_g, lse_g]

        y_s5, h_p = _s5_call(p_s5, zero_s5, s5_tab, l, nseq=nb_p, rows=nchunk, scan=True, row0=0, prev=None)
        y_s5, h_s = _s5_call(p_s5, s5_h0_s[l], s5_tab, l, nseq=1, rows=nb_s, scan=False, row0=n_p, prev=y_s5)
        outs["s5_p"].append(h_p)
        outs["s5_s"].append(h_s)

        y_ml, c_p, nn_p, m_p = _mlstm_call(p_ml, p_small, p_small_t, zero_c, zero_vec, zero_vec, 0, wts, l, nb=nb_p,
                                           nc=seq // (ml_q * nsub), q=ml_q, nsub=nsub, valid=ml_q, row0=0,
                                           prev=None)
        y_ml, c_s, nn_s, m_s = _mlstm_call(p_ml, p_small, small_t_s, state_mlstm_c, n_init_s, m_init_s, l, wts, l,
                                           nb=nb_s, nc=1, q=rows_s, nsub=1, valid=t_new, row0=n_p, prev=y_ml)
        for key, val in (("c_p", c_p), ("c_s", c_s), ("n_p", nn_p[:, :, 0]), ("n_s", nn_s[:, :, 0]),
                         ("m_p", m_p[:, :, 0, 0]), ("m_s", m_s[:, :, 0, 0])):
            outs[key].append(val)

        x = _merge_call(x, (y_ssd, *swa_acts, y_s5, p_s5, y_ml), wts, l)

        kv_p = _norm_matmul(mem_prompt.reshape(nb_p * n_mem, D_MODEL), wts["norm_mem"], wts["xa_wkv"], l, n_mem)
        outs["kv_p"].append(kv_p.reshape(nb_p, n_mem, 2, XA_HEADS, XA_HEAD_DIM))
        x_new = _xattn_call(x, kv_p.reshape(1, nb_p, n_mem, 2 * D_MODEL), 0, wts, l, tm=TOKEN_TILE,
                            n_tiles=n_p // TOKEN_TILE, tiles_per_seq=seq // TOKEN_TILE, row0=0, kv_rows=False,
                            prev=None)
        x = _xattn_call(x, mem_kv_s, l, wts, l, tm=rows_s, n_tiles=nb_s, tiles_per_seq=1, row0=n_p, kv_rows=True,
                        prev=x_new)

        x = _ffn_call(x, wts, l)

    y = _rmsnorm_call(x, norm_final[None, :], TOKEN_TILE)
    y_prompt = y[:n_p].reshape(nb_p, seq, D_MODEL)
    y_sample = y[n_p:].reshape(nb_s, rows_s, D_MODEL)[:, :t_new]
    st = lambda k: jnp.stack(outs[k])

    def s5_state(k, nb):
        h = st(k).reshape(depth, S5_SUPER, nb, 2, S5_SUPER_GROUPS, S5_STATE)
        return jnp.transpose(h, (0, 2, 1, 4, 5, 3)).reshape(depth, nb, S5_GROUPS, S5_STATE, 2)

    swa_out = []
    for g in range(len(SWA_PATTERN)):
        width_p = kvt_p[g].shape[-1]
        kp = kvt_p[g].reshape(depth, nb_p, 2, SWA_GROUP_HEADS, SWA_HEAD_DIM, width_p)
        swa_out.append(jnp.transpose(kp, (0, 1, 5, 2, 3, 4)))
        cs = cache_out[g].reshape(depth, nb_s, 2, SWA_GROUP_HEADS, SWA_HEAD_DIM, caches[g].shape[2])
        swa_out.append(jnp.transpose(cs, (0, 1, 5, 2, 3, 4)))
    return (y_prompt, y_sample, st("ssd_p"), st("ssd_s"), st("conv_p"), st("conv_s"), *swa_out,
            s5_state("s5_p", nb_p), s5_state("s5_s", nb_s), st("c_p"), st("c_s"),
            st("n_p"), st("n_s"), st("m_p"), st("m_s"), st("kv_p"))
```

```python
import functools
import math

import numpy as np
import jax
import jax.numpy as jnp
from jax import lax
from jax.experimental import pallas as pl
from jax.experimental.pallas import tpu as pltpu

F32 = jnp.float32
BF16 = jnp.bfloat16

D_MODEL = 1024
RMS_EPS = 1e-6
N_BRANCH = 4

SSD_D_INNER = 512
SSD_HEAD_DIM = 64
SSD_HEADS = 8
SSD_GROUPS = 2
SSD_STATE = 64
SSD_CONV = 4
SSD_CONV_DIM = SSD_D_INNER + 2 * SSD_GROUPS * SSD_STATE
SSD_SEG = SSD_D_INNER + SSD_CONV_DIM

SWA_PATTERN = ((128, 1), (512, 4), (2048, 16))
SWA_GROUP_HEADS = 4
SWA_HEAD_DIM = 64
SWA_GROUP_WIDTH = SWA_GROUP_HEADS * SWA_HEAD_DIM
SWA_WIDTH = 3 * SWA_GROUP_WIDTH
SWA_BLOCK = 128
REL_BUCKETS = 32
REL_MAX_EXACT = 16
REL_MAX_DIST = 2048

S5_WIDTH = 512
S5_GROUP = 16
S5_GROUPS = 32
S5_STATE = 64
S5_CHUNK = 16
S5_LANES = 128
S5_SUPER = S5_WIDTH // S5_LANES
S5_SUPER_GROUPS = S5_GROUPS // S5_SUPER
S5_SUPER_STATE = 2 * S5_SUPER_GROUPS * S5_STATE
S5_DOUBLINGS = 12

ML_WIDTH = 512
ML_HEADS = 4
ML_HEAD_DIM = 128

XA_HEADS = 4
XA_HEAD_DIM = 256
D_FF = 4096

MAIN_SSD = (0, SSD_SEG)
MAIN_SWA = (SSD_SEG, SSD_SEG + 3 * SWA_WIDTH)
MAIN_S5 = (MAIN_SWA[1], MAIN_SWA[1] + S5_WIDTH)
MAIN_ML = (MAIN_S5[1], MAIN_S5[1] + 4 * ML_WIDTH)
MAIN_COLS = MAIN_ML[1]
SMALL_COLS = 128
SMALL_ROWS = 16

SAMPLE_ROWS = 16
TOKEN_TILE = 512
SUB_TILE = 256
MASKED = -1e30

VMEM_LIMIT = 56 * 1024 * 1024

IN_SIZES = (SSD_D_INNER, SSD_CONV_DIM, SSD_HEADS, SWA_WIDTH, SWA_WIDTH, SWA_WIDTH, S5_WIDTH,
            ML_WIDTH, ML_WIDTH, ML_WIDTH, ML_WIDTH, ML_HEADS, ML_HEADS, N_BRANCH * D_MODEL)
IN_OFFS = tuple(int(v) for v in np.concatenate([[0], np.cumsum(IN_SIZES)]))


def _params(n_axes):
    return pltpu.CompilerParams(dimension_semantics=("arbitrary",) * n_axes, vmem_limit_bytes=VMEM_LIMIT)


def _full(shape):
    nd = len(shape)
    return pl.BlockSpec(shape, lambda *_: (0,) * nd)


def _layer(a, l):
    nd = a.ndim
    return pl.BlockSpec((None,) + a.shape[1:], lambda *_: (l,) + (0,) * (nd - 1), pipeline_mode=pl.Buffered(1))


def _sub_tiles(rows):
    step = min(SUB_TILE, rows)
    return [slice(r, r + step) for r in range(0, rows, step)]


def _with_aliased(kern, in_specs, args, aliased):
    n = len(args)
    aliases = {}
    for arr, out_idx in aliased:
        in_specs.append(pl.BlockSpec(memory_space=pl.ANY))
        aliases[len(args)] = out_idx
        args.append(arr)
    k = len(aliased)
    if k == 0:
        return kern, aliases

    def wrapped(*refs):
        return kern(*refs[:n], *refs[n + k:])
    return wrapped, aliases


def _dot(a, b):
    return jnp.dot(a.astype(BF16), b.astype(BF16), preferred_element_type=F32)


def _dot_nt(a, b):
    return lax.dot_general(a.astype(BF16), b.astype(BF16), (((1,), (1,)), ((), ())), preferred_element_type=F32)


def _dot_tn(a, b):
    return lax.dot_general(a.astype(BF16), b.astype(BF16), (((0,), (0,)), ((), ())), preferred_element_type=F32)


def _split3(x):
    hi = x.astype(BF16)
    r = x - hi.astype(F32)
    mid = r.astype(BF16)
    lo = (r - mid.astype(F32)).astype(BF16)
    return hi, mid, lo


def _dot_exact_l(ones_bf16, x):
    hi, mid, lo = _split3(x)
    f = lambda p: jnp.dot(ones_bf16, p, preferred_element_type=F32)
    return f(hi) + f(mid) + f(lo)


def _dot_exact_r(x, ones_bf16):
    hi, mid, lo = _split3(x)
    f = lambda p: jnp.dot(p, ones_bf16, preferred_element_type=F32)
    return f(hi) + f(mid) + f(lo)


def _rms(x, g):
    return x * lax.rsqrt(jnp.mean(x * x, axis=-1, keepdims=True) + RMS_EPS) * g


def _sigmoid(x):
    return 1.0 / (1.0 + jnp.exp(-x))


def _softplus(x):
    return jnp.maximum(x, 0.0) + jnp.log(1.0 + jnp.exp(-jnp.abs(x)))


def _tri(q, lower):
    r = lax.broadcasted_iota(jnp.int32, (q, q), 0)
    c = lax.broadcasted_iota(jnp.int32, (q, q), 1)
    return (r >= c) if lower else (r <= c)


def _inproj_kernel(x_ref, g_ref, wm_ref, ws_ref, wst_ref, oa_ref, ob_ref, oc_ref, od_ref, os_ref, ost_ref):
    for rs in _sub_tiles(x_ref.shape[0]):
        h = _rms(x_ref[rs, :], g_ref[...]).astype(BF16)
        for o_ref, (lo, hi) in ((oa_ref, MAIN_SSD), (ob_ref, MAIN_SWA), (oc_ref, MAIN_S5), (od_ref, MAIN_ML)):
            o_ref[rs, :] = jnp.dot(h, wm_ref[:, lo:hi], preferred_element_type=F32)
        os_ref[rs, :] = jnp.dot(h, ws_ref[...], preferred_element_type=F32)
        ost_ref[0, :, rs] = lax.dot_general(wst_ref[...], h, (((1,), (1,)), ((), ())), preferred_element_type=F32)


def _inproj(x, wts, l):
    ta = x.shape[0]
    tm = TOKEN_TILE
    widths = [hi - lo for lo, hi in (MAIN_SSD, MAIN_SWA, MAIN_S5, MAIN_ML)] + [SMALL_COLS]
    row = lambda n: pl.BlockSpec((tm, n), lambda i: (i, 0))
    ws = (wts["norm_mix"], wts["w_main"], wts["w_small"], wts["w_small_t"])
    return pl.pallas_call(
        _inproj_kernel,
        grid=(ta // tm,),
        in_specs=[row(D_MODEL)] + [_layer(w, l) for w in ws],
        out_specs=[row(n) for n in widths] + [pl.BlockSpec((1, SMALL_ROWS, tm), lambda i: (0, 0, i))],
        out_shape=[jax.ShapeDtypeStruct((ta, n), F32) for n in widths]
        + [jax.ShapeDtypeStruct((1, SMALL_ROWS, ta), F32)],
        compiler_params=_params(1),
        name="inproj",
    )(x, *ws)


def _ssd_kernel(p_ref, sm_ref, smt_ref, cinit_ref, sinit_ref, cw_ref, cb_ref, dtb_c_ref, dtb_r_ref,
                alog_c_ref, alog_r_ref, dskip_ref, g_ref, y_ref, sout_ref, ext_sc, st_sc, *, q, nsub, valid):
    c = pl.program_id(1)
    rows = q * nsub

    @pl.when(c == 0)
    def _():
        ext_sc[0:8, :] = cinit_ref[...]
        st_sc[...] = sinit_ref[...]

    ext_sc[8:8 + rows, :] = p_ref[:, SSD_D_INNER:SSD_SEG]
    conv = cb_ref[...] + cw_ref[0:1, :] * ext_sc[5:5 + rows, :]
    for j in range(1, SSD_CONV):
        conv = conv + cw_ref[j:j + 1, :] * ext_sc[5 + j:5 + j + rows, :]
    ext_sc[0:8, :] = ext_sc[rows:rows + 8, :]
    xbc_all = conv * _sigmoid(conv)
    z_all = p_ref[:, :SSD_D_INNER]

    dt_c_all = _softplus(sm_ref[:, 0:SSD_HEADS] + dtb_c_ref[...])
    dt_r_all = _softplus(smt_ref[0, 0:SSD_HEADS, :] + dtb_r_ref[...])
    if valid < q:
        dt_c_all = jnp.where(lax.broadcasted_iota(jnp.int32, dt_c_all.shape, 0) < valid, dt_c_all, 0.0)
        dt_r_all = jnp.where(lax.broadcasted_iota(jnp.int32, dt_r_all.shape, 1) < valid, dt_r_all, 0.0)
    da_c_all = dt_c_all * (-jnp.exp(alog_c_ref[...]))
    da_r_all = dt_r_all * (-jnp.exp(alog_r_ref[...]))
    causal = _tri(q, True)
    lower = jnp.where(causal, 1.0, 0.0).astype(BF16)
    upper = jnp.where(_tri(q, False), 1.0, 0.0).astype(BF16)

    rep = SSD_HEADS // SSD_GROUPS
    c_off = SSD_D_INNER + SSD_GROUPS * SSD_STATE
    states = [st_sc[h] for h in range(SSD_HEADS)]
    for j in range(nsub):
        rs = slice(j * q, (j + 1) * q)
        xbc, dt_c, dt_r = xbc_all[rs], dt_c_all[rs], dt_r_all[:, rs]
        xs = xbc[:, :SSD_D_INNER]
        cum_c = _dot_exact_l(lower, da_c_all[rs])
        cum_r = _dot_exact_r(da_r_all[:, rs], upper)
        ys = []
        for h in range(SSD_HEADS):
            grp = h // rep
            b_g = xbc[:, SSD_D_INNER + grp * SSD_STATE:SSD_D_INNER + (grp + 1) * SSD_STATE]
            c_g = xbc[:, c_off + grp * SSD_STATE:c_off + (grp + 1) * SSD_STATE]
            cb = _dot_nt(c_g, b_g)
            x_h = xs[:, h * SSD_HEAD_DIM:(h + 1) * SSD_HEAD_DIM]
            cc = cum_c[:, h:h + 1]
            seg = jnp.where(causal, cc - cum_r[h:h + 1, :], MASKED)
            w = cb * jnp.exp(seg) * dt_r[h:h + 1, :]
            s_h = states[h]
            ys.append(_dot(w, x_h) + jnp.exp(cc) * _dot_nt(c_g, s_h))
            last = cum_c[q - 1:q, h:h + 1]
            tail = jnp.exp(last - cc) * dt_c[:, h:h + 1]
            states[h] = s_h * jnp.exp(last) + _dot_tn(x_h * tail, b_g)
        z = z_all[rs]
        y = (jnp.concatenate(ys, axis=1) + dskip_ref[...] * xs) * (z * _sigmoid(z))
        y_ref[rs, :] = _rms(y, g_ref[...])
    for h in range(SSD_HEADS):
        st_sc[h] = states[h]

    @pl.when(c == pl.num_programs(1) - 1)
    def _():
        sout_ref[0] = st_sc[...]


def _seq_specs(rows_blk, nc, blk0, p_small_t):
    rows = lambda n: pl.BlockSpec((rows_blk, n), lambda b, c: (blk0 + b * nc + c, 0))
    if p_small_t.shape[0] == 1:
        smt_spec = pl.BlockSpec((1, SMALL_ROWS, rows_blk), lambda b, c: (0, 0, blk0 + b * nc + c))
    else:
        smt_spec = pl.BlockSpec((1, SMALL_ROWS, rows_blk), lambda b, c: (b, 0, 0))
    return rows, smt_spec


def _state_spec(a, sl):
    nd = a.ndim
    return pl.BlockSpec((None, None) + a.shape[2:], lambda b, c: (sl, b) + (0,) * (nd - 2))


def _ssd_call(p_ssd, p_small, p_small_t, conv_init, state_init, sl, wts, l, *, nb, nc, q, nsub, valid, row0, prev):
    ta = p_ssd.shape[0]
    blk = q * nsub
    assert row0 % blk == 0 and (valid == q or nsub == 1)
    rows, smt_spec = _seq_specs(blk, nc, row0 // blk, p_small_t)
    ws = tuple(wts[k] for k in ("ssd_conv_w", "ssd_conv_b", "dtb_c", "dtb_r", "alog_c", "alog_r", "ssd_dskip",
                                "ssd_norm"))
    in_specs = [rows(SSD_SEG), rows(SMALL_COLS), smt_spec, _state_spec(conv_init, sl), _state_spec(state_init, sl)]
    in_specs += [_layer(w, l) for w in ws]
    args = [p_ssd, p_small, p_small_t, conv_init, state_init, *ws]
    kern, aliases = _with_aliased(functools.partial(_ssd_kernel, q=q, nsub=nsub, valid=valid), in_specs, args,
                                  [] if prev is None else [(prev, 0)])
    return pl.pallas_call(
        kern,
        grid=(nb, nc),
        in_specs=in_specs,
        out_specs=[rows(SSD_D_INNER),
                   pl.BlockSpec((1, SSD_HEADS, SSD_HEAD_DIM, SSD_STATE), lambda b, c: (b, 0, 0, 0))],
        out_shape=[jax.ShapeDtypeStruct((ta, SSD_D_INNER), F32),
                   jax.ShapeDtypeStruct((nb, SSD_HEADS, SSD_HEAD_DIM, SSD_STATE), F32)],
        scratch_shapes=[pltpu.VMEM((blk + 8, SSD_CONV_DIM), F32),
                        pltpu.VMEM((SSD_HEADS, SSD_HEAD_DIM, SSD_STATE), F32)],
        input_output_aliases=aliases,
        compiler_params=_params(2),
        name="ssd_q%d" % q,
    )(*args)


def _mlstm_kernel(p_ref, sm_ref, smt_ref, cinit_ref, ninit_ref, minit_ref, ib_c_ref, ib_r_ref, fb_c_ref, fb_r_ref,
                  g_ref, y_ref, cout_ref, nout_ref, mout_ref, c_sc, n_sc, m_sc, *, q, nsub, valid):
    c = pl.program_id(1)

    @pl.when(c == 0)
    def _():
        c_sc[...] = cinit_ref[...]
        n_sc[...] = ninit_ref[...]
        m_sc[...] = minit_ref[...]

    w = ML_WIDTH
    i0 = SSD_HEADS
    f0 = SSD_HEADS + ML_HEADS
    ig_c_all = sm_ref[:, i0:i0 + ML_HEADS] + ib_c_ref[...]
    fg_c = sm_ref[:, f0:f0 + ML_HEADS] + fb_c_ref[...]
    ig_r_all = smt_ref[0, i0:i0 + ML_HEADS, :] + ib_r_ref[...]
    fg_r = smt_ref[0, f0:f0 + ML_HEADS, :] + fb_r_ref[...]
    lf_c_all = -_softplus(-fg_c)
    lf_r_all = -_softplus(-fg_r)
    if valid < q:
        ok_c = lax.broadcasted_iota(jnp.int32, ig_c_all.shape, 0) < valid
        ok_r = lax.broadcasted_iota(jnp.int32, ig_r_all.shape, 1) < valid
        ig_c_all = jnp.where(ok_c, ig_c_all, MASKED)
        ig_r_all = jnp.where(ok_r, ig_r_all, MASKED)
        lf_c_all = jnp.where(ok_c, lf_c_all, 0.0)
        lf_r_all = jnp.where(ok_r, lf_r_all, 0.0)
    causal = _tri(q, True)
    lower = jnp.where(causal, 1.0, 0.0).astype(BF16)
    upper = jnp.where(_tri(q, False), 1.0, 0.0).astype(BF16)

    scale = ML_HEAD_DIM ** -0.5
    c_st = [c_sc[h] for h in range(ML_HEADS)]
    n_st = [n_sc[h] for h in range(ML_HEADS)]
    m_st = [m_sc[h][:, 0:1] for h in range(ML_HEADS)]
    for j in range(nsub):
        rs = slice(j * q, (j + 1) * q)
        ig_c, ig_r = ig_c_all[rs], ig_r_all[:, rs]
        b_c = _dot_exact_l(lower, lf_c_all[rs])
        b_r = _dot_exact_r(lf_r_all[:, rs], upper)
        ys = []
        for h in range(ML_HEADS):
            hs = slice(h * ML_HEAD_DIM, (h + 1) * ML_HEAD_DIM)
            q_h = p_ref[rs, hs]
            k_h = p_ref[rs, w + h * ML_HEAD_DIM:w + (h + 1) * ML_HEAD_DIM] * scale
            v_h = p_ref[rs, 2 * w + h * ML_HEAD_DIM:2 * w + (h + 1) * ML_HEAD_DIM]
            o_h = p_ref[rs, 3 * w + h * ML_HEAD_DIM:3 * w + (h + 1) * ML_HEAD_DIM]
            m_prev, c_h, n_h = m_st[h], c_st[h], n_st[h]
            bc = b_c[:, h:h + 1]
            intra = jnp.where(causal, bc - b_r[h:h + 1, :] + ig_r[h:h + 1, :], MASKED)
            inter = bc + m_prev
            m_t = jnp.maximum(inter, jnp.max(intra, axis=1, keepdims=True))
            wgt = _dot_nt(q_h, k_h) * jnp.exp(intra - m_t)
            w_inter = jnp.exp(inter - m_t)
            num = _dot(wgt, v_h) + w_inter * _dot(q_h, c_h)
            den = jnp.sum(wgt, axis=1, keepdims=True) + w_inter * jnp.sum(q_h * n_h, axis=1, keepdims=True)
            hh = num / jnp.maximum(jnp.abs(den), jnp.exp(-m_t))
            m_new = m_t[q - 1:q, :]
            b_last = b_c[q - 1:q, h:h + 1]
            wk = jnp.exp(b_last - bc + ig_c[:, h:h + 1] - m_new)
            decay = jnp.exp(b_last + m_prev - m_new)
            kw = k_h * wk
            c_st[h] = decay * c_h + _dot_tn(kw, v_h)
            n_st[h] = decay * n_h + jnp.sum(kw, axis=0, keepdims=True)
            m_st[h] = m_new
            hn = hh * lax.rsqrt(jnp.mean(hh * hh, axis=-1, keepdims=True) + RMS_EPS) * g_ref[:, hs]
            ys.append(_sigmoid(o_h) * hn)
        y_ref[rs, :] = jnp.concatenate(ys, axis=1)
    for h in range(ML_HEADS):
        c_sc[h] = c_st[h]
        n_sc[h] = n_st[h]
        m_sc[h] = jnp.broadcast_to(m_st[h], (1, ML_HEAD_DIM))

    @pl.when(c == pl.num_programs(1) - 1)
    def _():
        cout_ref[0] = c_sc[...]
        nout_ref[0] = n_sc[...]
        mout_ref[0] = m_sc[...]


def _mlstm_call(p_ml, p_small, p_small_t, c_init, n_init, m_init, sl, wts, l, *, nb, nc, q, nsub, valid, row0,
                prev):
    ta = p_ml.shape[0]
    blk = q * nsub
    assert row0 % blk == 0 and (valid == q or nsub == 1)
    rows, smt_spec = _seq_specs(blk, nc, row0 // blk, p_small_t)
    ws = tuple(wts[k] for k in ("ib_c", "ib_r", "fb_c", "fb_r", "mlstm_norm"))
    c_spec = pl.BlockSpec((1, ML_HEADS, ML_HEAD_DIM, ML_HEAD_DIM), lambda b, c: (b, 0, 0, 0))
    v_spec = pl.BlockSpec((1, ML_HEADS, 1, ML_HEAD_DIM), lambda b, c: (b, 0, 0, 0))
    in_specs = [rows(4 * ML_WIDTH), rows(SMALL_COLS), smt_spec, _state_spec(c_init, sl), _state_spec(n_init, sl),
                _state_spec(m_init, sl)]
    in_specs += [_layer(w, l) for w in ws]
    args = [p_ml, p_small, p_small_t, c_init, n_init, m_init, *ws]
    kern, aliases = _with_aliased(functools.partial(_mlstm_kernel, q=q, nsub=nsub, valid=valid), in_specs, args,
                                  [] if prev is None else [(prev, 0)])
    vec = jax.ShapeDtypeStruct((nb, ML_HEADS, 1, ML_HEAD_DIM), F32)
    return pl.pallas_call(
        kern,
        grid=(nb, nc),
        in_specs=in_specs,
        out_specs=[rows(ML_WIDTH), c_spec, v_spec, v_spec],
        out_shape=[jax.ShapeDtypeStruct((ta, ML_WIDTH), F32),
                   jax.ShapeDtypeStruct((nb, ML_HEADS, ML_HEAD_DIM, ML_HEAD_DIM), F32), vec, vec],
        scratch_shapes=[pltpu.VMEM((ML_HEADS, ML_HEAD_DIM, ML_HEAD_DIM), F32),
                        pltpu.VMEM((ML_HEADS, 1, ML_HEAD_DIM), F32),
                        pltpu.VMEM((ML_HEADS, 1, ML_HEAD_DIM), F32)],
        input_output_aliases=aliases,
        compiler_params=_params(2),
        name="mlstm_q%d" % q,
    )(*args)


def _s5_kernel(u_ref, h0_ref, btc_ref, smc_ref, cmc_ref, e_ref, pa_ref, pb_ref, va_ref, vb_ref,
               y_ref, hout_ref, tz_ref, sm_ref, cm_ref, *, rows, scan):
    half = S5_SUPER_STATE // 2
    q, lanes, ng = S5_CHUNK, S5_LANES, S5_SUPER_GROUPS

    @pl.when(pl.program_id(1) == 0)
    def _():
        def expand(compact):
            hi = compact.astype(BF16)
            lo = (compact - hi.astype(F32)).astype(BF16)
            full = (jnp.dot(hi, e_ref[...], preferred_element_type=F32)
                    + jnp.dot(lo, e_ref[...], preferred_element_type=F32))
            rg = (lax.broadcasted_iota(jnp.int32, full.shape, 0) >> 4) & (ng - 1)
            cg = (lax.broadcasted_iota(jnp.int32, full.shape, 1) >> 6) & (ng - 1)
            return jnp.where(rg == cg, full, 0.0)

        def split(x):
            hi = x.astype(BF16)
            return hi, (x - hi.astype(F32)).astype(BF16)

        for r0 in range(0, q * lanes, 4 * lanes):
            sm_ref[r0:r0 + 4 * lanes, :] = expand(smc_ref[r0:r0 + 4 * lanes, :]).astype(BF16)
        bt_hi, bt_lo = split(expand(btc_ref[...]))
        nt = lambda a, b: lax.dot_general(a, b, (((1,), (1,)), ((), ())), preferred_element_type=F32)
        zero = jnp.zeros((lanes, lanes), BF16)
        for lag in range(q + 1):
            blk = expand(cmc_ref[lag * lanes:(lag + 1) * lanes, :])
            cm_ref[lag * lanes:(lag + 1) * lanes, :] = blk.astype(BF16)
            if lag < q:
                c_hi, c_lo = split(blk)
                k_lag = (nt(bt_hi, c_hi) + nt(bt_hi, c_lo) + nt(bt_lo, c_hi)).astype(BF16)
                for s in range(q - lag):
                    t = s + lag
                    tz_ref[s * lanes:(s + 1) * lanes, t * lanes:(t + 1) * lanes] = k_lag
                    if lag > 0:
                        tz_ref[t * lanes:(t + 1) * lanes, s * lanes:(s + 1) * lanes] = zero

    ucat = jnp.concatenate([u_ref[pl.ds(s, rows, stride=S5_CHUNK), :] for s in range(S5_CHUNK)], axis=1)
    ub = ucat.astype(BF16)

    def cmul(a, b, x):
        return a * x + b * pltpu.roll(x, half, axis=1)

    contrib = jnp.dot(ub, sm_ref[...], preferred_element_type=F32)
    h0 = h0_ref[0]
    carried = cmul(va_ref[...], vb_ref[...], h0)
    if scan:
        ridx = lax.broadcasted_iota(jnp.int32, (rows, S5_SUPER_STATE), 0)
        x = contrib + jnp.where(ridx == 0, carried, 0.0)
        k = 0
        while (1 << k) < rows:
            s = 1 << k
            shifted = jnp.where(ridx >= s, pltpu.roll(x, s, axis=0), 0.0)
            x = x + cmul(pa_ref[k:k + 1, :], pb_ref[k:k + 1, :], shifted)
            k += 1
        hprev = jnp.where(ridx == 0, h0, pltpu.roll(x, 1, axis=0))
        hout_ref[0] = x[rows - 1:rows, :]
    else:
        x = contrib + carried
        hprev = h0
        hout_ref[0] = x
    y = jnp.dot(ub, tz_ref[...], preferred_element_type=F32) + _dot_nt(hprev, cm_ref[lanes:, :])
    for t in range(S5_CHUNK):
        y_ref[pl.ds(t, rows, stride=S5_CHUNK), :] = y[:, t * S5_LANES:(t + 1) * S5_LANES]


def _s5_call(p_s5, h0, tab, l, *, nseq, rows, scan, row0, prev):
    ta = p_s5.shape[0]
    blk_rows = rows * S5_CHUNK
    blk0 = row0 // blk_rows
    hrows = h0.shape[2]
    tok = pl.BlockSpec((blk_rows, S5_LANES), lambda sb, b: (blk0 + b, sb))
    hspec = pl.BlockSpec((None, 1, hrows, S5_SUPER_STATE), lambda sb, b: (sb, b, 0, 0))
    va, vb = (tab["va16"], tab["vb16"]) if scan else (tab["va"], tab["vb"])
    per_sb = lambda a: pl.BlockSpec((None, None) + a.shape[2:], lambda sb, b: (l, sb) + (0,) * (a.ndim - 2))
    tabs = [(tab["btc"], per_sb), (tab["smc16"] if scan else tab["smc"], per_sb), (tab["cmc"], per_sb),
            (tab["e"], lambda a: _full(a.shape)),
            (tab["pa"], per_sb), (tab["pb"], per_sb), (va, per_sb), (vb, per_sb)]
    in_specs = [tok, hspec] + [mk(a) for a, mk in tabs]
    args = [p_s5, h0] + [a for a, _ in tabs]
    kern, aliases = _with_aliased(functools.partial(_s5_kernel, rows=rows, scan=scan), in_specs, args,
                                  [] if prev is None else [(prev, 0)])
    folded = S5_CHUNK * S5_LANES
    return pl.pallas_call(
        kern,
        grid=(S5_SUPER, nseq),
        in_specs=in_specs,
        out_specs=[tok, hspec],
        out_shape=[jax.ShapeDtypeStruct((ta, S5_WIDTH), F32), jax.ShapeDtypeStruct(h0.shape, F32)],
        scratch_shapes=[pltpu.VMEM((folded, folded), BF16), pltpu.VMEM((folded, S5_SUPER_STATE), BF16),
                        pltpu.VMEM((folded + S5_LANES, S5_SUPER_STATE), BF16)],
        input_output_aliases=aliases,
        compiler_params=_params(2),
        name="s5_scan" if scan else "s5_step",
    )(*args)


def _s5_tables(a_re, a_im, log_dt, b_re, b_im, c_re, c_im, valid):
    q = S5_CHUNK
    depth = a_re.shape[0]
    nsb, ng = S5_SUPER, S5_SUPER_GROUPS
    dt = jnp.exp(log_dt)[..., None]
    mag = jnp.exp(a_re * dt)
    ab_re = mag * jnp.cos(a_im * dt)
    ab_im = mag * jnp.sin(a_im * dt)
    inv = 1.0 / (a_re * a_re + a_im * a_im)
    co_re = ((ab_re - 1.0) * a_re + ab_im * a_im) * inv
    co_im = (ab_im * a_re - (ab_re - 1.0) * a_im) * inv

    def cmul(x, y):
        return x[0] * y[0] - x[1] * y[1], x[0] * y[1] + x[1] * y[0]

    pw = [(jnp.ones_like(ab_re), jnp.zeros_like(ab_re))]
    for _ in range(q):
        pw.append(cmul(pw[-1], (ab_re, ab_im)))
    row = lambda v: v[:, :, None, :]
    bt_t = (jnp.transpose(b_re, (0, 1, 3, 2)), jnp.transpose(b_im, (0, 1, 3, 2)))
    bt_re = row(co_re) * bt_t[0] - row(co_im) * bt_t[1]
    bt_im = row(co_re) * bt_t[1] + row(co_im) * bt_t[0]

    def compact(blocks):
        x = jnp.stack(blocks, axis=2)
        x = x.reshape(depth, nsb, ng, len(blocks), S5_GROUP, 2 * S5_STATE)
        return jnp.transpose(x, (0, 1, 3, 2, 4, 5)).reshape(depth, nsb, len(blocks) * S5_LANES, 2 * S5_STATE)

    def times(p):
        pr, pi = row(p[0]), row(p[1])
        b = jnp.concatenate([pr * bt_re - pi * bt_im, pr * bt_im + pi * bt_re], axis=-1)
        c = jnp.concatenate([c_re * pr - c_im * pi, -(c_re * pi + c_im * pr)], axis=-1)
        return b, c

    def state_in(nvalid):
        zero = jnp.zeros((depth, S5_GROUPS, S5_GROUP, 2 * S5_STATE), F32)
        return compact([times(pw[nvalid - 1 - s])[0] if s < nvalid else zero for s in range(q)])

    btc = compact([times(pw[0])[0]])
    cmc = compact([times(p)[1] for p in pw])
    e = np.zeros((2, S5_STATE, 2, ng, S5_STATE), np.float32)
    for g in range(ng):
        e[:, :, :, g, :] = np.eye(2 * S5_STATE).reshape(2, S5_STATE, 2, S5_STATE)
    e = jnp.asarray(e.reshape(2 * S5_STATE, S5_SUPER_STATE), BF16)

    def packed(p):
        pr = p[0].reshape(depth, nsb, ng * S5_STATE)
        pi = p[1].reshape(depth, nsb, ng * S5_STATE)
        return jnp.concatenate([pr, pr], axis=-1), jnp.concatenate([-pi, pi], axis=-1)

    doubling = [pw[q]]
    for _ in range(S5_DOUBLINGS - 1):
        doubling.append(cmul(doubling[-1], doubling[-1]))
    pa = jnp.stack([packed(p)[0] for p in doubling], axis=2)
    pb = jnp.stack([packed(p)[1] for p in doubling], axis=2)
    va16, vb16 = packed(pw[q])
    va, vb = packed(pw[valid])
    ex = lambda a: a[:, :, None, :]
    return {"btc": btc, "smc16": state_in(q), "smc": state_in(valid), "cmc": cmc, "e": e,
            "pa": pa, "pb": pb, "va16": ex(va16), "vb16": ex(vb16), "va": ex(va), "vb": ex(vb)}


def _t5_bucket(dist):
    dist = np.asarray(dist)
    large = REL_MAX_EXACT + (np.log(np.maximum(dist, 1) / REL_MAX_EXACT)
                             / math.log(REL_MAX_DIST / REL_MAX_EXACT)
                             * (REL_BUCKETS - REL_MAX_EXACT)).astype(np.int32)
    large = np.minimum(large, REL_BUCKETS - 1)
    return np.where(dist < REL_MAX_EXACT, dist, large).astype(np.int32)


def _bias_steps(rel_bias, grp, dil):
    heads = slice(grp * SWA_GROUP_HEADS, (grp + 1) * SWA_GROUP_HEADS)
    buckets = _t5_bucket(dil * np.arange(SWA_BLOCK + 1))
    onehot = np.zeros((SWA_BLOCK + 1, REL_BUCKETS), np.float32)
    onehot[np.arange(SWA_BLOCK + 1), buckets] = 1.0
    steps = jnp.einsum("jb,bh->hj", onehot, rel_bias[:, heads], precision=lax.Precision.HIGHEST)
    return steps.astype(F32)


def _softmax_pieces(logits):
    m = logits[0].max(axis=1, keepdims=True)
    for s in logits[1:]:
        m = jnp.maximum(m, s.max(axis=1, keepdims=True))
    ps = [jnp.exp(s - m) for s in logits]
    den = ps[0].sum(axis=1, keepdims=True)
    for p in ps[1:]:
        den = den + p.sum(axis=1, keepdims=True)
    return ps, den, m + jnp.log(den)


SWA_SAMPLE_LANES = 2048
SWA_UNROLL = 4
SWA_DENSE_BLOCKS = 4


def _swa_prompt_kernel(q_ref, k_ref, v_ref, kp_ref, vp_ref, bias_ref, o_ref, lse_ref, kvt_ref, *, dil):
    n = SWA_BLOCK
    pair = pl.program_id(1)
    first = pl.program_id(2) == 0
    scale = SWA_HEAD_DIM ** -0.5

    def tile(sl, kp, vp, mask_prev):
        qq, kk, vv = q_ref[sl, :], k_ref[sl, :], v_ref[sl, :]
        outs, lses = [], []
        for j in range(2):
            hs = slice(j * SWA_HEAD_DIM, (j + 1) * SWA_HEAD_DIM)
            bias = bias_ref[2 * pair + j]
            bias_prev = bias[:, 0:n] if mask_prev is None else jnp.where(mask_prev, MASKED, bias[:, 0:n])
            q_h = qq[:, hs] * scale
            ps, den, lse = _softmax_pieces([_dot_nt(q_h, kk[:, hs]) + bias[:, n:2 * n],
                                            _dot_nt(q_h, kp[:, hs]) + bias_prev])
            outs.append((_dot(ps[0], vv[:, hs]) + _dot(ps[1], vp[:, hs])) / den)
            lses.append(jnp.broadcast_to(lse, (n, SWA_HEAD_DIM)))
        o_ref[sl, :] = jnp.concatenate(outs, axis=1)
        lse_ref[sl, :] = jnp.concatenate(lses, axis=1)

    if dil == 1:
        for j in range(SWA_DENSE_BLOCKS):
            if j == 0:
                tile(slice(0, n), kp_ref[...], vp_ref[...], first)
            else:
                before = slice((j - 1) * n, j * n)
                tile(slice(j * n, (j + 1) * n), k_ref[before, :], v_ref[before, :], None)
    else:
        def group(i, carry):
            for u in range(SWA_UNROLL):
                sl = pl.ds(i * SWA_UNROLL + u, n, stride=dil)
                tile(sl, kp_ref[sl, :], vp_ref[sl, :], first)
            return carry
        if dil == SWA_UNROLL:
            group(0, 0)
        else:
            lax.fori_loop(0, dil // SWA_UNROLL, group, 0)
    rows = k_ref.shape[0]
    kvt_ref[0] = k_ref[rows - n * dil:rows, :].T
    kvt_ref[1] = v_ref[rows - n * dil:rows, :].T


def _swa_prompt_call(p_swa, bias, kvt_prev, l, depth, *, grp, dil, nb, seq):
    ta = p_swa.shape[0]
    win = SWA_BLOCK * dil
    sb = SWA_BLOCK * SWA_DENSE_BLOCKS if dil == 1 else win
    assert dil == 1 or dil % SWA_UNROLL == 0
    nsb = seq // sb
    lanes = 2 * SWA_HEAD_DIM
    npair = SWA_GROUP_HEADS // 2
    cur = lambda col: pl.BlockSpec((sb, lanes), lambda b, p, c: (b * nsb + c, 2 * col + p))
    if dil == 1:
        per = sb // SWA_BLOCK
        prv = lambda col: pl.BlockSpec(
            (SWA_BLOCK, lanes), lambda b, p, c: (jnp.maximum((b * nsb + c) * per - 1, 0), 2 * col + p))
    else:
        prv = lambda col: pl.BlockSpec((sb, lanes), lambda b, p, c: (b * nsb + jnp.maximum(c - 1, 0), 2 * col + p))
    out = pl.BlockSpec((sb, lanes), lambda b, p, c: (b * nsb + c, p))
    kvt_spec = pl.BlockSpec((None, None, 2, None, lanes, win), lambda b, p, c: (l, b, 0, p, 0, 0))
    shape = jax.ShapeDtypeStruct((ta, SWA_GROUP_WIDTH), F32)
    in_specs = [cur(grp), cur(3 + grp), cur(6 + grp), prv(3 + grp), prv(6 + grp), _full(bias.shape)]
    args = [p_swa, p_swa, p_swa, p_swa, p_swa, bias]
    kern, aliases = _with_aliased(functools.partial(_swa_prompt_kernel, dil=dil), in_specs, args,
                                  [] if kvt_prev is None else [(kvt_prev, 2)])
    return pl.pallas_call(
        kern,
        grid=(nb, npair, nsb),
        in_specs=in_specs,
        out_specs=[out, out, kvt_spec],
        out_shape=[shape, shape, jax.ShapeDtypeStruct((depth, nb, 2, npair, lanes, win), F32)],
        input_output_aliases=aliases,
        compiler_params=_params(3),
        name="swa_prompt_d%d" % dil,
    )(*args)


def _swa_prompt_bias(steps):
    n = SWA_BLOCK
    period = 3 * n + 1
    f = jnp.concatenate([steps[:, ::-1], jnp.full((steps.shape[0], period - (n + 1)), MASKED, F32)], axis=1)
    tiled = jnp.tile(f, (1, n))[:, :n * (period - 1)]
    return tiled.reshape(steps.shape[0], n, period - 1)[:, :, :2 * n]


def _swa_sample_kernel(q_ref, k_ref, v_ref, buf_ref, bias_buf_ref, bias_new_ref, o_ref, lse_ref, cache_ref, *,
                       width, t_new):
    scale = SWA_HEAD_DIM ** -0.5
    w = SWA_GROUP_WIDTH
    for i in range(buf_ref.shape[0]):
        rs = slice(i * SAMPLE_ROWS, (i + 1) * SAMPLE_ROWS)
        qq, kn, vn = q_ref[rs, :], k_ref[rs, :], v_ref[rs, :]
        outs, lses = [], []
        for h in range(SWA_GROUP_HEADS):
            hs = slice(h * SWA_HEAD_DIM, (h + 1) * SWA_HEAD_DIM)
            k_t = buf_ref[i, h * SWA_HEAD_DIM:(h + 1) * SWA_HEAD_DIM, :]
            v_t = buf_ref[i, w + h * SWA_HEAD_DIM:w + (h + 1) * SWA_HEAD_DIM, :]
            q_h = qq[:, hs] * scale
            ps, den, lse = _softmax_pieces([_dot(q_h, k_t) + bias_buf_ref[h],
                                            _dot_nt(q_h, kn[:, hs]) + bias_new_ref[h]])
            outs.append((_dot_nt(ps[0], v_t) + _dot(ps[1], vn[:, hs])) / den)
            lses.append(jnp.broadcast_to(lse, (SAMPLE_ROWS, SWA_HEAD_DIM)))
        o_ref[rs, :] = jnp.concatenate(outs, axis=1)
        lse_ref[rs, :] = jnp.concatenate(lses, axis=1)
        new_t = jnp.concatenate([kn, vn], axis=1).T
        cache_ref[i] = pltpu.roll(buf_ref[i], width - t_new, axis=1)
        cache_ref[i, :, width - t_new:width] = new_t[:, 0:t_new]


def _swa_sample_call(p_swa, cache_t, bias_buf, bias_new, prev_o, prev_lse, prev_cache, l, *, grp, nb, row0, t_new):
    ta = p_swa.shape[0]
    width = cache_t.shape[3]
    w = SWA_GROUP_WIDTH
    per = max(1, min(nb, SWA_SAMPLE_LANES // width))
    while nb % per:
        per -= 1
    assert row0 % (SAMPLE_ROWS * per) == 0
    blk0 = row0 // (SAMPLE_ROWS * per)
    nb = nb // per
    tok = lambda col: pl.BlockSpec((SAMPLE_ROWS * per, w), lambda b: (blk0 + b, col))
    cache_spec = pl.BlockSpec((None, per, 2 * w, width), lambda b: (l, b, 0, 0))
    in_specs = [tok(grp), tok(3 + grp), tok(6 + grp), cache_spec, _full(bias_buf.shape), _full(bias_new.shape)]
    args = [p_swa, p_swa, p_swa, cache_t, bias_buf, bias_new]
    aliased = [(prev_o, 0), (prev_lse, 1)] + ([] if prev_cache is None else [(prev_cache, 2)])
    kern, aliases = _with_aliased(functools.partial(_swa_sample_kernel, width=width, t_new=t_new), in_specs, args,
                                  aliased)
    return pl.pallas_call(
        kern,
        grid=(nb,),
        in_specs=in_specs,
        out_specs=[tok(0), tok(0), cache_spec],
        out_shape=[jax.ShapeDtypeStruct((ta, w), F32), jax.ShapeDtypeStruct((ta, w), F32),
                   jax.ShapeDtypeStruct(cache_t.shape, F32)],
        input_output_aliases=aliases,
        compiler_params=_params(1),
        name="swa_sample_w%d" % width,
    )(*args)


def _swa_sample_bias(steps, window, dil, width, t_new):
    n = window // dil
    nh = steps.shape[0]
    g = jnp.pad(steps[:, :, None], ((0, 0), (0, 0), (0, dil - 1)), constant_values=MASKED).reshape(nh, (n + 1) * dil)
    g = jnp.pad(g, ((0, 0), (0, width + SAMPLE_ROWS)), constant_values=MASKED)
    masked_row = jnp.full((nh, 1, width), MASKED, F32)
    rows = [g[:, t + 1:t + 1 + width][:, None, ::-1] if t < t_new else masked_row for t in range(SAMPLE_ROWS)]
    b_buf = jnp.concatenate(rows, axis=1)
    new_rows = []
    for t in range(SAMPLE_ROWS):
        if t < t_new:
            row = jnp.concatenate([g[:, 0:t + 1][:, ::-1], jnp.full((nh, SAMPLE_ROWS - t - 1), MASKED, F32)], axis=1)
        else:
            row = jnp.full((nh, SAMPLE_ROWS), MASKED, F32)
        new_rows.append(row[:, None, :])
    return b_buf, jnp.concatenate(new_rows, axis=1)


def _merge_kernel(x_ref, yssd_ref, o0_ref, l0_ref, o1_ref, l1_ref, o2_ref, l2_ref, ys5_ref, u_ref, yml_ref,
                  g_ref, wg_ref, wssd_ref, wswa_ref, ws5_ref, wml_ref, wglu_ref, wout_ref, d_ref, out_ref):
    for rs in _sub_tiles(x_ref.shape[0]):
        x = x_ref[rs, :]
        h = _rms(x, g_ref[...]).astype(BF16)
        l0, l1, l2 = l0_ref[rs, :], l1_ref[rs, :], l2_ref[rs, :]
        m = jnp.maximum(jnp.maximum(l0, l1), l2)
        e0, e1, e2 = jnp.exp(l0 - m), jnp.exp(l1 - m), jnp.exp(l2 - m)
        y_swa = (e0 * o0_ref[rs, :] + e1 * o1_ref[rs, :] + e2 * o2_ref[rs, :]) / (e0 + e1 + e2)
        y5 = ys5_ref[rs, :] + d_ref[...] * u_ref[rs, :]
        y_s5 = y5 * _sigmoid(_dot(y5, wglu_ref[...]))
        branches = ((yssd_ref[rs, :], wssd_ref), (y_swa, wswa_ref), (y_s5, ws5_ref), (yml_ref[rs, :], wml_ref))
        merged = None
        for i, (y, w_ref) in enumerate(branches):
            gate = _sigmoid(jnp.dot(h, wg_ref[:, i * D_MODEL:(i + 1) * D_MODEL], preferred_element_type=F32))
            term = gate * _dot(y, w_ref[...])
            merged = term if merged is None else merged + term
        out_ref[rs, :] = x + _dot(merged, wout_ref[...])


def _merge_call(x, acts, wts, l):
    ta = x.shape[0]
    tm = TOKEN_TILE
    row = lambda a: pl.BlockSpec((tm, a.shape[1]), lambda i: (i, 0))
    ws = tuple(wts[k] for k in ("norm_mix", "w_gate", "w_br_ssd", "w_br_swa", "w_br_s5", "w_br_mlstm", "s5_w_glu",
                                "w_out", "s5_d"))
    return pl.pallas_call(
        _merge_kernel,
        grid=(ta // tm,),
        in_specs=[row(a) for a in (x, *acts)] + [_layer(w, l) for w in ws],
        out_specs=row(x),
        out_shape=jax.ShapeDtypeStruct(x.shape, F32),
        compiler_params=_params(1),
        name="merge",
    )(x, *acts, *ws)


def _norm_matmul_kernel(x_ref, g_ref, w_ref, o_ref):
    o_ref[...] = jnp.dot(_rms(x_ref[...], g_ref[...]).astype(BF16), w_ref[...], preferred_element_type=F32)


def _norm_matmul(x, g, w, l, tm):
    rows = x.shape[0]
    return pl.pallas_call(
        _norm_matmul_kernel,
        grid=(rows // tm,),
        in_specs=[pl.BlockSpec((tm, x.shape[1]), lambda i: (i, 0)), _layer(g, l), _layer(w, l)],
        out_specs=pl.BlockSpec((tm, w.shape[2]), lambda i: (i, 0)),
        out_shape=jax.ShapeDtypeStruct((rows, w.shape[2]), F32),
        compiler_params=_params(1),
        name="norm_matmul",
    )(x, g, w)


def _rms_kernel(x_ref, g_ref, o_ref):
    o_ref[...] = _rms(x_ref[...], g_ref[...])


def _rmsnorm_call(x, g, tm):
    rows = x.shape[0]
    return pl.pallas_call(
        _rms_kernel,
        grid=(rows // tm,),
        in_specs=[pl.BlockSpec((tm, x.shape[1]), lambda i: (i, 0)), _full(g.shape)],
        out_specs=pl.BlockSpec((tm, x.shape[1]), lambda i: (i, 0)),
        out_shape=jax.ShapeDtypeStruct(x.shape, F32),
        compiler_params=_params(1),
        name="final_norm",
    )(x, g)


def _xattn_heads(q, kv_ref, kv_rows):
    scale = XA_HEAD_DIM ** -0.5
    n_mem = kv_ref.shape[0] // (2 * XA_HEADS) if kv_rows else kv_ref.shape[0]
    outs = []
    for h in range(XA_HEADS):
        hs = slice(h * XA_HEAD_DIM, (h + 1) * XA_HEAD_DIM)
        if kv_rows:
            k_h = kv_ref[h * n_mem:(h + 1) * n_mem, :]
            v_h = kv_ref[(XA_HEADS + h) * n_mem:(XA_HEADS + h + 1) * n_mem, :]
        else:
            k_h = kv_ref[:, hs]
            v_h = kv_ref[:, D_MODEL + h * XA_HEAD_DIM:D_MODEL + (h + 1) * XA_HEAD_DIM]
        logits = _dot_nt(q[:, hs], k_h) * scale
        m = logits.max(axis=1, keepdims=True)
        p = jnp.exp(logits - m)
        outs.append(_dot(p / p.sum(axis=1, keepdims=True), v_h))
    return jnp.concatenate(outs, axis=1)


def _xattn_prompt_kernel(x_ref, kv_ref, g_ref, wq_ref, wo_ref, out_ref):
    for rs in _sub_tiles(x_ref.shape[0]):
        x = x_ref[rs, :]
        q = jnp.dot(_rms(x, g_ref[...]).astype(BF16), wq_ref[...], preferred_element_type=F32)
        out_ref[rs, :] = x + _dot(_xattn_heads(q, kv_ref, False), wo_ref[...])


def _xattn_prompt_call(x, kv, wts, l, *, n_tiles, tiles_per_seq):
    ta = x.shape[0]
    tm = TOKEN_TILE
    row = pl.BlockSpec((tm, D_MODEL), lambda i: (i, 0))
    ws = tuple(wts[k] for k in ("norm_xa", "xa_wq", "xa_wo"))
    return pl.pallas_call(
        _xattn_prompt_kernel,
        grid=(n_tiles,),
        in_specs=[row, pl.BlockSpec((None,) + kv.shape[1:], lambda i: (i // tiles_per_seq, 0, 0))]
        + [_layer(w, l) for w in ws],
        out_specs=row,
        out_shape=jax.ShapeDtypeStruct((ta, D_MODEL), F32),
        compiler_params=_params(1),
        name="xattn_prompt",
    )(x, kv, *ws)


def _xattn_sample_kernel(x_ref, kv_ref, g_ref, wq_ref, wo_ref, out_ref, q_sc, o_sc):
    b = pl.program_id(0)

    @pl.when(b == 0)
    def _():
        q_sc[...] = jnp.dot(_rms(x_ref[...], g_ref[...]).astype(BF16), wq_ref[...], preferred_element_type=F32)

    rows = pl.ds(pl.multiple_of(b * SAMPLE_ROWS, SAMPLE_ROWS), SAMPLE_ROWS)
    o_sc[rows, :] = _xattn_heads(q_sc[rows, :], kv_ref, True)

    @pl.when(b == pl.num_programs(0) - 1)
    def _():
        out_ref[...] = x_ref[...] + _dot(o_sc[...], wo_ref[...])


def _xattn_sample_call(x, kv, wts, l, *, nb, row0, prev):
    ta = x.shape[0]
    n_s = nb * SAMPLE_ROWS
    rows = pl.BlockSpec((n_s, D_MODEL), lambda b: (row0 // n_s, 0))
    ws = tuple(wts[k] for k in ("norm_xa", "xa_wq", "xa_wo"))
    in_specs = [rows, pl.BlockSpec((None, None) + kv.shape[2:], lambda b: (l, b, 0, 0))] + [_layer(w, l) for w in ws]
    args = [x, kv, *ws]
    kern, aliases = _with_aliased(_xattn_sample_kernel, in_specs, args, [(prev, 0)])
    return pl.pallas_call(
        kern,
        grid=(nb,),
        in_specs=in_specs,
        out_specs=rows,
        out_shape=jax.ShapeDtypeStruct((ta, D_MODEL), F32),
        scratch_shapes=[pltpu.VMEM((n_s, D_MODEL), F32), pltpu.VMEM((n_s, D_MODEL), F32)],
        input_output_aliases=aliases,
        compiler_params=_params(1),
        name="xattn_sample",
    )(*args)


def _ffn_kernel(x_ref, g_ref, w1_ref, w2_ref, out_ref):
    for rs in _sub_tiles(x_ref.shape[0]):
        x = x_ref[rs, :]
        a = jnp.dot(_rms(x, g_ref[...]).astype(BF16), w1_ref[...], preferred_element_type=F32)
        a = jnp.square(jnp.maximum(a, 0.0))
        out_ref[rs, :] = x + _dot(a, w2_ref[...])


def _ffn_call(x, wts, l):
    ta = x.shape[0]
    tm = TOKEN_TILE
    row = pl.BlockSpec((tm, D_MODEL), lambda i: (i, 0))
    ws = tuple(wts[k] for k in ("norm_mlp", "w_ff1", "w_ff2"))
    return pl.pallas_call(
        _ffn_kernel,
        grid=(ta // tm,),
        in_specs=[row] + [_layer(w, l) for w in ws],
        out_specs=row,
        out_shape=jax.ShapeDtypeStruct(x.shape, F32),
        compiler_params=_params(1),
        name="ffn",
    )(x, *ws)


def _prep_weights(w):
    o = IN_OFFS
    w_in = w["w_in"]
    small = jnp.concatenate([w_in[:, :, o[2]:o[3]], w_in[:, :, o[11]:o[13]]], axis=2)
    row = lambda v: v[:, None, :].astype(F32)
    col = lambda v: v[:, :, None].astype(F32)
    bf = lambda v: v.astype(BF16)
    return {
        "norm_mix": row(w["norm_mix"]),
        "w_main": jnp.concatenate([w_in[:, :, o[0]:o[2]], w_in[:, :, o[3]:o[11]]], axis=2).astype(BF16),
        "w_gate": bf(w_in[:, :, o[13]:o[14]]),
        "w_small": jnp.pad(small, ((0, 0), (0, 0), (0, SMALL_COLS - small.shape[2]))).astype(BF16),
        "w_small_t": jnp.transpose(small, (0, 2, 1)).astype(BF16),
        "ssd_conv_w": w["ssd_conv_w"], "ssd_conv_b": row(w["ssd_conv_b"]),
        "dtb_c": row(w["ssd_dt_bias"]), "dtb_r": col(w["ssd_dt_bias"]),
        "alog_c": row(w["ssd_a_log"]), "alog_r": col(w["ssd_a_log"]),
        "ssd_dskip": row(jnp.repeat(w["ssd_d"], SSD_HEAD_DIM, axis=1)), "ssd_norm": row(w["ssd_norm"]),
        "ib_c": row(w["mlstm_i_bias"]), "ib_r": col(w["mlstm_i_bias"]),
        "fb_c": row(w["mlstm_f_bias"]), "fb_r": col(w["mlstm_f_bias"]),
        "mlstm_norm": row(w["mlstm_norm"]),
        "s5_d": row(w["s5_d"]), "s5_w_glu": bf(w["s5_w_glu"]),
        "w_br_ssd": bf(w["w_br_ssd"]), "w_br_swa": bf(w["w_br_swa"]), "w_br_s5": bf(w["w_br_s5"]),
        "w_br_mlstm": bf(w["w_br_mlstm"]), "w_out": bf(w["w_out"]),
        "norm_xa": row(w["norm_xa"]), "xa_wq": bf(w["xa_wq"]), "xa_wo": bf(w["xa_wo"]),
        "norm_mem": row(w["norm_mem"]),
        "xa_wkv": jnp.concatenate([w["xa_wk"], w["xa_wv"]], axis=2).astype(BF16),
        "norm_mlp": row(w["norm_mlp"]), "w_ff1": bf(w["w_ff1"]), "w_ff2": bf(w["w_ff2"]),
    }


def kernel(x_prompt, x_sample, state_ssd, state_ssd_conv, cache_swa_w128, cache_swa_w512, cache_swa_w2048, state_s5, state_mlstm_c, state_mlstm_n, state_mlstm_m, cache_mem_kv, mem_prompt, norm_mix, w_in, ssd_conv_w, ssd_conv_b, ssd_dt_bias, ssd_a_log, ssd_d, ssd_norm, rel_bias, s5_a_re, s5_a_im, s5_log_dt, s5_b_re, s5_b_im, s5_c_re, s5_c_im, s5_d, s5_w_glu, mlstm_i_bias, mlstm_f_bias, mlstm_norm, w_br_ssd, w_br_swa, w_br_s5, w_br_mlstm, w_out, norm_xa, norm_mem, xa_wq, xa_wk, xa_wv, xa_wo, norm_mlp, w_ff1, w_ff2, norm_final):
    wts = _prep_weights(dict(
        norm_mix=norm_mix, w_in=w_in, ssd_conv_w=ssd_conv_w, ssd_conv_b=ssd_conv_b, ssd_dt_bias=ssd_dt_bias,
        ssd_a_log=ssd_a_log, ssd_d=ssd_d, ssd_norm=ssd_norm, s5_d=s5_d, s5_w_glu=s5_w_glu,
        mlstm_i_bias=mlstm_i_bias, mlstm_f_bias=mlstm_f_bias, mlstm_norm=mlstm_norm, w_br_ssd=w_br_ssd,
        w_br_swa=w_br_swa, w_br_s5=w_br_s5, w_br_mlstm=w_br_mlstm, w_out=w_out, norm_xa=norm_xa, norm_mem=norm_mem,
        xa_wq=xa_wq, xa_wk=xa_wk, xa_wv=xa_wv, xa_wo=xa_wo, norm_mlp=norm_mlp, w_ff1=w_ff1, w_ff2=w_ff2))
    nb_p, seq, _ = x_prompt.shape
    nb_s, t_new, _ = x_sample.shape
    depth = w_in.shape[0]
    n_mem = mem_prompt.shape[1]
    rows_s = SAMPLE_ROWS
    n_p = nb_p * seq
    n_s = nb_s * rows_s
    ta = n_p + n_s
    assert seq % (SWA_BLOCK * SWA_PATTERN[-1][1]) == 0 and ta % TOKEN_TILE == 0 and t_new <= rows_s
    assert n_p % n_s == 0 and (seq // S5_CHUNK) & (seq // S5_CHUNK - 1) == 0
    caches = (cache_swa_w128, cache_swa_w512, cache_swa_w2048)

    xs_pad = jnp.pad(x_sample, ((0, 0), (0, rows_s - t_new), (0, 0))).reshape(n_s, D_MODEL)
    x = jnp.concatenate([x_prompt.reshape(n_p, D_MODEL), xs_pad], axis=0)

    steps = [_bias_steps(rel_bias, g, dil) for g, (_, dil) in enumerate(SWA_PATTERN)]
    prompt_bias = [_swa_prompt_bias(s) for s in steps]
    sample_bias = [_swa_sample_bias(steps[g], win, dil, caches[g].shape[2], t_new)
                   for g, (win, dil) in enumerate(SWA_PATTERN)]
    caches_t = [jnp.transpose(c, (0, 1, 3, 4, 5, 2)).reshape(depth, nb_s, 2 * SWA_GROUP_WIDTH, c.shape[2])
                for c in caches]
    mem_kv_s = jnp.transpose(cache_mem_kv, (0, 1, 3, 4, 2, 5)).reshape(depth, nb_s, 2 * XA_HEADS * n_mem, XA_HEAD_DIM)
    s5_tab = _s5_tables(s5_a_re, s5_a_im, s5_log_dt, s5_b_re, s5_b_im, s5_c_re, s5_c_im, t_new)
    s5_h0_s = jnp.transpose(state_s5.reshape(depth, nb_s, S5_SUPER, S5_SUPER_GROUPS, S5_STATE, 2),
                            (0, 2, 1, 5, 3, 4)).reshape(depth, S5_SUPER, 1, nb_s, S5_SUPER_STATE)

    zeros = lambda *s: jnp.zeros(s, F32)
    zero_conv = zeros(1, nb_p, 8, SSD_CONV_DIM)
    zero_ssd = zeros(1, nb_p, SSD_HEADS, SSD_HEAD_DIM, SSD_STATE)
    zero_c = zeros(1, nb_p, ML_HEADS, ML_HEAD_DIM, ML_HEAD_DIM)
    zero_vec = zeros(1, nb_p, ML_HEADS, 1, ML_HEAD_DIM)
    zero_s5 = zeros(S5_SUPER, nb_p, 1, S5_SUPER_STATE)
    conv_init_s = jnp.pad(state_ssd_conv, ((0, 0), (0, 0), (8 - (SSD_CONV - 1), 0), (0, 0)))
    n_init_s = state_mlstm_n[:, :, :, None, :]
    m_init_s = jnp.broadcast_to(state_mlstm_m[:, :, :, None, None], (depth, nb_s, ML_HEADS, 1, ML_HEAD_DIM))

    outs = {k: [] for k in ("ssd_p", "ssd_s", "conv_p", "conv_s", "s5_p", "s5_s", "c_p", "c_s", "n_p", "n_s",
                            "m_p", "m_s", "kv_p")}
    kvt_p = [None] * len(SWA_PATTERN)
    cache_out = [None] * len(SWA_PATTERN)
    ssd_q, ml_q, nsub = 128, 128, 4
    nchunk = seq // S5_CHUNK

    for l in range(depth):
        p_ssd, p_swa, p_s5, p_ml, p_small, p_small_t = _inproj(x, wts, l)
        small_t_s = jnp.transpose(p_small[n_p:, :SMALL_ROWS].reshape(nb_s, rows_s, SMALL_ROWS), (0, 2, 1))

        y_ssd, st_p = _ssd_call(p_ssd, p_small, p_small_t, zero_conv, zero_ssd, 0, wts, l, nb=nb_p,
                                nc=seq // (ssd_q * nsub), q=ssd_q, nsub=nsub, valid=ssd_q, row0=0, prev=None)
        y_ssd, st_s = _ssd_call(p_ssd, p_small, small_t_s, conv_init_s, state_ssd, l, wts, l,
                                nb=nb_s, nc=1, q=rows_s, nsub=1, valid=t_new, row0=n_p, prev=y_ssd)
        outs["ssd_p"].append(st_p)
        outs["ssd_s"].append(st_s)
        tail = SSD_CONV - 1
        xbc_s = p_ssd[n_p:, SSD_D_INNER:].reshape(nb_s, rows_s, SSD_CONV_DIM)[:, :t_new]
        outs["conv_p"].append(jnp.stack([p_ssd[(b + 1) * seq - tail:(b + 1) * seq, SSD_D_INNER:]
                                         for b in range(nb_p)]))
        outs["conv_s"].append(jnp.concatenate([state_ssd_conv[l], xbc_s], axis=1)[:, -(SSD_CONV - 1):])

        swa_acts = []
        for g, (win, dil) in enumerate(SWA_PATTERN):
            o_g, lse_g, kvt_p[g] = _swa_prompt_call(p_swa, prompt_bias[g], kvt_p[g], l, depth, grp=g, dil=dil,
                                                    nb=nb_p, seq=seq)
            o_g, lse_g, cache_out[g] = _swa_sample_call(p_swa, caches_t[g], sample_bias[g][0], sample_bias[g][1],
                                                        o_g, lse_g, cache_out[g], l, grp=g, nb=nb_s, row0=n_p,
                                                        t_new=t_new)
            swa_acts += [o_g, lse_g]

        y_s5, h_p = _s5_call(p_s5, zero_s5, s5_tab, l, nseq=nb_p, rows=nchunk, scan=True, row0=0, prev=None)
        y_s5, h_s = _s5_call(p_s5, s5_h0_s[l], s5_tab, l, nseq=1, rows=nb_s, scan=False, row0=n_p, prev=y_s5)
        outs["s5_p"].append(h_p)
        outs["s5_s"].append(h_s)

        y_ml, c_p, nn_p, m_p = _mlstm_call(p_ml, p_small, p_small_t, zero_c, zero_vec, zero_vec, 0, wts, l, nb=nb_p,
                                           nc=seq // (ml_q * nsub), q=ml_q, nsub=nsub, valid=ml_q, row0=0,
                                           prev=None)
        y_ml, c_s, nn_s, m_s = _mlstm_call(p_ml, p_small, small_t_s, state_mlstm_c, n_init_s, m_init_s, l, wts, l,
                                           nb=nb_s, nc=1, q=rows_s, nsub=1, valid=t_new, row0=n_p, prev=y_ml)
        for key, val in (("c_p", c_p), ("c_s", c_s), ("n_p", nn_p[:, :, 0]), ("n_s", nn_s[:, :, 0]),
                         ("m_p", m_p[:, :, 0, 0]), ("m_s", m_s[:, :, 0, 0])):
            outs[key].append(val)

        x = _merge_call(x, (y_ssd, *swa_acts, y_s5, p_s5, y_ml), wts, l)

        kv_p = _norm_matmul(mem_prompt.reshape(nb_p * n_mem, D_MODEL), wts["norm_mem"], wts["xa_wkv"], l, n_mem)
        outs["kv_p"].append(kv_p.reshape(nb_p, n_mem, 2, XA_HEADS, XA_HEAD_DIM))
        x_new = _xattn_prompt_call(x, kv_p.reshape(nb_p, n_mem, 2 * D_MODEL), wts, l, n_tiles=n_p // TOKEN_TILE,
                                   tiles_per_seq=seq // TOKEN_TILE)
        x = _xattn_sample_call(x, mem_kv_s, wts, l, nb=nb_s, row0=n_p, prev=x_new)

        x = _ffn_call(x, wts, l)

    y = _rmsnorm_call(x, norm_final[None, :], TOKEN_TILE)
    y_prompt = y[:n_p].reshape(nb_p, seq, D_MODEL)
    y_sample = y[n_p:].reshape(nb_s, rows_s, D_MODEL)[:, :t_new]
    st = lambda k: jnp.stack(outs[k])

    def s5_state(k, nb):
        h = st(k).reshape(depth, S5_SUPER, nb, 2, S5_SUPER_GROUPS, S5_STATE)
        return jnp.transpose(h, (0, 2, 1, 4, 5, 3)).reshape(depth, nb, S5_GROUPS, S5_STATE, 2)

    swa_out = []
    for g in range(len(SWA_PATTERN)):
        width_p = kvt_p[g].shape[-1]
        kp = kvt_p[g].reshape(depth, nb_p, 2, SWA_GROUP_HEADS, SWA_HEAD_DIM, width_p)
        swa_out.append(jnp.transpose(kp, (0, 1, 5, 2, 3, 4)))
        cs = cache_out[g].reshape(depth, nb_s, 2, SWA_GROUP_HEADS, SWA_HEAD_DIM, caches[g].shape[2])
        swa_out.append(jnp.transpose(cs, (0, 1, 5, 2, 3, 4)))
    return (y_prompt, y_sample, st("ssd_p"), st("ssd_s"), st("conv_p"), st("conv_s"), *swa_out,
            s5_state("s5_p", nb_p), s5_state("s5_s", nb_s), st("c_p"), st("c_s"),
            st("n_p"), st("n_s"), st("m_p"), st("m_s"), st("kv_p"))
```

```python
import functools
import math

import numpy as np
import jax
import jax.numpy as jnp
from jax import lax
from jax.experimental import pallas as pl
from jax.experimental.pallas import tpu as pltpu

F32 = jnp.float32
BF16 = jnp.bfloat16

D_MODEL = 1024
RMS_EPS = 1e-6
N_BRANCH = 4

SSD_D_INNER = 512
SSD_HEAD_DIM = 64
SSD_HEADS = 8
SSD_GROUPS = 2
SSD_STATE = 64
SSD_CONV = 4
SSD_CONV_DIM = SSD_D_INNER + 2 * SSD_GROUPS * SSD_STATE
SSD_SEG = SSD_D_INNER + SSD_CONV_DIM

SWA_PATTERN = ((128, 1), (512, 4), (2048, 16))
SWA_GROUP_HEADS = 4
SWA_HEAD_DIM = 64
SWA_GROUP_WIDTH = SWA_GROUP_HEADS * SWA_HEAD_DIM
SWA_WIDTH = 3 * SWA_GROUP_WIDTH
SWA_BLOCK = 128
REL_BUCKETS = 32
REL_MAX_EXACT = 16
REL_MAX_DIST = 2048

S5_WIDTH = 512
S5_GROUP = 16
S5_GROUPS = 32
S5_STATE = 64
S5_CHUNK = 16
S5_LANES = 128
S5_SUPER = S5_WIDTH // S5_LANES
S5_SUPER_GROUPS = S5_GROUPS // S5_SUPER
S5_SUPER_STATE = 2 * S5_SUPER_GROUPS * S5_STATE
S5_DOUBLINGS = 12

ML_WIDTH = 512
ML_HEADS = 4
ML_HEAD_DIM = 128

XA_HEADS = 4
XA_HEAD_DIM = 256
D_FF = 4096

MAIN_SSD = (0, SSD_SEG)
MAIN_SWA = (SSD_SEG, SSD_SEG + 3 * SWA_WIDTH)
MAIN_S5 = (MAIN_SWA[1], MAIN_SWA[1] + S5_WIDTH)
MAIN_ML = (MAIN_S5[1], MAIN_S5[1] + 4 * ML_WIDTH)
MAIN_COLS = MAIN_ML[1]
SMALL_COLS = 128
SMALL_ROWS = 16

SAMPLE_ROWS = 16
TOKEN_TILE = 512
SUB_TILE = 256
MASKED = -1e30

VMEM_LIMIT = 56 * 1024 * 1024

IN_SIZES = (SSD_D_INNER, SSD_CONV_DIM, SSD_HEADS, SWA_WIDTH, SWA_WIDTH, SWA_WIDTH, S5_WIDTH,
            ML_WIDTH, ML_WIDTH, ML_WIDTH, ML_WIDTH, ML_HEADS, ML_HEADS, N_BRANCH * D_MODEL)
IN_OFFS = tuple(int(v) for v in np.concatenate([[0], np.cumsum(IN_SIZES)]))


def _params(n_axes):
    return pltpu.CompilerParams(dimension_semantics=("arbitrary",) * n_axes, vmem_limit_bytes=VMEM_LIMIT)


def _full(shape):
    nd = len(shape)
    return pl.BlockSpec(shape, lambda *_: (0,) * nd)


def _layer(a, l):
    nd = a.ndim
    return pl.BlockSpec((None,) + a.shape[1:], lambda *_: (l,) + (0,) * (nd - 1), pipeline_mode=pl.Buffered(1))


def _sub_tiles(rows):
    step = min(SUB_TILE, rows)
    return [slice(r, r + step) for r in range(0, rows, step)]


def _with_aliased(kern, in_specs, args, aliased):
    n = len(args)
    aliases = {}
    for arr, out_idx in aliased:
        in_specs.append(pl.BlockSpec(memory_space=pl.ANY))
        aliases[len(args)] = out_idx
        args.append(arr)
    k = len(aliased)
    if k == 0:
        return kern, aliases

    def wrapped(*refs):
        return kern(*refs[:n], *refs[n + k:])
    return wrapped, aliases


def _dot(a, b):
    return jnp.dot(a.astype(BF16), b.astype(BF16), preferred_element_type=F32)


def _dot_nt(a, b):
    return lax.dot_general(a.astype(BF16), b.astype(BF16), (((1,), (1,)), ((), ())), preferred_element_type=F32)


def _dot_tn(a, b):
    return lax.dot_general(a.astype(BF16), b.astype(BF16), (((0,), (0,)), ((), ())), preferred_element_type=F32)


def _split3(x):
    hi = x.astype(BF16)
    r = x - hi.astype(F32)
    mid = r.astype(BF16)
    lo = (r - mid.astype(F32)).astype(BF16)
    return hi, mid, lo


def _dot_exact_l(ones_bf16, x):
    hi, mid, lo = _split3(x)
    f = lambda p: jnp.dot(ones_bf16, p, preferred_element_type=F32)
    return f(hi) + f(mid) + f(lo)


def _dot_exact_r(x, ones_bf16):
    hi, mid, lo = _split3(x)
    f = lambda p: jnp.dot(p, ones_bf16, preferred_element_type=F32)
    return f(hi) + f(mid) + f(lo)


def _rms(x, g):
    return x * lax.rsqrt(jnp.mean(x * x, axis=-1, keepdims=True) + RMS_EPS) * g


def _sigmoid(x):
    return 1.0 / (1.0 + jnp.exp(-x))


def _softplus(x):
    return jnp.maximum(x, 0.0) + jnp.log(1.0 + jnp.exp(-jnp.abs(x)))


def _tri(q, lower):
    r = lax.broadcasted_iota(jnp.int32, (q, q), 0)
    c = lax.broadcasted_iota(jnp.int32, (q, q), 1)
    return (r >= c) if lower else (r <= c)


def _inproj_kernel(x_ref, g_ref, wm_ref, ws_ref, wst_ref, oa_ref, ob_ref, oc_ref, od_ref, os_ref, ost_ref):
    for rs in _sub_tiles(x_ref.shape[0]):
        h = _rms(x_ref[rs, :], g_ref[...]).astype(BF16)
        for o_ref, (lo, hi) in ((oa_ref, MAIN_SSD), (ob_ref, MAIN_SWA), (oc_ref, MAIN_S5), (od_ref, MAIN_ML)):
            o_ref[rs, :] = jnp.dot(h, wm_ref[:, lo:hi], preferred_element_type=F32)
        os_ref[rs, :] = jnp.dot(h, ws_ref[...], preferred_element_type=F32)
        ost_ref[0, :, rs] = lax.dot_general(wst_ref[...], h, (((1,), (1,)), ((), ())), preferred_element_type=F32)


def _inproj(x, wts, l):
    ta = x.shape[0]
    tm = TOKEN_TILE
    widths = [hi - lo for lo, hi in (MAIN_SSD, MAIN_SWA, MAIN_S5, MAIN_ML)] + [SMALL_COLS]
    row = lambda n: pl.BlockSpec((tm, n), lambda i: (i, 0))
    ws = (wts["norm_mix"], wts["w_main"], wts["w_small"], wts["w_small_t"])
    return pl.pallas_call(
        _inproj_kernel,
        grid=(ta // tm,),
        in_specs=[row(D_MODEL)] + [_layer(w, l) for w in ws],
        out_specs=[row(n) for n in widths] + [pl.BlockSpec((1, SMALL_ROWS, tm), lambda i: (0, 0, i))],
        out_shape=[jax.ShapeDtypeStruct((ta, n), F32) for n in widths]
        + [jax.ShapeDtypeStruct((1, SMALL_ROWS, ta), F32)],
        compiler_params=_params(1),
        name="inproj",
    )(x, *ws)


def _ssd_kernel(p_ref, sm_ref, smt_ref, cinit_ref, sinit_ref, cw_ref, cb_ref, dtb_c_ref, dtb_r_ref,
                alog_c_ref, alog_r_ref, dskip_ref, g_ref, y_ref, sout_ref, ext_sc, st_sc, *, q, nsub, valid):
    c = pl.program_id(1)
    rows = q * nsub

    @pl.when(c == 0)
    def _():
        ext_sc[0:8, :] = cinit_ref[...]
        st_sc[...] = sinit_ref[...]

    ext_sc[8:8 + rows, :] = p_ref[:, SSD_D_INNER:SSD_SEG]
    conv = cb_ref[...] + cw_ref[0:1, :] * ext_sc[5:5 + rows, :]
    for j in range(1, SSD_CONV):
        conv = conv + cw_ref[j:j + 1, :] * ext_sc[5 + j:5 + j + rows, :]
    ext_sc[0:8, :] = ext_sc[rows:rows + 8, :]
    xbc_all = conv * _sigmoid(conv)
    z_all = p_ref[:, :SSD_D_INNER]

    dt_c_all = _softplus(sm_ref[:, 0:SSD_HEADS] + dtb_c_ref[...])
    dt_r_all = _softplus(smt_ref[0, 0:SSD_HEADS, :] + dtb_r_ref[...])
    if valid < q:
        dt_c_all = jnp.where(lax.broadcasted_iota(jnp.int32, dt_c_all.shape, 0) < valid, dt_c_all, 0.0)
        dt_r_all = jnp.where(lax.broadcasted_iota(jnp.int32, dt_r_all.shape, 1) < valid, dt_r_all, 0.0)
    da_c_all = dt_c_all * (-jnp.exp(alog_c_ref[...]))
    da_r_all = dt_r_all * (-jnp.exp(alog_r_ref[...]))
    causal = _tri(q, True)
    lower = jnp.where(causal, 1.0, 0.0).astype(BF16)
    upper = jnp.where(_tri(q, False), 1.0, 0.0).astype(BF16)

    rep = SSD_HEADS // SSD_GROUPS
    c_off = SSD_D_INNER + SSD_GROUPS * SSD_STATE
    states = [st_sc[h] for h in range(SSD_HEADS)]
    for j in range(nsub):
        rs = slice(j * q, (j + 1) * q)
        xbc, dt_c, dt_r = xbc_all[rs], dt_c_all[rs], dt_r_all[:, rs]
        xs = xbc[:, :SSD_D_INNER]
        cum_c = _dot_exact_l(lower, da_c_all[rs])
        cum_r = _dot_exact_r(da_r_all[:, rs], upper)
        ys = []
        for h in range(SSD_HEADS):
            grp = h // rep
            b_g = xbc[:, SSD_D_INNER + grp * SSD_STATE:SSD_D_INNER + (grp + 1) * SSD_STATE]
            c_g = xbc[:, c_off + grp * SSD_STATE:c_off + (grp + 1) * SSD_STATE]
            cb = _dot_nt(c_g, b_g)
            x_h = xs[:, h * SSD_HEAD_DIM:(h + 1) * SSD_HEAD_DIM]
            cc = cum_c[:, h:h + 1]
            seg = jnp.where(causal, cc - cum_r[h:h + 1, :], MASKED)
            w = cb * jnp.exp(seg) * dt_r[h:h + 1, :]
            s_h = states[h]
            ys.append(_dot(w, x_h) + jnp.exp(cc) * _dot_nt(c_g, s_h))
            last = cum_c[q - 1:q, h:h + 1]
            tail = jnp.exp(last - cc) * dt_c[:, h:h + 1]
            states[h] = s_h * jnp.exp(last) + _dot_tn(x_h * tail, b_g)
        z = z_all[rs]
        y = (jnp.concatenate(ys, axis=1) + dskip_ref[...] * xs) * (z * _sigmoid(z))
        y_ref[rs, :] = _rms(y, g_ref[...])
    for h in range(SSD_HEADS):
        st_sc[h] = states[h]

    @pl.when(c == pl.num_programs(1) - 1)
    def _():
        sout_ref[0] = st_sc[...]


def _seq_specs(rows_blk, nc, blk0, p_small_t):
    rows = lambda n: pl.BlockSpec((rows_blk, n), lambda b, c: (blk0 + b * nc + c, 0))
    if p_small_t.shape[0] == 1:
        smt_spec = pl.BlockSpec((1, SMALL_ROWS, rows_blk), lambda b, c: (0, 0, blk0 + b * nc + c))
    else:
        smt_spec = pl.BlockSpec((1, SMALL_ROWS, rows_blk), lambda b, c: (b, 0, 0))
    return rows, smt_spec


def _state_spec(a, sl):
    nd = a.ndim
    return pl.BlockSpec((None, None) + a.shape[2:], lambda b, c: (sl, b) + (0,) * (nd - 2))


def _ssd_call(p_ssd, p_small, p_small_t, conv_init, state_init, sl, wts, l, *, nb, nc, q, nsub, valid, row0, prev):
    ta = p_ssd.shape[0]
    blk = q * nsub
    assert row0 % blk == 0 and (valid == q or nsub == 1)
    rows, smt_spec = _seq_specs(blk, nc, row0 // blk, p_small_t)
    ws = tuple(wts[k] for k in ("ssd_conv_w", "ssd_conv_b", "dtb_c", "dtb_r", "alog_c", "alog_r", "ssd_dskip",
                                "ssd_norm"))
    in_specs = [rows(SSD_SEG), rows(SMALL_COLS), smt_spec, _state_spec(conv_init, sl), _state_spec(state_init, sl)]
    in_specs += [_layer(w, l) for w in ws]
    args = [p_ssd, p_small, p_small_t, conv_init, state_init, *ws]
    kern, aliases = _with_aliased(functools.partial(_ssd_kernel, q=q, nsub=nsub, valid=valid), in_specs, args,
                                  [] if prev is None else [(prev, 0)])
    return pl.pallas_call(
        kern,
        grid=(nb, nc),
        in_specs=in_specs,
        out_specs=[rows(SSD_D_INNER),
                   pl.BlockSpec((1, SSD_HEADS, SSD_HEAD_DIM, SSD_STATE), lambda b, c: (b, 0, 0, 0))],
        out_shape=[jax.ShapeDtypeStruct((ta, SSD_D_INNER), F32),
                   jax.ShapeDtypeStruct((nb, SSD_HEADS, SSD_HEAD_DIM, SSD_STATE), F32)],
        scratch_shapes=[pltpu.VMEM((blk + 8, SSD_CONV_DIM), F32),
                        pltpu.VMEM((SSD_HEADS, SSD_HEAD_DIM, SSD_STATE), F32)],
        input_output_aliases=aliases,
        compiler_params=_params(2),
        name="ssd_q%d" % q,
    )(*args)


def _mlstm_kernel(p_ref, sm_ref, smt_ref, cinit_ref, ninit_ref, minit_ref, ib_c_ref, ib_r_ref, fb_c_ref, fb_r_ref,
                  g_ref, y_ref, cout_ref, nout_ref, mout_ref, c_sc, n_sc, m_sc, *, q, nsub, valid):
    c = pl.program_id(1)

    @pl.when(c == 0)
    def _():
        c_sc[...] = cinit_ref[...]
        n_sc[...] = ninit_ref[...]
        m_sc[...] = minit_ref[...]

    w = ML_WIDTH
    i0 = SSD_HEADS
    f0 = SSD_HEADS + ML_HEADS
    ig_c_all = sm_ref[:, i0:i0 + ML_HEADS] + ib_c_ref[...]
    fg_c = sm_ref[:, f0:f0 + ML_HEADS] + fb_c_ref[...]
    ig_r_all = smt_ref[0, i0:i0 + ML_HEADS, :] + ib_r_ref[...]
    fg_r = smt_ref[0, f0:f0 + ML_HEADS, :] + fb_r_ref[...]
    lf_c_all = -_softplus(-fg_c)
    lf_r_all = -_softplus(-fg_r)
    if valid < q:
        ok_c = lax.broadcasted_iota(jnp.int32, ig_c_all.shape, 0) < valid
        ok_r = lax.broadcasted_iota(jnp.int32, ig_r_all.shape, 1) < valid
        ig_c_all = jnp.where(ok_c, ig_c_all, MASKED)
        ig_r_all = jnp.where(ok_r, ig_r_all, MASKED)
        lf_c_all = jnp.where(ok_c, lf_c_all, 0.0)
        lf_r_all = jnp.where(ok_r, lf_r_all, 0.0)
    causal = _tri(q, True)
    lower = jnp.where(causal, 1.0, 0.0).astype(BF16)
    upper = jnp.where(_tri(q, False), 1.0, 0.0).astype(BF16)

    scale = ML_HEAD_DIM ** -0.5
    c_st = [c_sc[h] for h in range(ML_HEADS)]
    n_st = [n_sc[h] for h in range(ML_HEADS)]
    m_st = [m_sc[h][:, 0:1] for h in range(ML_HEADS)]
    heads = range(ML_HEADS)
    col = lambda ref_rows, base, h: p_ref[ref_rows, base + h * ML_HEAD_DIM:base + (h + 1) * ML_HEAD_DIM]
    pre = []
    for j in range(nsub):
        rs = slice(j * q, (j + 1) * q)
        b_c = _dot_exact_l(lower, lf_c_all[rs])
        b_r = _dot_exact_r(lf_r_all[:, rs], upper)
        ig_r = ig_r_all[:, rs]
        per_head = []
        for h in heads:
            bc = b_c[:, h:h + 1]
            intra = jnp.where(causal, bc - b_r[h:h + 1, :] + ig_r[h:h + 1, :], MASKED)
            qk = _dot_nt(col(rs, 0, h), col(rs, w, h) * scale)
            per_head.append((bc, intra, jnp.max(intra, axis=1, keepdims=True), qk))
        pre.append((rs, b_c, per_head))
    for j, (rs, b_c, per_head) in enumerate(pre):
        ig_c = ig_c_all[rs]
        mid = []
        for h in heads:
            bc, intra, row_max, qk = per_head[h]
            q_h, v_h = col(rs, 0, h), col(rs, 2 * w, h)
            inter = bc + m_st[h]
            m_t = jnp.maximum(inter, row_max)
            wgt = qk * jnp.exp(intra - m_t)
            w_inter = jnp.exp(inter - m_t)
            num = _dot(wgt, v_h) + w_inter * _dot(q_h, c_st[h])
            den = jnp.sum(wgt, axis=1, keepdims=True) + w_inter * jnp.sum(q_h * n_st[h], axis=1, keepdims=True)
            mid.append((m_t, num, den))
        last = []
        for h in heads:
            m_t, num, den = mid[h]
            hh = num / jnp.maximum(jnp.abs(den), jnp.exp(-m_t))
            last.append((hh, jnp.mean(hh * hh, axis=-1, keepdims=True)))
        ys = []
        for h in heads:
            hs = slice(h * ML_HEAD_DIM, (h + 1) * ML_HEAD_DIM)
            bc, m_t = per_head[h][0], mid[h][0]
            hh, ms = last[h]
            ys.append(_sigmoid(col(rs, 3 * w, h)) * (hh * lax.rsqrt(ms + RMS_EPS) * g_ref[:, hs]))
            m_new = m_t[q - 1:q, :]
            b_last = b_c[q - 1:q, h:h + 1]
            wk = jnp.exp(b_last - bc + ig_c[:, h:h + 1] - m_new)
            decay = jnp.exp(b_last + m_st[h] - m_new)
            kw = col(rs, w, h) * scale * wk
            c_st[h] = decay * c_st[h] + _dot_tn(kw, col(rs, 2 * w, h))
            n_st[h] = decay * n_st[h] + jnp.sum(kw, axis=0, keepdims=True)
            m_st[h] = m_new
        y_ref[rs, :] = jnp.concatenate(ys, axis=1)
    for h in range(ML_HEADS):
        c_sc[h] = c_st[h]
        n_sc[h] = n_st[h]
        m_sc[h] = jnp.broadcast_to(m_st[h], (1, ML_HEAD_DIM))

    @pl.when(c == pl.num_programs(1) - 1)
    def _():
        cout_ref[0] = c_sc[...]
        nout_ref[0] = n_sc[...]
        mout_ref[0] = m_sc[...]


def _mlstm_call(p_ml, p_small, p_small_t, c_init, n_init, m_init, sl, wts, l, *, nb, nc, q, nsub, valid, row0,
                prev):
    ta = p_ml.shape[0]
    blk = q * nsub
    assert row0 % blk == 0 and (valid == q or nsub == 1)
    rows, smt_spec = _seq_specs(blk, nc, row0 // blk, p_small_t)
    ws = tuple(wts[k] for k in ("ib_c", "ib_r", "fb_c", "fb_r", "mlstm_norm"))
    c_spec = pl.BlockSpec((1, ML_HEADS, ML_HEAD_DIM, ML_HEAD_DIM), lambda b, c: (b, 0, 0, 0))
    v_spec = pl.BlockSpec((1, ML_HEADS, 1, ML_HEAD_DIM), lambda b, c: (b, 0, 0, 0))
    in_specs = [rows(4 * ML_WIDTH), rows(SMALL_COLS), smt_spec, _state_spec(c_init, sl), _state_spec(n_init, sl),
                _state_spec(m_init, sl)]
    in_specs += [_layer(w, l) for w in ws]
    args = [p_ml, p_small, p_small_t, c_init, n_init, m_init, *ws]
    kern, aliases = _with_aliased(functools.partial(_mlstm_kernel, q=q, nsub=nsub, valid=valid), in_specs, args,
                                  [] if prev is None else [(prev, 0)])
    vec = jax.ShapeDtypeStruct((nb, ML_HEADS, 1, ML_HEAD_DIM), F32)
    return pl.pallas_call(
        kern,
        grid=(nb, nc),
        in_specs=in_specs,
        out_specs=[rows(ML_WIDTH), c_spec, v_spec, v_spec],
        out_shape=[jax.ShapeDtypeStruct((ta, ML_WIDTH), F32),
                   jax.ShapeDtypeStruct((nb, ML_HEADS, ML_HEAD_DIM, ML_HEAD_DIM), F32), vec, vec],
        scratch_shapes=[pltpu.VMEM((ML_HEADS, ML_HEAD_DIM, ML_HEAD_DIM), F32),
                        pltpu.VMEM((ML_HEADS, 1, ML_HEAD_DIM), F32),
                        pltpu.VMEM((ML_HEADS, 1, ML_HEAD_DIM), F32)],
        input_output_aliases=aliases,
        compiler_params=_params(2),
        name="mlstm_q%d" % q,
    )(*args)


def _s5_kernel(u_ref, h0_ref, btc_ref, smc_ref, cmc_ref, e_ref, pa_ref, pb_ref, va_ref, vb_ref,
               y_ref, hout_ref, tz_ref, sm_ref, cm_ref, *, rows, scan):
    half = S5_SUPER_STATE // 2
    q, lanes, ng = S5_CHUNK, S5_LANES, S5_SUPER_GROUPS

    @pl.when(pl.program_id(1) == 0)
    def _():
        def expand(compact):
            hi = compact.astype(BF16)
            lo = (compact - hi.astype(F32)).astype(BF16)
            full = (jnp.dot(hi, e_ref[...], preferred_element_type=F32)
                    + jnp.dot(lo, e_ref[...], preferred_element_type=F32))
            rg = (lax.broadcasted_iota(jnp.int32, full.shape, 0) >> 4) & (ng - 1)
            cg = (lax.broadcasted_iota(jnp.int32, full.shape, 1) >> 6) & (ng - 1)
            return jnp.where(rg == cg, full, 0.0)

        def split(x):
            hi = x.astype(BF16)
            return hi, (x - hi.astype(F32)).astype(BF16)

        for r0 in range(0, q * lanes, 4 * lanes):
            sm_ref[r0:r0 + 4 * lanes, :] = expand(smc_ref[r0:r0 + 4 * lanes, :]).astype(BF16)
        bt_hi, bt_lo = split(expand(btc_ref[...]))
        nt = lambda a, b: lax.dot_general(a, b, (((1,), (1,)), ((), ())), preferred_element_type=F32)
        zero = jnp.zeros((lanes, lanes), BF16)
        for lag in range(q + 1):
            blk = expand(cmc_ref[lag * lanes:(lag + 1) * lanes, :])
            cm_ref[lag * lanes:(lag + 1) * lanes, :] = blk.astype(BF16)
            if lag < q:
                c_hi, c_lo = split(blk)
                k_lag = (nt(bt_hi, c_hi) + nt(bt_hi, c_lo) + nt(bt_lo, c_hi)).astype(BF16)
                for s in range(q - lag):
                    t = s + lag
                    tz_ref[s * lanes:(s + 1) * lanes, t * lanes:(t + 1) * lanes] = k_lag
                    if lag > 0:
                        tz_ref[t * lanes:(t + 1) * lanes, s * lanes:(s + 1) * lanes] = zero

    ucat = jnp.concatenate([u_ref[pl.ds(s, rows, stride=S5_CHUNK), :] for s in range(S5_CHUNK)], axis=1)
    ub = ucat.astype(BF16)

    def cmul(a, b, x):
        return a * x + b * pltpu.roll(x, half, axis=1)

    contrib = jnp.dot(ub, sm_ref[...], preferred_element_type=F32)
    h0 = h0_ref[0]
    carried = cmul(va_ref[...], vb_ref[...], h0)
    if scan:
        ridx = lax.broadcasted_iota(jnp.int32, (rows, S5_SUPER_STATE), 0)
        x = contrib + jnp.where(ridx == 0, carried, 0.0)
        k = 0
        while (1 << k) < rows:
            s = 1 << k
            shifted = jnp.where(ridx >= s, pltpu.roll(x, s, axis=0), 0.0)
            x = x + cmul(pa_ref[k:k + 1, :], pb_ref[k:k + 1, :], shifted)
            k += 1
        hprev = jnp.where(ridx == 0, h0, pltpu.roll(x, 1, axis=0))
        hout_ref[0] = x[rows - 1:rows, :]
    else:
        x = contrib + carried
        hprev = h0
        hout_ref[0] = x
    y = jnp.dot(ub, tz_ref[...], preferred_element_type=F32) + _dot_nt(hprev, cm_ref[lanes:, :])
    for t in range(S5_CHUNK):
        y_ref[pl.ds(t, rows, stride=S5_CHUNK), :] = y[:, t * S5_LANES:(t + 1) * S5_LANES]


def _s5_call(p_s5, h0, tab, l, *, nseq, rows, scan, row0, prev):
    ta = p_s5.shape[0]
    blk_rows = rows * S5_CHUNK
    blk0 = row0 // blk_rows
    hrows = h0.shape[2]
    tok = pl.BlockSpec((blk_rows, S5_LANES), lambda sb, b: (blk0 + b, sb))
    hspec = pl.BlockSpec((None, 1, hrows, S5_SUPER_STATE), lambda sb, b: (sb, b, 0, 0))
    va, vb = (tab["va16"], tab["vb16"]) if scan else (tab["va"], tab["vb"])
    per_sb = lambda a: pl.BlockSpec((None, None) + a.shape[2:], lambda sb, b: (l, sb) + (0,) * (a.ndim - 2))
    tabs = [(tab["btc"], per_sb), (tab["smc16"] if scan else tab["smc"], per_sb), (tab["cmc"], per_sb),
            (tab["e"], lambda a: _full(a.shape)),
            (tab["pa"], per_sb), (tab["pb"], per_sb), (va, per_sb), (vb, per_sb)]
    in_specs = [tok, hspec] + [mk(a) for a, mk in tabs]
    args = [p_s5, h0] + [a for a, _ in tabs]
    kern, aliases = _with_aliased(functools.partial(_s5_kernel, rows=rows, scan=scan), in_specs, args,
                                  [] if prev is None else [(prev, 0)])
    folded = S5_CHUNK * S5_LANES
    return pl.pallas_call(
        kern,
        grid=(S5_SUPER, nseq),
        in_specs=in_specs,
        out_specs=[tok, hspec],
        out_shape=[jax.ShapeDtypeStruct((ta, S5_WIDTH), F32), jax.ShapeDtypeStruct(h0.shape, F32)],
        scratch_shapes=[pltpu.VMEM((folded, folded), BF16), pltpu.VMEM((folded, S5_SUPER_STATE), BF16),
                        pltpu.VMEM((folded + S5_LANES, S5_SUPER_STATE), BF16)],
        input_output_aliases=aliases,
        compiler_params=_params(2),
        name="s5_scan" if scan else "s5_step",
    )(*args)


def _s5_tables(a_re, a_im, log_dt, b_re, b_im, c_re, c_im, valid):
    q = S5_CHUNK
    depth = a_re.shape[0]
    nsb, ng = S5_SUPER, S5_SUPER_GROUPS
    dt = jnp.exp(log_dt)[..., None]
    mag = jnp.exp(a_re * dt)
    ab_re = mag * jnp.cos(a_im * dt)
    ab_im = mag * jnp.sin(a_im * dt)
    inv = 1.0 / (a_re * a_re + a_im * a_im)
    co_re = ((ab_re - 1.0) * a_re + ab_im * a_im) * inv
    co_im = (ab_im * a_re - (ab_re - 1.0) * a_im) * inv

    def cmul(x, y):
        return x[0] * y[0] - x[1] * y[1], x[0] * y[1] + x[1] * y[0]

    pw = [(jnp.ones_like(ab_re), jnp.zeros_like(ab_re))]
    for _ in range(q):
        pw.append(cmul(pw[-1], (ab_re, ab_im)))
    row = lambda v: v[:, :, None, :]
    bt_t = (jnp.transpose(b_re, (0, 1, 3, 2)), jnp.transpose(b_im, (0, 1, 3, 2)))
    bt_re = row(co_re) * bt_t[0] - row(co_im) * bt_t[1]
    bt_im = row(co_re) * bt_t[1] + row(co_im) * bt_t[0]

    def compact(blocks):
        x = jnp.stack(blocks, axis=2)
        x = x.reshape(depth, nsb, ng, len(blocks), S5_GROUP, 2 * S5_STATE)
        return jnp.transpose(x, (0, 1, 3, 2, 4, 5)).reshape(depth, nsb, len(blocks) * S5_LANES, 2 * S5_STATE)

    def times(p):
        pr, pi = row(p[0]), row(p[1])
        b = jnp.concatenate([pr * bt_re - pi * bt_im, pr * bt_im + pi * bt_re], axis=-1)
        c = jnp.concatenate([c_re * pr - c_im * pi, -(c_re * pi + c_im * pr)], axis=-1)
        return b, c

    def state_in(nvalid):
        zero = jnp.zeros((depth, S5_GROUPS, S5_GROUP, 2 * S5_STATE), F32)
        return compact([times(pw[nvalid - 1 - s])[0] if s < nvalid else zero for s in range(q)])

    btc = compact([times(pw[0])[0]])
    cmc = compact([times(p)[1] for p in pw])
    e = np.zeros((2, S5_STATE, 2, ng, S5_STATE), np.float32)
    for g in range(ng):
        e[:, :, :, g, :] = np.eye(2 * S5_STATE).reshape(2, S5_STATE, 2, S5_STATE)
    e = jnp.asarray(e.reshape(2 * S5_STATE, S5_SUPER_STATE), BF16)

    def packed(p):
        pr = p[0].reshape(depth, nsb, ng * S5_STATE)
        pi = p[1].reshape(depth, nsb, ng * S5_STATE)
        return jnp.concatenate([pr, pr], axis=-1), jnp.concatenate([-pi, pi], axis=-1)

    doubling = [pw[q]]
    for _ in range(S5_DOUBLINGS - 1):
        doubling.append(cmul(doubling[-1], doubling[-1]))
    pa = jnp.stack([packed(p)[0] for p in doubling], axis=2)
    pb = jnp.stack([packed(p)[1] for p in doubling], axis=2)
    va16, vb16 = packed(pw[q])
    va, vb = packed(pw[valid])
    ex = lambda a: a[:, :, None, :]
    return {"btc": btc, "smc16": state_in(q), "smc": state_in(valid), "cmc": cmc, "e": e,
            "pa": pa, "pb": pb, "va16": ex(va16), "vb16": ex(vb16), "va": ex(va), "vb": ex(vb)}


def _t5_bucket(dist):
    dist = np.asarray(dist)
    large = REL_MAX_EXACT + (np.log(np.maximum(dist, 1) / REL_MAX_EXACT)
                             / math.log(REL_MAX_DIST / REL_MAX_EXACT)
                             * (REL_BUCKETS - REL_MAX_EXACT)).astype(np.int32)
    large = np.minimum(large, REL_BUCKETS - 1)
    return np.where(dist < REL_MAX_EXACT, dist, large).astype(np.int32)


def _bias_steps(rel_bias, grp, dil):
    heads = slice(grp * SWA_GROUP_HEADS, (grp + 1) * SWA_GROUP_HEADS)
    buckets = _t5_bucket(dil * np.arange(SWA_BLOCK + 1))
    onehot = np.zeros((SWA_BLOCK + 1, REL_BUCKETS), np.float32)
    onehot[np.arange(SWA_BLOCK + 1), buckets] = 1.0
    steps = jnp.einsum("jb,bh->hj", onehot, rel_bias[:, heads], precision=lax.Precision.HIGHEST)
    return steps.astype(F32)


def _softmax_pieces(logits):
    m = logits[0].max(axis=1, keepdims=True)
    for s in logits[1:]:
        m = jnp.maximum(m, s.max(axis=1, keepdims=True))
    ps = [jnp.exp(s - m) for s in logits]
    den = ps[0].sum(axis=1, keepdims=True)
    for p in ps[1:]:
        den = den + p.sum(axis=1, keepdims=True)
    return ps, den, m + jnp.log(den)


SWA_SAMPLE_LANES = 2048
SWA_UNROLL = 8
SWA_DENSE_BLOCKS = 8


def _swa_prompt_kernel(q_ref, k_ref, v_ref, kp_ref, vp_ref, bias_ref, o_ref, lse_ref, kvt_ref, *, dil):
    n = SWA_BLOCK
    pair = pl.program_id(1)
    first = pl.program_id(2) == 0
    scale = SWA_HEAD_DIM ** -0.5

    ones = jnp.ones((n, SWA_HEAD_DIM), BF16)

    def logits(sl, kp, vp, mask_prev):
        qq, kk = q_ref[sl, :], k_ref[sl, :]
        heads = []
        for j in range(2):
            hs = slice(j * SWA_HEAD_DIM, (j + 1) * SWA_HEAD_DIM)
            bias = bias_ref[2 * pair + j]
            bias_prev = bias[:, 0:n] if mask_prev is None else jnp.where(mask_prev, MASKED, bias[:, 0:n])
            q_h = qq[:, hs] * scale
            s_cur = _dot_nt(q_h, kk[:, hs]) + bias[:, n:2 * n]
            s_prev = _dot_nt(q_h, kp[:, hs]) + bias_prev
            heads.append((s_cur, s_prev, jnp.max(jnp.maximum(s_cur, s_prev), axis=1, keepdims=True)))
        return sl, vp, heads

    def finish(sl, vp, heads):
        vv = v_ref[sl, :]
        outs, lses = [], []
        for j, (s_cur, s_prev, m) in enumerate(heads):
            hs = slice(j * SWA_HEAD_DIM, (j + 1) * SWA_HEAD_DIM)
            p_cur = jnp.exp(s_cur - m).astype(BF16)
            p_prev = jnp.exp(s_prev - m).astype(BF16)
            den = jnp.dot(p_cur, ones, preferred_element_type=F32) + jnp.dot(p_prev, ones, preferred_element_type=F32)
            outs.append((_dot(p_cur, vv[:, hs]) + _dot(p_prev, vp[:, hs])) / den)
            lses.append(m + jnp.log(den))
        o_ref[sl, :] = jnp.concatenate(outs, axis=1)
        lse_ref[sl, :] = jnp.concatenate(lses, axis=1)

    def run(tiles):
        for t in [logits(*a) for a in tiles]:
            finish(*t)

    if dil == 1:
        tiles = []
        for j in range(SWA_DENSE_BLOCKS):
            if j == 0:
                tiles.append((slice(0, n), kp_ref[...], vp_ref[...], first))
            else:
                before = slice((j - 1) * n, j * n)
                tiles.append((slice(j * n, (j + 1) * n), k_ref[before, :], v_ref[before, :], None))
        run(tiles)
    else:
        unroll = min(dil, SWA_UNROLL)

        def group(i, carry):
            tiles = []
            for u in range(unroll):
                sl = pl.ds(i * unroll + u, n, stride=dil)
                tiles.append((sl, kp_ref[sl, :], vp_ref[sl, :], first))
            run(tiles)
            return carry
        if dil == unroll:
            group(0, 0)
        else:
            lax.fori_loop(0, dil // unroll, group, 0)
    rows = k_ref.shape[0]
    kvt_ref[0] = k_ref[rows - n * dil:rows, :].T
    kvt_ref[1] = v_ref[rows - n * dil:rows, :].T


def _swa_prompt_call(p_swa, bias, kvt_prev, l, depth, *, grp, dil, nb, seq):
    ta = p_swa.shape[0]
    win = SWA_BLOCK * dil
    sb = SWA_BLOCK * SWA_DENSE_BLOCKS if dil == 1 else win
    assert dil == 1 or dil % min(dil, SWA_UNROLL) == 0
    nsb = seq // sb
    lanes = 2 * SWA_HEAD_DIM
    npair = SWA_GROUP_HEADS // 2
    cur = lambda col: pl.BlockSpec((sb, lanes), lambda b, p, c: (b * nsb + c, 2 * col + p))
    if dil == 1:
        per = sb // SWA_BLOCK
        prv = lambda col: pl.BlockSpec(
            (SWA_BLOCK, lanes), lambda b, p, c: (jnp.maximum((b * nsb + c) * per - 1, 0), 2 * col + p))
    else:
        prv = lambda col: pl.BlockSpec((sb, lanes), lambda b, p, c: (b * nsb + jnp.maximum(c - 1, 0), 2 * col + p))
    out = pl.BlockSpec((sb, lanes), lambda b, p, c: (b * nsb + c, p))
    kvt_spec = pl.BlockSpec((None, None, 2, None, lanes, win), lambda b, p, c: (l, b, 0, p, 0, 0))
    shape = jax.ShapeDtypeStruct((ta, SWA_GROUP_WIDTH), F32)
    in_specs = [cur(grp), cur(3 + grp), cur(6 + grp), prv(3 + grp), prv(6 + grp), _full(bias.shape)]
    args = [p_swa, p_swa, p_swa, p_swa, p_swa, bias]
    kern, aliases = _with_aliased(functools.partial(_swa_prompt_kernel, dil=dil), in_specs, args,
                                  [] if kvt_prev is None else [(kvt_prev, 2)])
    return pl.pallas_call(
        kern,
        grid=(nb, npair, nsb),
        in_specs=in_specs,
        out_specs=[out, out, kvt_spec],
        out_shape=[shape, shape, jax.ShapeDtypeStruct((depth, nb, 2, npair, lanes, win), F32)],
        input_output_aliases=aliases,
        compiler_params=_params(3),
        name="swa_prompt_d%d" % dil,
    )(*args)


def _swa_prompt_bias(steps):
    n = SWA_BLOCK
    period = 3 * n + 1
    f = jnp.concatenate([steps[:, ::-1], jnp.full((steps.shape[0], period - (n + 1)), MASKED, F32)], axis=1)
    tiled = jnp.tile(f, (1, n))[:, :n * (period - 1)]
    return tiled.reshape(steps.shape[0], n, period - 1)[:, :, :2 * n]


def _swa_sample_kernel(q_ref, k_ref, v_ref, buf_ref, bias_buf_ref, bias_new_ref, o_ref, lse_ref, cache_ref, *,
                       width, t_new):
    scale = SWA_HEAD_DIM ** -0.5
    w = SWA_GROUP_WIDTH
    for i in range(buf_ref.shape[0]):
        rs = slice(i * SAMPLE_ROWS, (i + 1) * SAMPLE_ROWS)
        qq, kn, vn = q_ref[rs, :], k_ref[rs, :], v_ref[rs, :]
        outs, lses = [], []
        for h in range(SWA_GROUP_HEADS):
            hs = slice(h * SWA_HEAD_DIM, (h + 1) * SWA_HEAD_DIM)
            k_t = buf_ref[i, h * SWA_HEAD_DIM:(h + 1) * SWA_HEAD_DIM, :]
            v_t = buf_ref[i, w + h * SWA_HEAD_DIM:w + (h + 1) * SWA_HEAD_DIM, :]
            q_h = qq[:, hs] * scale
            ps, den, lse = _softmax_pieces([_dot(q_h, k_t) + bias_buf_ref[h],
                                            _dot_nt(q_h, kn[:, hs]) + bias_new_ref[h]])
            outs.append((_dot_nt(ps[0], v_t) + _dot(ps[1], vn[:, hs])) / den)
            lses.append(jnp.broadcast_to(lse, (SAMPLE_ROWS, SWA_HEAD_DIM)))
        o_ref[rs, :] = jnp.concatenate(outs, axis=1)
        lse_ref[rs, :] = jnp.concatenate(lses, axis=1)
        new_t = jnp.concatenate([kn, vn], axis=1).T
        cache_ref[i] = pltpu.roll(buf_ref[i], width - t_new, axis=1)
        cache_ref[i, :, width - t_new:width] = new_t[:, 0:t_new]


def _swa_sample_call(p_swa, cache_t, bias_buf, bias_new, prev_o, prev_lse, prev_cache, l, *, grp, nb, row0, t_new):
    ta = p_swa.shape[0]
    width = cache_t.shape[3]
    w = SWA_GROUP_WIDTH
    per = max(1, min(nb, SWA_SAMPLE_LANES // width))
    while nb % per:
        per -= 1
    assert row0 % (SAMPLE_ROWS * per) == 0
    blk0 = row0 // (SAMPLE_ROWS * per)
    nb = nb // per
    tok = lambda col: pl.BlockSpec((SAMPLE_ROWS * per, w), lambda b: (blk0 + b, col))
    cache_spec = pl.BlockSpec((None, per, 2 * w, width), lambda b: (l, b, 0, 0))
    in_specs = [tok(grp), tok(3 + grp), tok(6 + grp), cache_spec, _full(bias_buf.shape), _full(bias_new.shape)]
    args = [p_swa, p_swa, p_swa, cache_t, bias_buf, bias_new]
    aliased = [(prev_o, 0), (prev_lse, 1)] + ([] if prev_cache is None else [(prev_cache, 2)])
    kern, aliases = _with_aliased(functools.partial(_swa_sample_kernel, width=width, t_new=t_new), in_specs, args,
                                  aliased)
    return pl.pallas_call(
        kern,
        grid=(nb,),
        in_specs=in_specs,
        out_specs=[tok(0), tok(0), cache_spec],
        out_shape=[jax.ShapeDtypeStruct((ta, w), F32), jax.ShapeDtypeStruct((ta, w), F32),
                   jax.ShapeDtypeStruct(cache_t.shape, F32)],
        input_output_aliases=aliases,
        compiler_params=_params(1),
        name="swa_sample_w%d" % width,
    )(*args)


def _swa_sample_bias(steps, window, dil, width, t_new):
    n = window // dil
    nh = steps.shape[0]
    g = jnp.pad(steps[:, :, None], ((0, 0), (0, 0), (0, dil - 1)), constant_values=MASKED).reshape(nh, (n + 1) * dil)
    g = jnp.pad(g, ((0, 0), (0, width + SAMPLE_ROWS)), constant_values=MASKED)
    masked_row = jnp.full((nh, 1, width), MASKED, F32)
    rows = [g[:, t + 1:t + 1 + width][:, None, ::-1] if t < t_new else masked_row for t in range(SAMPLE_ROWS)]
    b_buf = jnp.concatenate(rows, axis=1)
    new_rows = []
    for t in range(SAMPLE_ROWS):
        if t < t_new:
            row = jnp.concatenate([g[:, 0:t + 1][:, ::-1], jnp.full((nh, SAMPLE_ROWS - t - 1), MASKED, F32)], axis=1)
        else:
            row = jnp.full((nh, SAMPLE_ROWS), MASKED, F32)
        new_rows.append(row[:, None, :])
    return b_buf, jnp.concatenate(new_rows, axis=1)


def _merge_kernel(x_ref, yssd_ref, o0_ref, l0_ref, o1_ref, l1_ref, o2_ref, l2_ref, ys5_ref, u_ref, yml_ref,
                  g_ref, wg_ref, wssd_ref, wswa_ref, ws5_ref, wml_ref, wglu_ref, wout_ref, d_ref, out_ref):
    for rs in _sub_tiles(x_ref.shape[0]):
        x = x_ref[rs, :]
        h = _rms(x, g_ref[...]).astype(BF16)
        l0, l1, l2 = l0_ref[rs, :], l1_ref[rs, :], l2_ref[rs, :]
        m = jnp.maximum(jnp.maximum(l0, l1), l2)
        e0, e1, e2 = jnp.exp(l0 - m), jnp.exp(l1 - m), jnp.exp(l2 - m)
        y_swa = (e0 * o0_ref[rs, :] + e1 * o1_ref[rs, :] + e2 * o2_ref[rs, :]) / (e0 + e1 + e2)
        y5 = ys5_ref[rs, :] + d_ref[...] * u_ref[rs, :]
        y_s5 = y5 * _sigmoid(_dot(y5, wglu_ref[...]))
        branches = ((yssd_ref[rs, :], wssd_ref), (y_swa, wswa_ref), (y_s5, ws5_ref), (yml_ref[rs, :], wml_ref))
        merged = None
        for i, (y, w_ref) in enumerate(branches):
            gate = _sigmoid(jnp.dot(h, wg_ref[:, i * D_MODEL:(i + 1) * D_MODEL], preferred_element_type=F32))
            term = gate * _dot(y, w_ref[...])
            merged = term if merged is None else merged + term
        out_ref[rs, :] = x + _dot(merged, wout_ref[...])


def _merge_call(x, acts, wts, l):
    ta = x.shape[0]
    tm = TOKEN_TILE
    row = lambda a: pl.BlockSpec((tm, a.shape[1]), lambda i: (i, 0))
    ws = tuple(wts[k] for k in ("norm_mix", "w_gate", "w_br_ssd", "w_br_swa", "w_br_s5", "w_br_mlstm", "s5_w_glu",
                                "w_out", "s5_d"))
    return pl.pallas_call(
        _merge_kernel,
        grid=(ta // tm,),
        in_specs=[row(a) for a in (x, *acts)] + [_layer(w, l) for w in ws],
        out_specs=row(x),
        out_shape=jax.ShapeDtypeStruct(x.shape, F32),
        compiler_params=_params(1),
        name="merge",
    )(x, *acts, *ws)


def _norm_matmul_kernel(x_ref, g_ref, w_ref, o_ref):
    o_ref[...] = jnp.dot(_rms(x_ref[...], g_ref[...]).astype(BF16), w_ref[...], preferred_element_type=F32)


def _norm_matmul(x, g, w, l, tm):
    rows = x.shape[0]
    return pl.pallas_call(
        _norm_matmul_kernel,
        grid=(rows // tm,),
        in_specs=[pl.BlockSpec((tm, x.shape[1]), lambda i: (i, 0)), _layer(g, l), _layer(w, l)],
        out_specs=pl.BlockSpec((tm, w.shape[2]), lambda i: (i, 0)),
        out_shape=jax.ShapeDtypeStruct((rows, w.shape[2]), F32),
        compiler_params=_params(1),
        name="norm_matmul",
    )(x, g, w)


def _rms_kernel(x_ref, g_ref, o_ref):
    o_ref[...] = _rms(x_ref[...], g_ref[...])


def _rmsnorm_call(x, g, tm):
    rows = x.shape[0]
    return pl.pallas_call(
        _rms_kernel,
        grid=(rows // tm,),
        in_specs=[pl.BlockSpec((tm, x.shape[1]), lambda i: (i, 0)), _full(g.shape)],
        out_specs=pl.BlockSpec((tm, x.shape[1]), lambda i: (i, 0)),
        out_shape=jax.ShapeDtypeStruct(x.shape, F32),
        compiler_params=_params(1),
        name="final_norm",
    )(x, g)


def _xattn_heads(q, kv_ref, kv_rows):
    scale = XA_HEAD_DIM ** -0.5
    n_mem = kv_ref.shape[0] // (2 * XA_HEADS) if kv_rows else kv_ref.shape[0]
    outs = []
    for h in range(XA_HEADS):
        hs = slice(h * XA_HEAD_DIM, (h + 1) * XA_HEAD_DIM)
        if kv_rows:
            k_h = kv_ref[h * n_mem:(h + 1) * n_mem, :]
            v_h = kv_ref[(XA_HEADS + h) * n_mem:(XA_HEADS + h + 1) * n_mem, :]
        else:
            k_h = kv_ref[:, hs]
            v_h = kv_ref[:, D_MODEL + h * XA_HEAD_DIM:D_MODEL + (h + 1) * XA_HEAD_DIM]
        logits = _dot_nt(q[:, hs], k_h) * scale
        m = logits.max(axis=1, keepdims=True)
        p = jnp.exp(logits - m)
        outs.append(_dot(p / p.sum(axis=1, keepdims=True), v_h))
    return jnp.concatenate(outs, axis=1)


def _xattn_prompt_kernel(x_ref, kv_ref, g_ref, wq_ref, wo_ref, out_ref):
    for rs in _sub_tiles(x_ref.shape[0]):
        x = x_ref[rs, :]
        q = jnp.dot(_rms(x, g_ref[...]).astype(BF16), wq_ref[...], preferred_element_type=F32)
        out_ref[rs, :] = x + _dot(_xattn_heads(q, kv_ref, False), wo_ref[...])


def _xattn_prompt_call(x, kv, wts, l, *, n_tiles, tiles_per_seq):
    ta = x.shape[0]
    tm = TOKEN_TILE
    row = pl.BlockSpec((tm, D_MODEL), lambda i: (i, 0))
    ws = tuple(wts[k] for k in ("norm_xa", "xa_wq", "xa_wo"))
    return pl.pallas_call(
        _xattn_prompt_kernel,
        grid=(n_tiles,),
        in_specs=[row, pl.BlockSpec((None,) + kv.shape[1:], lambda i: (i // tiles_per_seq, 0, 0))]
        + [_layer(w, l) for w in ws],
        out_specs=row,
        out_shape=jax.ShapeDtypeStruct((ta, D_MODEL), F32),
        compiler_params=_params(1),
        name="xattn_prompt",
    )(x, kv, *ws)


def _xattn_sample_kernel(x_ref, kv_ref, g_ref, wq_ref, wo_ref, out_ref, q_sc, o_sc):
    b = pl.program_id(0)

    @pl.when(b == 0)
    def _():
        q_sc[...] = jnp.dot(_rms(x_ref[...], g_ref[...]).astype(BF16), wq_ref[...], preferred_element_type=F32)

    rows = pl.ds(pl.multiple_of(b * SAMPLE_ROWS, SAMPLE_ROWS), SAMPLE_ROWS)
    o_sc[rows, :] = _xattn_heads(q_sc[rows, :], kv_ref, True)

    @pl.when(b == pl.num_programs(0) - 1)
    def _():
        out_ref[...] = x_ref[...] + _dot(o_sc[...], wo_ref[...])


def _xattn_sample_call(x, kv, wts, l, *, nb, row0, prev):
    ta = x.shape[0]
    n_s = nb * SAMPLE_ROWS
    rows = pl.BlockSpec((n_s, D_MODEL), lambda b: (row0 // n_s, 0))
    ws = tuple(wts[k] for k in ("norm_xa", "xa_wq", "xa_wo"))
    in_specs = [rows, pl.BlockSpec((None, None) + kv.shape[2:], lambda b: (l, b, 0, 0))] + [_layer(w, l) for w in ws]
    args = [x, kv, *ws]
    kern, aliases = _with_aliased(_xattn_sample_kernel, in_specs, args, [(prev, 0)])
    return pl.pallas_call(
        kern,
        grid=(nb,),
        in_specs=in_specs,
        out_specs=rows,
        out_shape=jax.ShapeDtypeStruct((ta, D_MODEL), F32),
        scratch_shapes=[pltpu.VMEM((n_s, D_MODEL), F32), pltpu.VMEM((n_s, D_MODEL), F32)],
        input_output_aliases=aliases,
        compiler_params=_params(1),
        name="xattn_sample",
    )(*args)


def _ffn_kernel(x_ref, g_ref, w1_ref, w2_ref, out_ref):
    for rs in _sub_tiles(x_ref.shape[0]):
        x = x_ref[rs, :]
        a = jnp.dot(_rms(x, g_ref[...]).astype(BF16), w1_ref[...], preferred_element_type=F32)
        a = jnp.square(jnp.maximum(a, 0.0))
        out_ref[rs, :] = x + _dot(a, w2_ref[...])


def _ffn_call(x, wts, l):
    ta = x.shape[0]
    tm = TOKEN_TILE
    row = pl.BlockSpec((tm, D_MODEL), lambda i: (i, 0))
    ws = tuple(wts[k] for k in ("norm_mlp", "w_ff1", "w_ff2"))
    return pl.pallas_call(
        _ffn_kernel,
        grid=(ta // tm,),
        in_specs=[row] + [_layer(w, l) for w in ws],
        out_specs=row,
        out_shape=jax.ShapeDtypeStruct(x.shape, F32),
        compiler_params=_params(1),
        name="ffn",
    )(x, *ws)


def _prep_weights(w):
    o = IN_OFFS
    w_in = w["w_in"]
    small = jnp.concatenate([w_in[:, :, o[2]:o[3]], w_in[:, :, o[11]:o[13]]], axis=2)
    row = lambda v: v[:, None, :].astype(F32)
    col = lambda v: v[:, :, None].astype(F32)
    bf = lambda v: v.astype(BF16)
    return {
        "norm_mix": row(w["norm_mix"]),
        "w_main": jnp.concatenate([w_in[:, :, o[0]:o[2]], w_in[:, :, o[3]:o[11]]], axis=2).astype(BF16),
        "w_gate": bf(w_in[:, :, o[13]:o[14]]),
        "w_small": jnp.pad(small, ((0, 0), (0, 0), (0, SMALL_COLS - small.shape[2]))).astype(BF16),
        "w_small_t": jnp.transpose(small, (0, 2, 1)).astype(BF16),
        "ssd_conv_w": w["ssd_conv_w"], "ssd_conv_b": row(w["ssd_conv_b"]),
        "dtb_c": row(w["ssd_dt_bias"]), "dtb_r": col(w["ssd_dt_bias"]),
        "alog_c": row(w["ssd_a_log"]), "alog_r": col(w["ssd_a_log"]),
        "ssd_dskip": row(jnp.repeat(w["ssd_d"], SSD_HEAD_DIM, axis=1)), "ssd_norm": row(w["ssd_norm"]),
        "ib_c": row(w["mlstm_i_bias"]), "ib_r": col(w["mlstm_i_bias"]),
        "fb_c": row(w["mlstm_f_bias"]), "fb_r": col(w["mlstm_f_bias"]),
        "mlstm_norm": row(w["mlstm_norm"]),
        "s5_d": row(w["s5_d"]), "s5_w_glu": bf(w["s5_w_glu"]),
        "w_br_ssd": bf(w["w_br_ssd"]), "w_br_swa": bf(w["w_br_swa"]), "w_br_s5": bf(w["w_br_s5"]),
        "w_br_mlstm": bf(w["w_br_mlstm"]), "w_out": bf(w["w_out"]),
        "norm_xa": row(w["norm_xa"]), "xa_wq": bf(w["xa_wq"]), "xa_wo": bf(w["xa_wo"]),
        "norm_mem": row(w["norm_mem"]),
        "xa_wkv": jnp.concatenate([w["xa_wk"], w["xa_wv"]], axis=2).astype(BF16),
        "norm_mlp": row(w["norm_mlp"]), "w_ff1": bf(w["w_ff1"]), "w_ff2": bf(w["w_ff2"]),
    }


def kernel(x_prompt, x_sample, state_ssd, state_ssd_conv, cache_swa_w128, cache_swa_w512, cache_swa_w2048, state_s5, state_mlstm_c, state_mlstm_n, state_mlstm_m, cache_mem_kv, mem_prompt, norm_mix, w_in, ssd_conv_w, ssd_conv_b, ssd_dt_bias, ssd_a_log, ssd_d, ssd_norm, rel_bias, s5_a_re, s5_a_im, s5_log_dt, s5_b_re, s5_b_im, s5_c_re, s5_c_im, s5_d, s5_w_glu, mlstm_i_bias, mlstm_f_bias, mlstm_norm, w_br_ssd, w_br_swa, w_br_s5, w_br_mlstm, w_out, norm_xa, norm_mem, xa_wq, xa_wk, xa_wv, xa_wo, norm_mlp, w_ff1, w_ff2, norm_final):
    wts = _prep_weights(dict(
        norm_mix=norm_mix, w_in=w_in, ssd_conv_w=ssd_conv_w, ssd_conv_b=ssd_conv_b, ssd_dt_bias=ssd_dt_bias,
        ssd_a_log=ssd_a_log, ssd_d=ssd_d, ssd_norm=ssd_norm, s5_d=s5_d, s5_w_glu=s5_w_glu,
        mlstm_i_bias=mlstm_i_bias, mlstm_f_bias=mlstm_f_bias, mlstm_norm=mlstm_norm, w_br_ssd=w_br_ssd,
        w_br_swa=w_br_swa, w_br_s5=w_br_s5, w_br_mlstm=w_br_mlstm, w_out=w_out, norm_xa=norm_xa, norm_mem=norm_mem,
        xa_wq=xa_wq, xa_wk=xa_wk, xa_wv=xa_wv, xa_wo=xa_wo, norm_mlp=norm_mlp, w_ff1=w_ff1, w_ff2=w_ff2))
    nb_p, seq, _ = x_prompt.shape
    nb_s, t_new, _ = x_sample.shape
    depth = w_in.shape[0]
    n_mem = mem_prompt.shape[1]
    rows_s = SAMPLE_ROWS
    n_p = nb_p * seq
    n_s = nb_s * rows_s
    ta = n_p + n_s
    assert seq % (SWA_BLOCK * SWA_PATTERN[-1][1]) == 0 and ta % TOKEN_TILE == 0 and t_new <= rows_s
    assert n_p % n_s == 0 and (seq // S5_CHUNK) & (seq // S5_CHUNK - 1) == 0
    caches = (cache_swa_w128, cache_swa_w512, cache_swa_w2048)

    xs_pad = jnp.pad(x_sample, ((0, 0), (0, rows_s - t_new), (0, 0))).reshape(n_s, D_MODEL)
    x = jnp.concatenate([x_prompt.reshape(n_p, D_MODEL), xs_pad], axis=0)

    steps = [_bias_steps(rel_bias, g, dil) for g, (_, dil) in enumerate(SWA_PATTERN)]
    prompt_bias = [_swa_prompt_bias(s) for s in steps]
    sample_bias = [_swa_sample_bias(steps[g], win, dil, caches[g].shape[2], t_new)
                   for g, (win, dil) in enumerate(SWA_PATTERN)]
    caches_t = [jnp.transpose(c, (0, 1, 3, 4, 5, 2)).reshape(depth, nb_s, 2 * SWA_GROUP_WIDTH, c.shape[2])
                for c in caches]
    mem_kv_s = jnp.transpose(cache_mem_kv, (0, 1, 3, 4, 2, 5)).reshape(depth, nb_s, 2 * XA_HEADS * n_mem, XA_HEAD_DIM)
    s5_tab = _s5_tables(s5_a_re, s5_a_im, s5_log_dt, s5_b_re, s5_b_im, s5_c_re, s5_c_im, t_new)
    s5_h0_s = jnp.transpose(state_s5.reshape(depth, nb_s, S5_SUPER, S5_SUPER_GROUPS, S5_STATE, 2),
                            (0, 2, 1, 5, 3, 4)).reshape(depth, S5_SUPER, 1, nb_s, S5_SUPER_STATE)

    zeros = lambda *s: jnp.zeros(s, F32)
    zero_conv = zeros(1, nb_p, 8, SSD_CONV_DIM)
    zero_ssd = zeros(1, nb_p, SSD_HEADS, SSD_HEAD_DIM, SSD_STATE)
    zero_c = zeros(1, nb_p, ML_HEADS, ML_HEAD_DIM, ML_HEAD_DIM)
    zero_vec = zeros(1, nb_p, ML_HEADS, 1, ML_HEAD_DIM)
    zero_s5 = zeros(S5_SUPER, nb_p, 1, S5_SUPER_STATE)
    conv_init_s = jnp.pad(state_ssd_conv, ((0, 0), (0, 0), (8 - (SSD_CONV - 1), 0), (0, 0)))
    n_init_s = state_mlstm_n[:, :, :, None, :]
    m_init_s = jnp.broadcast_to(state_mlstm_m[:, :, :, None, None], (depth, nb_s, ML_HEADS, 1, ML_HEAD_DIM))

    outs = {k: [] for k in ("ssd_p", "ssd_s", "conv_p", "conv_s", "s5_p", "s5_s", "c_p", "c_s", "n_p", "n_s",
                            "m_p", "m_s", "kv_p")}
    kvt_p = [None] * len(SWA_PATTERN)
    cache_out = [None] * len(SWA_PATTERN)
    ssd_q, ml_q, nsub = 128, 128, 4
    nchunk = seq // S5_CHUNK

    for l in range(depth):
        p_ssd, p_swa, p_s5, p_ml, p_small, p_small_t = _inproj(x, wts, l)
        small_t_s = jnp.transpose(p_small[n_p:, :SMALL_ROWS].reshape(nb_s, rows_s, SMALL_ROWS), (0, 2, 1))

        y_ssd, st_p = _ssd_call(p_ssd, p_small, p_small_t, zero_conv, zero_ssd, 0, wts, l, nb=nb_p,
                                nc=seq // (ssd_q * nsub), q=ssd_q, nsub=nsub, valid=ssd_q, row0=0, prev=None)
        y_ssd, st_s = _ssd_call(p_ssd, p_small, small_t_s, conv_init_s, state_ssd, l, wts, l,
                                nb=nb_s, nc=1, q=rows_s, nsub=1, valid=t_new, row0=n_p, prev=y_ssd)
        outs["ssd_p"].append(st_p)
        outs["ssd_s"].append(st_s)
        tail = SSD_CONV - 1
        xbc_s = p_ssd[n_p:, SSD_D_INNER:].reshape(nb_s, rows_s, SSD_CONV_DIM)[:, :t_new]
        outs["conv_p"].append(jnp.stack([p_ssd[(b + 1) * seq - tail:(b + 1) * seq, SSD_D_INNER:]
                                         for b in range(nb_p)]))
        outs["conv_s"].append(jnp.concatenate([state_ssd_conv[l], xbc_s], axis=1)[:, -(SSD_CONV - 1):])

        swa_acts = []
        for g, (win, dil) in enumerate(SWA_PATTERN):
            o_g, lse_g, kvt_p[g] = _swa_prompt_call(p_swa, prompt_bias[g], kvt_p[g], l, depth, grp=g, dil=dil,
                                                    nb=nb_p, seq=seq)
            o_g, lse_g, cache_out[g] = _swa_sample_call(p_swa, caches_t[g], sample_bias[g][0], sample_bias[g][1],
                                                        o_g, lse_g, cache_out[g], l, grp=g, nb=nb_s, row0=n_p,
                                                        t_new=t_new)
            swa_acts += [o_g, lse_g]

        y_s5, h_p = _s5_call(p_s5, zero_s5, s5_tab, l, nseq=nb_p, rows=nchunk, scan=True, row0=0, prev=None)
        y_s5, h_s = _s5_call(p_s5, s5_h0_s[l], s5_tab, l, nseq=1, rows=nb_s, scan=False, row0=n_p, prev=y_s5)
        outs["s5_p"].append(h_p)
        outs["s5_s"].append(h_s)

        y_ml, c_p, nn_p, m_p = _mlstm_call(p_ml, p_small, p_small_t, zero_c, zero_vec, zero_vec, 0, wts, l, nb=nb_p,
                                           nc=seq // (ml_q * nsub), q=ml_q, nsub=nsub, valid=ml_q, row0=0,
                                           prev=None)
        y_ml, c_s, nn_s, m_s = _mlstm_call(p_ml, p_small, small_t_s, state_mlstm_c, n_init_s, m_init_s, l, wts, l,
                                           nb=nb_s, nc=1, q=rows_s, nsub=1, valid=t_new, row0=n_p, prev=y_ml)
        for key, val in (("c_p", c_p), ("c_s", c_s), ("n_p", nn_p[:, :, 0]), ("n_s", nn_s[:, :, 0]),
                         ("m_p", m_p[:, :, 0, 0]), ("m_s", m_s[:, :, 0, 0])):
            outs[key].append(val)

        x = _merge_call(x, (y_ssd, *swa_acts, y_s5, p_s5, y_ml), wts, l)

        kv_p = _norm_matmul(mem_prompt.reshape(nb_p * n_mem, D_MODEL), wts["norm_mem"], wts["xa_wkv"], l, n_mem)
        outs["kv_p"].append(kv_p.reshape(nb_p, n_mem, 2, XA_HEADS, XA_HEAD_DIM))
        x_new = _xattn_prompt_call(x, kv_p.reshape(nb_p, n_mem, 2 * D_MODEL), wts, l, n_tiles=n_p // TOKEN_TILE,
                                   tiles_per_seq=seq // TOKEN_TILE)
        x = _xattn_sample_call(x, mem_kv_s, wts, l, nb=nb_s, row0=n_p, prev=x_new)

        x = _ffn_call(x, wts, l)

    y = _rmsnorm_call(x, norm_final[None, :], TOKEN_TILE)
    y_prompt = y[:n_p].reshape(nb_p, seq, D_MODEL)
    y_sample = y[n_p:].reshape(nb_s, rows_s, D_MODEL)[:, :t_new]
    st = lambda k: jnp.stack(outs[k])

    def s5_state(k, nb):
        h = st(k).reshape(depth, S5_SUPER, nb, 2, S5_SUPER_GROUPS, S5_STATE)
        return jnp.transpose(h, (0, 2, 1, 4, 5, 3)).reshape(depth, nb, S5_GROUPS, S5_STATE, 2)

    swa_out = []
    for g in range(len(SWA_PATTERN)):
        width_p = kvt_p[g].shape[-1]
        kp = kvt_p[g].reshape(depth, nb_p, 2, SWA_GROUP_HEADS, SWA_HEAD_DIM, width_p)
        swa_out.append(jnp.transpose(kp, (0, 1, 5, 2, 3, 4)))
        cs = cache_out[g].reshape(depth, nb_s, 2, SWA_GROUP_HEADS, SWA_HEAD_DIM, caches[g].shape[2])
        swa_out.append(jnp.transpose(cs, (0, 1, 5, 2, 3, 4)))
    return (y_prompt, y_sample, st("ssd_p"), st("ssd_s"), st("conv_p"), st("conv_s"), *swa_out,
            s5_state("s5_p", nb_p), s5_state("s5_s", nb_s), st("c_p"), st("c_s"),
            st("n_p"), st("n_s"), st("m_p"), st("m_s"), st("kv_p"))
```

```python
import functools
import math

import numpy as np
import jax
import jax.numpy as jnp
from jax import lax
from jax.experimental import pallas as pl
from jax.experimental.pallas import tpu as pltpu

F32 = jnp.float32
BF16 = jnp.bfloat16

D_MODEL = 1024
RMS_EPS = 1e-6
N_BRANCH = 4

SSD_D_INNER = 512
SSD_HEAD_DIM = 64
SSD_HEADS = 8
SSD_GROUPS = 2
SSD_STATE = 64
SSD_CONV = 4
SSD_CONV_DIM = SSD_D_INNER + 2 * SSD_GROUPS * SSD_STATE
SSD_SEG = SSD_D_INNER + SSD_CONV_DIM

SWA_PATTERN = ((128, 1), (512, 4), (2048, 16))
SWA_GROUP_HEADS = 4
SWA_HEAD_DIM = 64
SWA_GROUP_WIDTH = SWA_GROUP_HEADS * SWA_HEAD_DIM
SWA_WIDTH = 3 * SWA_GROUP_WIDTH
SWA_BLOCK = 128
REL_BUCKETS = 32
REL_MAX_EXACT = 16
REL_MAX_DIST = 2048

S5_WIDTH = 512
S5_GROUP = 16
S5_GROUPS = 32
S5_STATE = 64
S5_CHUNK = 16
S5_LANES = 128
S5_SUPER = S5_WIDTH // S5_LANES
S5_SUPER_GROUPS = S5_GROUPS // S5_SUPER
S5_SUPER_STATE = 2 * S5_SUPER_GROUPS * S5_STATE
S5_DOUBLINGS = 12

ML_WIDTH = 512
ML_HEADS = 4
ML_HEAD_DIM = 128

XA_HEADS = 4
XA_HEAD_DIM = 256
D_FF = 4096

MAIN_SSD = (0, SSD_SEG)
MAIN_SWA = (SSD_SEG, SSD_SEG + 3 * SWA_WIDTH)
MAIN_S5 = (MAIN_SWA[1], MAIN_SWA[1] + S5_WIDTH)
MAIN_ML = (MAIN_S5[1], MAIN_S5[1] + 4 * ML_WIDTH)
MAIN_COLS = MAIN_ML[1]
GATE_COLS = N_BRANCH * D_MODEL
SMALL_COLS = 128
SMALL_ROWS = 16

SAMPLE_ROWS = 16
TOKEN_TILE = 512
SUB_TILE = 256
MASKED = -1e30

VMEM_LIMIT = 56 * 1024 * 1024

IN_SIZES = (SSD_D_INNER, SSD_CONV_DIM, SSD_HEADS, SWA_WIDTH, SWA_WIDTH, SWA_WIDTH, S5_WIDTH,
            ML_WIDTH, ML_WIDTH, ML_WIDTH, ML_WIDTH, ML_HEADS, ML_HEADS, N_BRANCH * D_MODEL)
IN_OFFS = tuple(int(v) for v in np.concatenate([[0], np.cumsum(IN_SIZES)]))


def _params(n_axes):
    return pltpu.CompilerParams(dimension_semantics=("arbitrary",) * n_axes, vmem_limit_bytes=VMEM_LIMIT)


def _full(shape):
    nd = len(shape)
    return pl.BlockSpec(shape, lambda *_: (0,) * nd)


def _layer(a, l):
    nd = a.ndim
    return pl.BlockSpec((None,) + a.shape[1:], lambda *_: (l,) + (0,) * (nd - 1), pipeline_mode=pl.Buffered(1))


def _sub_tiles(rows):
    step = min(SUB_TILE, rows)
    return [slice(r, r + step) for r in range(0, rows, step)]


def _with_aliased(kern, in_specs, args, aliased):
    n = len(args)
    aliases = {}
    for arr, out_idx in aliased:
        in_specs.append(pl.BlockSpec(memory_space=pl.ANY))
        aliases[len(args)] = out_idx
        args.append(arr)
    k = len(aliased)
    if k == 0:
        return kern, aliases

    def wrapped(*refs):
        return kern(*refs[:n], *refs[n + k:])
    return wrapped, aliases


def _dot(a, b):
    return jnp.dot(a.astype(BF16), b.astype(BF16), preferred_element_type=F32)


def _dot_nt(a, b):
    return lax.dot_general(a.astype(BF16), b.astype(BF16), (((1,), (1,)), ((), ())), preferred_element_type=F32)


def _dot_tn(a, b):
    return lax.dot_general(a.astype(BF16), b.astype(BF16), (((0,), (0,)), ((), ())), preferred_element_type=F32)


def _split3(x):
    hi = x.astype(BF16)
    r = x - hi.astype(F32)
    mid = r.astype(BF16)
    lo = (r - mid.astype(F32)).astype(BF16)
    return hi, mid, lo


def _dot_exact_l(ones_bf16, x):
    hi, mid, lo = _split3(x)
    f = lambda p: jnp.dot(ones_bf16, p, preferred_element_type=F32)
    return f(hi) + f(mid) + f(lo)


def _dot_exact_r(x, ones_bf16):
    hi, mid, lo = _split3(x)
    f = lambda p: jnp.dot(p, ones_bf16, preferred_element_type=F32)
    return f(hi) + f(mid) + f(lo)


def _rms(x, g):
    return x * lax.rsqrt(jnp.mean(x * x, axis=-1, keepdims=True) + RMS_EPS) * g


def _sigmoid(x):
    return 1.0 / (1.0 + jnp.exp(-x))


def _softplus(x):
    return jnp.maximum(x, 0.0) + jnp.log(1.0 + jnp.exp(-jnp.abs(x)))


def _tri(q, lower):
    r = lax.broadcasted_iota(jnp.int32, (q, q), 0)
    c = lax.broadcasted_iota(jnp.int32, (q, q), 1)
    return (r >= c) if lower else (r <= c)


def _inproj_kernel(x_ref, g_ref, wm_ref, ws_ref, wst_ref, oa_ref, ob_ref, oc_ref, od_ref, os_ref, ost_ref):
    for rs in _sub_tiles(x_ref.shape[0]):
        h = _rms(x_ref[rs, :], g_ref[...]).astype(BF16)
        for o_ref, (lo, hi) in ((oa_ref, MAIN_SSD), (ob_ref, MAIN_SWA), (oc_ref, MAIN_S5), (od_ref, MAIN_ML)):
            o_ref[rs, :] = jnp.dot(h, wm_ref[:, lo:hi], preferred_element_type=F32)
        os_ref[rs, :] = jnp.dot(h, ws_ref[...], preferred_element_type=F32)
        ost_ref[0, :, rs] = lax.dot_general(wst_ref[...], h, (((1,), (1,)), ((), ())), preferred_element_type=F32)


def _inproj(x, wts, l):
    ta = x.shape[0]
    tm = TOKEN_TILE
    widths = [hi - lo for lo, hi in (MAIN_SSD, MAIN_SWA, MAIN_S5, MAIN_ML)] + [SMALL_COLS]
    row = lambda n: pl.BlockSpec((tm, n), lambda i: (i, 0))
    ws = (wts["norm_mix"], wts["w_main"], wts["w_small"], wts["w_small_t"])
    mixer_cols = pl.BlockSpec((None, D_MODEL, MAIN_COLS), lambda i: (l, 0, 0), pipeline_mode=pl.Buffered(1))
    return pl.pallas_call(
        _inproj_kernel,
        grid=(ta // tm,),
        in_specs=[row(D_MODEL), _layer(ws[0], l), mixer_cols, _layer(ws[2], l), _layer(ws[3], l)],
        out_specs=[row(n) for n in widths] + [pl.BlockSpec((1, SMALL_ROWS, tm), lambda i: (0, 0, i))],
        out_shape=[jax.ShapeDtypeStruct((ta, n), F32) for n in widths]
        + [jax.ShapeDtypeStruct((1, SMALL_ROWS, ta), F32)],
        compiler_params=_params(1),
        name="inproj",
    )(x, *ws)


def _ssd_kernel(p_ref, sm_ref, smt_ref, cinit_ref, sinit_ref, cw_ref, cb_ref, dtb_c_ref, dtb_r_ref,
                alog_c_ref, alog_r_ref, dskip_ref, g_ref, y_ref, sout_ref, ext_sc, st_sc, *, q, nsub, valid):
    c = pl.program_id(1)
    rows = q * nsub

    @pl.when(c == 0)
    def _():
        ext_sc[0:8, :] = cinit_ref[...]
        st_sc[...] = sinit_ref[...].reshape(st_sc.shape)

    ext_sc[8:8 + rows, :] = p_ref[:, SSD_D_INNER:SSD_SEG]
    conv = cb_ref[...] + cw_ref[0:1, :] * ext_sc[5:5 + rows, :]
    for j in range(1, SSD_CONV):
        conv = conv + cw_ref[j:j + 1, :] * ext_sc[5 + j:5 + j + rows, :]
    ext_sc[0:8, :] = ext_sc[rows:rows + 8, :]
    xbc_all = conv * _sigmoid(conv)
    z_all = p_ref[:, :SSD_D_INNER]

    dt_c_all = _softplus(sm_ref[:, 0:SSD_HEADS] + dtb_c_ref[...])
    dt_r_all = _softplus(smt_ref[0, 0:SSD_HEADS, :] + dtb_r_ref[...])
    if valid < q:
        dt_c_all = jnp.where(lax.broadcasted_iota(jnp.int32, dt_c_all.shape, 0) < valid, dt_c_all, 0.0)
        dt_r_all = jnp.where(lax.broadcasted_iota(jnp.int32, dt_r_all.shape, 1) < valid, dt_r_all, 0.0)
    da_c_all = dt_c_all * (-jnp.exp(alog_c_ref[...]))
    da_r_all = dt_r_all * (-jnp.exp(alog_r_ref[...]))
    causal = _tri(q, True)
    lower = jnp.where(causal, 1.0, 0.0).astype(BF16)
    upper = jnp.where(_tri(q, False), 1.0, 0.0).astype(BF16)

    rep = SSD_HEADS // SSD_GROUPS
    c_off = SSD_D_INNER + SSD_GROUPS * SSD_STATE
    hd = SSD_HEAD_DIM
    states = [st_sc[g] for g in range(SSD_GROUPS)]
    for j in range(nsub):
        rs = slice(j * q, (j + 1) * q)
        xbc, dt_c, dt_r = xbc_all[rs], dt_c_all[rs], dt_r_all[:, rs]
        xs = xbc[:, :SSD_D_INNER]
        cum_c = _dot_exact_l(lower, da_c_all[rs])
        cum_r = _dot_exact_r(da_r_all[:, rs], upper)
        groups = []
        for g in range(SSD_GROUPS):
            b_g = xbc[:, SSD_D_INNER + g * SSD_STATE:SSD_D_INNER + (g + 1) * SSD_STATE]
            c_g = xbc[:, c_off + g * SSD_STATE:c_off + (g + 1) * SSD_STATE]
            groups.append((b_g, _dot_nt(c_g, b_g), _dot_nt(c_g, states[g])))
        ws = []
        for h in range(SSD_HEADS):
            seg = jnp.where(causal, cum_c[:, h:h + 1] - cum_r[h:h + 1, :], MASKED)
            ws.append(groups[h // rep][1] * jnp.exp(seg) * dt_r[h:h + 1, :])
        ys = []
        for h in range(SSD_HEADS):
            hl = h % rep
            x_h = xs[:, h * hd:(h + 1) * hd]
            ys.append(_dot(ws[h], x_h) + jnp.exp(cum_c[:, h:h + 1]) * groups[h // rep][2][:, hl * hd:(hl + 1) * hd])
        z = z_all[rs]
        y = (jnp.concatenate(ys, axis=1) + dskip_ref[...] * xs) * (z * _sigmoid(z))
        y_ref[rs, :] = _rms(y, g_ref[...])
        for g in range(SSD_GROUPS):
            scaled, decay = [], []
            for h in range(g * rep, (g + 1) * rep):
                last = cum_c[q - 1:q, h:h + 1]
                tail = jnp.exp(last - cum_c[:, h:h + 1]) * dt_c[:, h:h + 1]
                scaled.append(xs[:, h * hd:(h + 1) * hd] * tail)
                decay.append(jnp.broadcast_to(jnp.exp(last), (hd, SSD_STATE)))
            states[g] = (states[g] * jnp.concatenate(decay, axis=0)
                         + _dot_tn(jnp.concatenate(scaled, axis=1), groups[g][0]))
    for g in range(SSD_GROUPS):
        st_sc[g] = states[g]

    @pl.when(c == pl.num_programs(1) - 1)
    def _():
        sout_ref[0] = st_sc[...].reshape(SSD_HEADS, SSD_HEAD_DIM, SSD_STATE)


def _seq_specs(rows_blk, nc, blk0, p_small_t):
    rows = lambda n: pl.BlockSpec((rows_blk, n), lambda b, c: (blk0 + b * nc + c, 0))
    if p_small_t.shape[0] == 1:
        smt_spec = pl.BlockSpec((1, SMALL_ROWS, rows_blk), lambda b, c: (0, 0, blk0 + b * nc + c))
    else:
        smt_spec = pl.BlockSpec((1, SMALL_ROWS, rows_blk), lambda b, c: (b, 0, 0))
    return rows, smt_spec


def _state_spec(a, sl):
    nd = a.ndim
    return pl.BlockSpec((None, None) + a.shape[2:], lambda b, c: (sl, b) + (0,) * (nd - 2))


def _ssd_call(p_ssd, p_small, p_small_t, conv_init, state_init, sl, wts, l, *, nb, nc, q, nsub, valid, row0, prev):
    ta = p_ssd.shape[0]
    blk = q * nsub
    assert row0 % blk == 0 and (valid == q or nsub == 1)
    rows, smt_spec = _seq_specs(blk, nc, row0 // blk, p_small_t)
    ws = tuple(wts[k] for k in ("ssd_conv_w", "ssd_conv_b", "dtb_c", "dtb_r", "alog_c", "alog_r", "ssd_dskip",
                                "ssd_norm"))
    in_specs = [rows(SSD_SEG), rows(SMALL_COLS), smt_spec, _state_spec(conv_init, sl), _state_spec(state_init, sl)]
    in_specs += [_layer(w, l) for w in ws]
    args = [p_ssd, p_small, p_small_t, conv_init, state_init, *ws]
    kern, aliases = _with_aliased(functools.partial(_ssd_kernel, q=q, nsub=nsub, valid=valid), in_specs, args,
                                  [] if prev is None else [(prev, 0)])
    return pl.pallas_call(
        kern,
        grid=(nb, nc),
        in_specs=in_specs,
        out_specs=[rows(SSD_D_INNER),
                   pl.BlockSpec((1, SSD_HEADS, SSD_HEAD_DIM, SSD_STATE), lambda b, c: (b, 0, 0, 0))],
        out_shape=[jax.ShapeDtypeStruct((ta, SSD_D_INNER), F32),
                   jax.ShapeDtypeStruct((nb, SSD_HEADS, SSD_HEAD_DIM, SSD_STATE), F32)],
        scratch_shapes=[pltpu.VMEM((blk + 8, SSD_CONV_DIM), F32),
                        pltpu.VMEM((SSD_GROUPS, SSD_HEADS // SSD_GROUPS * SSD_HEAD_DIM, SSD_STATE), F32)],
        input_output_aliases=aliases,
        compiler_params=_params(2),
        name="ssd_q%d" % q,
    )(*args)


def _mlstm_kernel(p_ref, sm_ref, smt_ref, cinit_ref, ninit_ref, minit_ref, ib_c_ref, ib_r_ref, fb_c_ref, fb_r_ref,
                  g_ref, y_ref, cout_ref, nout_ref, mout_ref, c_sc, n_sc, m_sc, *, q, nsub, valid):
    c = pl.program_id(1)

    @pl.when(c == 0)
    def _():
        c_sc[...] = cinit_ref[...]
        n_sc[...] = ninit_ref[...]
        m_sc[...] = minit_ref[...]

    w = ML_WIDTH
    i0 = SSD_HEADS
    f0 = SSD_HEADS + ML_HEADS
    ig_c_all = sm_ref[:, i0:i0 + ML_HEADS] + ib_c_ref[...]
    fg_c = sm_ref[:, f0:f0 + ML_HEADS] + fb_c_ref[...]
    ig_r_all = smt_ref[0, i0:i0 + ML_HEADS, :] + ib_r_ref[...]
    fg_r = smt_ref[0, f0:f0 + ML_HEADS, :] + fb_r_ref[...]
    lf_c_all = -_softplus(-fg_c)
    lf_r_all = -_softplus(-fg_r)
    if valid < q:
        ok_c = lax.broadcasted_iota(jnp.int32, ig_c_all.shape, 0) < valid
        ok_r = lax.broadcasted_iota(jnp.int32, ig_r_all.shape, 1) < valid
        ig_c_all = jnp.where(ok_c, ig_c_all, MASKED)
        ig_r_all = jnp.where(ok_r, ig_r_all, MASKED)
        lf_c_all = jnp.where(ok_c, lf_c_all, 0.0)
        lf_r_all = jnp.where(ok_r, lf_r_all, 0.0)
    causal = _tri(q, True)
    lower = jnp.where(causal, 1.0, 0.0).astype(BF16)
    upper = jnp.where(_tri(q, False), 1.0, 0.0).astype(BF16)

    scale = ML_HEAD_DIM ** -0.5
    c_st = [c_sc[h] for h in range(ML_HEADS)]
    n_st = [n_sc[h] for h in range(ML_HEADS)]
    m_st = [m_sc[h][:, 0:1] for h in range(ML_HEADS)]
    heads = range(ML_HEADS)
    col = lambda ref_rows, base, h: p_ref[ref_rows, base + h * ML_HEAD_DIM:base + (h + 1) * ML_HEAD_DIM]
    pre = []
    for j in range(nsub):
        rs = slice(j * q, (j + 1) * q)
        b_c = _dot_exact_l(lower, lf_c_all[rs])
        b_r = _dot_exact_r(lf_r_all[:, rs], upper)
        ig_r = ig_r_all[:, rs]
        per_head = []
        for h in heads:
            bc = b_c[:, h:h + 1]
            intra = jnp.where(causal, bc - b_r[h:h + 1, :] + ig_r[h:h + 1, :], MASKED)
            qk = _dot_nt(col(rs, 0, h), col(rs, w, h) * scale)
            per_head.append((bc, intra, jnp.max(intra, axis=1, keepdims=True), qk))
        pre.append((rs, b_c, per_head))
    for j, (rs, b_c, per_head) in enumerate(pre):
        ig_c = ig_c_all[rs]
        mid = []
        for h in heads:
            bc, intra, row_max, qk = per_head[h]
            q_h, v_h = col(rs, 0, h), col(rs, 2 * w, h)
            inter = bc + m_st[h]
            m_t = jnp.maximum(inter, row_max)
            wgt = qk * jnp.exp(intra - m_t)
            w_inter = jnp.exp(inter - m_t)
            num = _dot(wgt, v_h) + w_inter * _dot(q_h, c_st[h])
            den = jnp.sum(wgt, axis=1, keepdims=True) + w_inter * jnp.sum(q_h * n_st[h], axis=1, keepdims=True)
            mid.append((m_t, num, den))
        last = []
        for h in heads:
            m_t, num, den = mid[h]
            hh = num / jnp.maximum(jnp.abs(den), jnp.exp(-m_t))
            last.append((hh, jnp.mean(hh * hh, axis=-1, keepdims=True)))
        ys = []
        for h in heads:
            hs = slice(h * ML_HEAD_DIM, (h + 1) * ML_HEAD_DIM)
            bc, m_t = per_head[h][0], mid[h][0]
            hh, ms = last[h]
            ys.append(_sigmoid(col(rs, 3 * w, h)) * (hh * lax.rsqrt(ms + RMS_EPS) * g_ref[:, hs]))
            m_new = m_t[q - 1:q, :]
            b_last = b_c[q - 1:q, h:h + 1]
            wk = jnp.exp(b_last - bc + ig_c[:, h:h + 1] - m_new)
            decay = jnp.exp(b_last + m_st[h] - m_new)
            kw = col(rs, w, h) * scale * wk
            c_st[h] = decay * c_st[h] + _dot_tn(kw, col(rs, 2 * w, h))
            n_st[h] = decay * n_st[h] + jnp.sum(kw, axis=0, keepdims=True)
            m_st[h] = m_new
        y_ref[rs, :] = jnp.concatenate(ys, axis=1)
    for h in range(ML_HEADS):
        c_sc[h] = c_st[h]
        n_sc[h] = n_st[h]
        m_sc[h] = jnp.broadcast_to(m_st[h], (1, ML_HEAD_DIM))

    @pl.when(c == pl.num_programs(1) - 1)
    def _():
        cout_ref[0] = c_sc[...]
        nout_ref[0] = n_sc[...]
        mout_ref[0] = m_sc[...]


def _mlstm_call(p_ml, p_small, p_small_t, c_init, n_init, m_init, sl, wts, l, *, nb, nc, q, nsub, valid, row0,
                prev):
    ta = p_ml.shape[0]
    blk = q * nsub
    assert row0 % blk == 0 and (valid == q or nsub == 1)
    rows, smt_spec = _seq_specs(blk, nc, row0 // blk, p_small_t)
    ws = tuple(wts[k] for k in ("ib_c", "ib_r", "fb_c", "fb_r", "mlstm_norm"))
    c_spec = pl.BlockSpec((1, ML_HEADS, ML_HEAD_DIM, ML_HEAD_DIM), lambda b, c: (b, 0, 0, 0))
    v_spec = pl.BlockSpec((1, ML_HEADS, 1, ML_HEAD_DIM), lambda b, c: (b, 0, 0, 0))
    in_specs = [rows(4 * ML_WIDTH), rows(SMALL_COLS), smt_spec, _state_spec(c_init, sl), _state_spec(n_init, sl),
                _state_spec(m_init, sl)]
    in_specs += [_layer(w, l) for w in ws]
    args = [p_ml, p_small, p_small_t, c_init, n_init, m_init, *ws]
    kern, aliases = _with_aliased(functools.partial(_mlstm_kernel, q=q, nsub=nsub, valid=valid), in_specs, args,
                                  [] if prev is None else [(prev, 0)])
    vec = jax.ShapeDtypeStruct((nb, ML_HEADS, 1, ML_HEAD_DIM), F32)
    return pl.pallas_call(
        kern,
        grid=(nb, nc),
        in_specs=in_specs,
        out_specs=[rows(ML_WIDTH), c_spec, v_spec, v_spec],
        out_shape=[jax.ShapeDtypeStruct((ta, ML_WIDTH), F32),
                   jax.ShapeDtypeStruct((nb, ML_HEADS, ML_HEAD_DIM, ML_HEAD_DIM), F32), vec, vec],
        scratch_shapes=[pltpu.VMEM((ML_HEADS, ML_HEAD_DIM, ML_HEAD_DIM), F32),
                        pltpu.VMEM((ML_HEADS, 1, ML_HEAD_DIM), F32),
                        pltpu.VMEM((ML_HEADS, 1, ML_HEAD_DIM), F32)],
        input_output_aliases=aliases,
        compiler_params=_params(2),
        name="mlstm_q%d" % q,
    )(*args)


def _s5_kernel(u_ref, h0_ref, btc_ref, smc_ref, cmc_ref, e_ref, pa_ref, pb_ref, va_ref, vb_ref,
               y_ref, hout_ref, tz_ref, sm_ref, cm_ref, *, rows, scan):
    half = S5_SUPER_STATE // 2
    q, lanes, ng = S5_CHUNK, S5_LANES, S5_SUPER_GROUPS

    @pl.when(pl.program_id(1) == 0)
    def _():
        def expand(compact):
            hi = compact.astype(BF16)
            lo = (compact - hi.astype(F32)).astype(BF16)
            full = (jnp.dot(hi, e_ref[...], preferred_element_type=F32)
                    + jnp.dot(lo, e_ref[...], preferred_element_type=F32))
            rg = (lax.broadcasted_iota(jnp.int32, full.shape, 0) >> 4) & (ng - 1)
            cg = (lax.broadcasted_iota(jnp.int32, full.shape, 1) >> 6) & (ng - 1)
            return jnp.where(rg == cg, full, 0.0)

        def split(x):
            hi = x.astype(BF16)
            return hi, (x - hi.astype(F32)).astype(BF16)

        for r0 in range(0, q * lanes, 4 * lanes):
            sm_ref[r0:r0 + 4 * lanes, :] = expand(smc_ref[r0:r0 + 4 * lanes, :]).astype(BF16)
        bt_hi, bt_lo = split(expand(btc_ref[...]))
        nt = lambda a, b: lax.dot_general(a, b, (((1,), (1,)), ((), ())), preferred_element_type=F32)
        zero = jnp.zeros((lanes, lanes), BF16)
        per = 4
        for lag0 in range(0, q + 1, per):
            nl = min(per, q + 1 - lag0)
            blk = expand(cmc_ref[lag0 * lanes:(lag0 + nl) * lanes, :])
            cm_ref[lag0 * lanes:(lag0 + nl) * lanes, :] = blk.astype(BF16)
            if lag0 >= q:
                continue
            c_hi, c_lo = split(blk)
            k_all = (nt(bt_hi, c_hi) + nt(bt_hi, c_lo) + nt(bt_lo, c_hi)).astype(BF16)
            for lag in range(lag0, min(lag0 + nl, q)):
                k_lag = k_all[:, (lag - lag0) * lanes:(lag - lag0 + 1) * lanes]
                for s in range(q - lag):
                    t = s + lag
                    tz_ref[s * lanes:(s + 1) * lanes, t * lanes:(t + 1) * lanes] = k_lag
                    if lag > 0:
                        tz_ref[t * lanes:(t + 1) * lanes, s * lanes:(s + 1) * lanes] = zero

    ucat = jnp.concatenate([u_ref[pl.ds(s, rows, stride=S5_CHUNK), :] for s in range(S5_CHUNK)], axis=1)
    ub = ucat.astype(BF16)

    def cmul(a, b, x):
        return a * x + b * pltpu.roll(x, half, axis=1)

    contrib = jnp.dot(ub, sm_ref[...], preferred_element_type=F32)
    h0 = h0_ref[0]
    carried = cmul(va_ref[...], vb_ref[...], h0)
    if scan:
        ridx = lax.broadcasted_iota(jnp.int32, (rows, S5_SUPER_STATE), 0)
        x = contrib + jnp.where(ridx == 0, carried, 0.0)
        k = 0
        while (1 << k) < rows:
            s = 1 << k
            shifted = jnp.where(ridx >= s, pltpu.roll(x, s, axis=0), 0.0)
            x = x + cmul(pa_ref[k:k + 1, :], pb_ref[k:k + 1, :], shifted)
            k += 1
        hprev = jnp.where(ridx == 0, h0, pltpu.roll(x, 1, axis=0))
        hout_ref[0] = x[rows - 1:rows, :]
    else:
        x = contrib + carried
        hprev = h0
        hout_ref[0] = x
    y = jnp.dot(ub, tz_ref[...], preferred_element_type=F32) + _dot_nt(hprev, cm_ref[lanes:, :])
    for t in range(S5_CHUNK):
        y_ref[pl.ds(t, rows, stride=S5_CHUNK), :] = y[:, t * S5_LANES:(t + 1) * S5_LANES]


def _s5_call(p_s5, h0, tab, l, *, nseq, rows, scan, row0, prev):
    ta = p_s5.shape[0]
    blk_rows = rows * S5_CHUNK
    blk0 = row0 // blk_rows
    hrows = h0.shape[2]
    tok = pl.BlockSpec((blk_rows, S5_LANES), lambda sb, b: (blk0 + b, sb))
    hspec = pl.BlockSpec((None, 1, hrows, S5_SUPER_STATE), lambda sb, b: (sb, b, 0, 0))
    va, vb = (tab["va16"], tab["vb16"]) if scan else (tab["va"], tab["vb"])
    per_sb = lambda a: pl.BlockSpec((None, None) + a.shape[2:], lambda sb, b: (l, sb) + (0,) * (a.ndim - 2))
    tabs = [(tab["btc"], per_sb), (tab["smc16"] if scan else tab["smc"], per_sb), (tab["cmc"], per_sb),
            (tab["e"], lambda a: _full(a.shape)),
            (tab["pa"], per_sb), (tab["pb"], per_sb), (va, per_sb), (vb, per_sb)]
    in_specs = [tok, hspec] + [mk(a) for a, mk in tabs]
    args = [p_s5, h0] + [a for a, _ in tabs]
    kern, aliases = _with_aliased(functools.partial(_s5_kernel, rows=rows, scan=scan), in_specs, args,
                                  [] if prev is None else [(prev, 0)])
    folded = S5_CHUNK * S5_LANES
    return pl.pallas_call(
        kern,
        grid=(S5_SUPER, nseq),
        in_specs=in_specs,
        out_specs=[tok, hspec],
        out_shape=[jax.ShapeDtypeStruct((ta, S5_WIDTH), F32), jax.ShapeDtypeStruct(h0.shape, F32)],
        scratch_shapes=[pltpu.VMEM((folded, folded), BF16), pltpu.VMEM((folded, S5_SUPER_STATE), BF16),
                        pltpu.VMEM((folded + S5_LANES, S5_SUPER_STATE), BF16)],
        input_output_aliases=aliases,
        compiler_params=_params(2),
        name="s5_scan" if scan else "s5_step",
    )(*args)


def _s5_tables(a_re, a_im, log_dt, b_re, b_im, c_re, c_im, valid):
    q = S5_CHUNK
    depth = a_re.shape[0]
    nsb, ng = S5_SUPER, S5_SUPER_GROUPS
    dt = jnp.exp(log_dt)[..., None]
    mag = jnp.exp(a_re * dt)
    ab_re = mag * jnp.cos(a_im * dt)
    ab_im = mag * jnp.sin(a_im * dt)
    inv = 1.0 / (a_re * a_re + a_im * a_im)
    co_re = ((ab_re - 1.0) * a_re + ab_im * a_im) * inv
    co_im = (ab_im * a_re - (ab_re - 1.0) * a_im) * inv

    def cmul(x, y):
        return x[0] * y[0] - x[1] * y[1], x[0] * y[1] + x[1] * y[0]

    pw = [(jnp.ones_like(ab_re), jnp.zeros_like(ab_re))]
    for _ in range(q):
        pw.append(cmul(pw[-1], (ab_re, ab_im)))
    row = lambda v: v[:, :, None, :]
    bt_t = (jnp.transpose(b_re, (0, 1, 3, 2)), jnp.transpose(b_im, (0, 1, 3, 2)))
    bt_re = row(co_re) * bt_t[0] - row(co_im) * bt_t[1]
    bt_im = row(co_re) * bt_t[1] + row(co_im) * bt_t[0]

    def compact(blocks):
        x = jnp.stack(blocks, axis=2)
        x = x.reshape(depth, nsb, ng, len(blocks), S5_GROUP, 2 * S5_STATE)
        return jnp.transpose(x, (0, 1, 3, 2, 4, 5)).reshape(depth, nsb, len(blocks) * S5_LANES, 2 * S5_STATE)

    def times(p):
        pr, pi = row(p[0]), row(p[1])
        b = jnp.concatenate([pr * bt_re - pi * bt_im, pr * bt_im + pi * bt_re], axis=-1)
        c = jnp.concatenate([c_re * pr - c_im * pi, -(c_re * pi + c_im * pr)], axis=-1)
        return b, c

    def state_in(nvalid):
        zero = jnp.zeros((depth, S5_GROUPS, S5_GROUP, 2 * S5_STATE), F32)
        return compact([times(pw[nvalid - 1 - s])[0] if s < nvalid else zero for s in range(q)])

    btc = compact([times(pw[0])[0]])
    cmc = compact([times(p)[1] for p in pw])
    e = np.zeros((2, S5_STATE, 2, ng, S5_STATE), np.float32)
    for g in range(ng):
        e[:, :, :, g, :] = np.eye(2 * S5_STATE).reshape(2, S5_STATE, 2, S5_STATE)
    e = jnp.asarray(e.reshape(2 * S5_STATE, S5_SUPER_STATE), BF16)

    def packed(p):
        pr = p[0].reshape(depth, nsb, ng * S5_STATE)
        pi = p[1].reshape(depth, nsb, ng * S5_STATE)
        return jnp.concatenate([pr, pr], axis=-1), jnp.concatenate([-pi, pi], axis=-1)

    doubling = [pw[q]]
    for _ in range(S5_DOUBLINGS - 1):
        doubling.append(cmul(doubling[-1], doubling[-1]))
    pa = jnp.stack([packed(p)[0] for p in doubling], axis=2)
    pb = jnp.stack([packed(p)[1] for p in doubling], axis=2)
    va16, vb16 = packed(pw[q])
    va, vb = packed(pw[valid])
    ex = lambda a: a[:, :, None, :]
    return {"btc": btc, "smc16": state_in(q), "smc": state_in(valid), "cmc": cmc, "e": e,
            "pa": pa, "pb": pb, "va16": ex(va16), "vb16": ex(vb16), "va": ex(va), "vb": ex(vb)}


def _t5_bucket(dist):
    dist = np.asarray(dist)
    large = REL_MAX_EXACT + (np.log(np.maximum(dist, 1) / REL_MAX_EXACT)
                             / math.log(REL_MAX_DIST / REL_MAX_EXACT)
                             * (REL_BUCKETS - REL_MAX_EXACT)).astype(np.int32)
    large = np.minimum(large, REL_BUCKETS - 1)
    return np.where(dist < REL_MAX_EXACT, dist, large).astype(np.int32)


def _bias_steps(rel_bias, grp, dil):
    heads = slice(grp * SWA_GROUP_HEADS, (grp + 1) * SWA_GROUP_HEADS)
    buckets = _t5_bucket(dil * np.arange(SWA_BLOCK + 1))
    onehot = np.zeros((SWA_BLOCK + 1, REL_BUCKETS), np.float32)
    onehot[np.arange(SWA_BLOCK + 1), buckets] = 1.0
    steps = jnp.einsum("jb,bh->hj", onehot, rel_bias[:, heads], precision=lax.Precision.HIGHEST)
    return steps.astype(F32)


def _softmax_pieces(logits):
    m = logits[0].max(axis=1, keepdims=True)
    for s in logits[1:]:
        m = jnp.maximum(m, s.max(axis=1, keepdims=True))
    ps = [jnp.exp(s - m) for s in logits]
    den = ps[0].sum(axis=1, keepdims=True)
    for p in ps[1:]:
        den = den + p.sum(axis=1, keepdims=True)
    return ps, den, m + jnp.log(den)


SWA_SAMPLE_LANES = 2048
SWA_UNROLL = 8
SWA_DENSE_BLOCKS = 8


def _swa_prompt_kernel(q_ref, k_ref, v_ref, kp_ref, vp_ref, bias_ref, o_ref, lse_ref, kvt_ref, *, dil):
    n = SWA_BLOCK
    pair = pl.program_id(1)
    first = pl.program_id(2) == 0
    scale = SWA_HEAD_DIM ** -0.5

    ones = jnp.ones((n, SWA_HEAD_DIM), BF16)

    def logits(sl, kp, vp, mask_prev):
        qq, kk = q_ref[sl, :], k_ref[sl, :]
        heads = []
        for j in range(2):
            hs = slice(j * SWA_HEAD_DIM, (j + 1) * SWA_HEAD_DIM)
            bias = bias_ref[2 * pair + j]
            bias_prev = bias[:, 0:n] if mask_prev is None else jnp.where(mask_prev, MASKED, bias[:, 0:n])
            q_h = qq[:, hs] * scale
            s_cur = _dot_nt(q_h, kk[:, hs]) + bias[:, n:2 * n]
            s_prev = _dot_nt(q_h, kp[:, hs]) + bias_prev
            heads.append((s_cur, s_prev, jnp.max(jnp.maximum(s_cur, s_prev), axis=1, keepdims=True)))
        return sl, vp, heads

    def finish(sl, vp, heads):
        vv = v_ref[sl, :]
        outs, lses = [], []
        for j, (s_cur, s_prev, m) in enumerate(heads):
            hs = slice(j * SWA_HEAD_DIM, (j + 1) * SWA_HEAD_DIM)
            p_cur = jnp.exp(s_cur - m).astype(BF16)
            p_prev = jnp.exp(s_prev - m).astype(BF16)
            den = jnp.dot(p_cur, ones, preferred_element_type=F32) + jnp.dot(p_prev, ones, preferred_element_type=F32)
            outs.append((_dot(p_cur, vv[:, hs]) + _dot(p_prev, vp[:, hs])) / den)
            lses.append(m + jnp.log(den))
        o_ref[sl, :] = jnp.concatenate(outs, axis=1)
        lse_ref[sl, :] = jnp.concatenate(lses, axis=1)

    def run(tiles):
        for t in [logits(*a) for a in tiles]:
            finish(*t)

    if dil == 1:
        tiles = []
        for j in range(SWA_DENSE_BLOCKS):
            if j == 0:
                tiles.append((slice(0, n), kp_ref[...], vp_ref[...], first))
            else:
                before = slice((j - 1) * n, j * n)
                tiles.append((slice(j * n, (j + 1) * n), k_ref[before, :], v_ref[before, :], None))
        run(tiles)
    else:
        unroll = min(dil, SWA_UNROLL)

        def group(i, carry):
            tiles = []
            for u in range(unroll):
                sl = pl.ds(i * unroll + u, n, stride=dil)
                tiles.append((sl, kp_ref[sl, :], vp_ref[sl, :], first))
            run(tiles)
            return carry
        if dil == unroll:
            group(0, 0)
        else:
            lax.fori_loop(0, dil // unroll, group, 0)
    rows = k_ref.shape[0]
    kvt_ref[0] = k_ref[rows - n * dil:rows, :].T
    kvt_ref[1] = v_ref[rows - n * dil:rows, :].T


def _swa_prompt_call(p_swa, bias, kvt_prev, l, depth, *, grp, dil, nb, seq):
    ta = p_swa.shape[0]
    win = SWA_BLOCK * dil
    sb = SWA_BLOCK * SWA_DENSE_BLOCKS if dil == 1 else win
    assert dil == 1 or dil % min(dil, SWA_UNROLL) == 0
    nsb = seq // sb
    lanes = 2 * SWA_HEAD_DIM
    npair = SWA_GROUP_HEADS // 2
    cur = lambda col: pl.BlockSpec((sb, lanes), lambda b, p, c: (b * nsb + c, 2 * col + p))
    if dil == 1:
        per = sb // SWA_BLOCK
        prv = lambda col: pl.BlockSpec(
            (SWA_BLOCK, lanes), lambda b, p, c: (jnp.maximum((b * nsb + c) * per - 1, 0), 2 * col + p))
    else:
        prv = lambda col: pl.BlockSpec((sb, lanes), lambda b, p, c: (b * nsb + jnp.maximum(c - 1, 0), 2 * col + p))
    out = pl.BlockSpec((sb, lanes), lambda b, p, c: (b * nsb + c, p))
    kvt_spec = pl.BlockSpec((None, None, 2, None, lanes, win), lambda b, p, c: (l, b, 0, p, 0, 0))
    shape = jax.ShapeDtypeStruct((ta, SWA_GROUP_WIDTH), F32)
    in_specs = [cur(grp), cur(3 + grp), cur(6 + grp), prv(3 + grp), prv(6 + grp), _full(bias.shape)]
    args = [p_swa, p_swa, p_swa, p_swa, p_swa, bias]
    kern, aliases = _with_aliased(functools.partial(_swa_prompt_kernel, dil=dil), in_specs, args,
                                  [] if kvt_prev is None else [(kvt_prev, 2)])
    return pl.pallas_call(
        kern,
        grid=(nb, npair, nsb),
        in_specs=in_specs,
        out_specs=[out, out, kvt_spec],
        out_shape=[shape, shape, jax.ShapeDtypeStruct((depth, nb, 2, npair, lanes, win), F32)],
        input_output_aliases=aliases,
        compiler_params=_params(3),
        name="swa_prompt_d%d" % dil,
    )(*args)


def _swa_prompt_bias(steps):
    n = SWA_BLOCK
    period = 3 * n + 1
    f = jnp.concatenate([steps[:, ::-1], jnp.full((steps.shape[0], period - (n + 1)), MASKED, F32)], axis=1)
    tiled = jnp.tile(f, (1, n))[:, :n * (period - 1)]
    return tiled.reshape(steps.shape[0], n, period - 1)[:, :, :2 * n]


def _swa_sample_kernel(q_ref, k_ref, v_ref, buf_ref, bias_buf_ref, bias_new_ref, o_ref, lse_ref, cache_ref, *,
                       width, t_new):
    scale = SWA_HEAD_DIM ** -0.5
    w = SWA_GROUP_WIDTH
    for i in range(buf_ref.shape[0]):
        rs = slice(i * SAMPLE_ROWS, (i + 1) * SAMPLE_ROWS)
        qq, kn, vn = q_ref[rs, :], k_ref[rs, :], v_ref[rs, :]
        outs, lses = [], []
        for h in range(SWA_GROUP_HEADS):
            hs = slice(h * SWA_HEAD_DIM, (h + 1) * SWA_HEAD_DIM)
            k_t = buf_ref[i, h * SWA_HEAD_DIM:(h + 1) * SWA_HEAD_DIM, :]
            v_t = buf_ref[i, w + h * SWA_HEAD_DIM:w + (h + 1) * SWA_HEAD_DIM, :]
            q_h = qq[:, hs] * scale
            ps, den, lse = _softmax_pieces([_dot(q_h, k_t) + bias_buf_ref[h],
                                            _dot_nt(q_h, kn[:, hs]) + bias_new_ref[h]])
            outs.append((_dot_nt(ps[0], v_t) + _dot(ps[1], vn[:, hs])) / den)
            lses.append(jnp.broadcast_to(lse, (SAMPLE_ROWS, SWA_HEAD_DIM)))
        o_ref[rs, :] = jnp.concatenate(outs, axis=1)
        lse_ref[rs, :] = jnp.concatenate(lses, axis=1)
        new_t = jnp.concatenate([kn, vn], axis=1).T
        cache_ref[i] = pltpu.roll(buf_ref[i], width - t_new, axis=1)
        cache_ref[i, :, width - t_new:width] = new_t[:, 0:t_new]


def _swa_sample_call(p_swa, cache_t, bias_buf, bias_new, prev_o, prev_lse, prev_cache, l, *, grp, nb, row0, t_new):
    ta = p_swa.shape[0]
    width = cache_t.shape[3]
    w = SWA_GROUP_WIDTH
    per = max(1, min(nb, SWA_SAMPLE_LANES // width))
    while nb % per:
        per -= 1
    assert row0 % (SAMPLE_ROWS * per) == 0
    blk0 = row0 // (SAMPLE_ROWS * per)
    nb = nb // per
    tok = lambda col: pl.BlockSpec((SAMPLE_ROWS * per, w), lambda b: (blk0 + b, col))
    cache_spec = pl.BlockSpec((None, per, 2 * w, width), lambda b: (l, b, 0, 0))
    in_specs = [tok(grp), tok(3 + grp), tok(6 + grp), cache_spec, _full(bias_buf.shape), _full(bias_new.shape)]
    args = [p_swa, p_swa, p_swa, cache_t, bias_buf, bias_new]
    aliased = [(prev_o, 0), (prev_lse, 1)] + ([] if prev_cache is None else [(prev_cache, 2)])
    kern, aliases = _with_aliased(functools.partial(_swa_sample_kernel, width=width, t_new=t_new), in_specs, args,
                                  aliased)
    return pl.pallas_call(
        kern,
        grid=(nb,),
        in_specs=in_specs,
        out_specs=[tok(0), tok(0), cache_spec],
        out_shape=[jax.ShapeDtypeStruct((ta, w), F32), jax.ShapeDtypeStruct((ta, w), F32),
                   jax.ShapeDtypeStruct(cache_t.shape, F32)],
        input_output_aliases=aliases,
        compiler_params=_params(1),
        name="swa_sample_w%d" % width,
    )(*args)


def _swa_sample_bias(steps, window, dil, width, t_new):
    n = window // dil
    nh = steps.shape[0]
    g = jnp.pad(steps[:, :, None], ((0, 0), (0, 0), (0, dil - 1)), constant_values=MASKED).reshape(nh, (n + 1) * dil)
    g = jnp.pad(g, ((0, 0), (0, width + SAMPLE_ROWS)), constant_values=MASKED)
    masked_row = jnp.full((nh, 1, width), MASKED, F32)
    rows = [g[:, t + 1:t + 1 + width][:, None, ::-1] if t < t_new else masked_row for t in range(SAMPLE_ROWS)]
    b_buf = jnp.concatenate(rows, axis=1)
    new_rows = []
    for t in range(SAMPLE_ROWS):
        if t < t_new:
            row = jnp.concatenate([g[:, 0:t + 1][:, ::-1], jnp.full((nh, SAMPLE_ROWS - t - 1), MASKED, F32)], axis=1)
        else:
            row = jnp.full((nh, SAMPLE_ROWS), MASKED, F32)
        new_rows.append(row[:, None, :])
    return b_buf, jnp.concatenate(new_rows, axis=1)


def _merge_kernel(x_ref, yssd_ref, o0_ref, l0_ref, o1_ref, l1_ref, o2_ref, l2_ref, ys5_ref, u_ref, yml_ref,
                  g_ref, wg01_ref, wg23_ref, wssd_ref, wswa_ref, ws5_ref, wml_ref, wglu_ref, wout_ref, d_ref, out_ref):
    for rs in _sub_tiles(x_ref.shape[0]):
        x = x_ref[rs, :]
        h = _rms(x, g_ref[...]).astype(BF16)
        l0, l1, l2 = l0_ref[rs, :], l1_ref[rs, :], l2_ref[rs, :]
        m = jnp.maximum(jnp.maximum(l0, l1), l2)
        e0, e1, e2 = jnp.exp(l0 - m), jnp.exp(l1 - m), jnp.exp(l2 - m)
        y_swa = (e0 * o0_ref[rs, :] + e1 * o1_ref[rs, :] + e2 * o2_ref[rs, :]) / (e0 + e1 + e2)
        y5 = ys5_ref[rs, :] + d_ref[...] * u_ref[rs, :]
        y_s5 = y5 * _sigmoid(_dot(y5, wglu_ref[...]))
        branches = ((yssd_ref[rs, :], wssd_ref), (y_swa, wswa_ref), (y_s5, ws5_ref), (yml_ref[rs, :], wml_ref))
        merged = None
        for i, (y, w_ref) in enumerate(branches):
            wg_ref = wg01_ref if i < 2 else wg23_ref
            gate = _sigmoid(jnp.dot(h, wg_ref[:, (i % 2) * D_MODEL:(i % 2 + 1) * D_MODEL],
                                    preferred_element_type=F32))
            term = gate * _dot(y, w_ref[...])
            merged = term if merged is None else merged + term
        out_ref[rs, :] = x + _dot(merged, wout_ref[...])


def _merge_call(x, acts, wts, l):
    ta = x.shape[0]
    tm = TOKEN_TILE
    row = lambda a: pl.BlockSpec((tm, a.shape[1]), lambda i: (i, 0))
    ws = tuple(wts[k] for k in ("w_br_ssd", "w_br_swa", "w_br_s5", "w_br_mlstm", "s5_w_glu", "w_out", "s5_d"))
    half = GATE_COLS // 2
    gate = lambda k: pl.BlockSpec((None, D_MODEL, half), lambda i: (l, 0, MAIN_COLS // half + k),
                                  pipeline_mode=pl.Buffered(1))
    w_all = wts["w_main"]
    return pl.pallas_call(
        _merge_kernel,
        grid=(ta // tm,),
        in_specs=[row(a) for a in (x, *acts)] + [_layer(wts["norm_mix"], l), gate(0), gate(1)]
        + [_layer(w, l) for w in ws],
        out_specs=row(x),
        out_shape=jax.ShapeDtypeStruct(x.shape, F32),
        compiler_params=_params(1),
        name="merge",
    )(x, *acts, wts["norm_mix"], w_all, w_all, *ws)


def _norm_matmul_kernel(x_ref, g_ref, w_ref, o_ref):
    o_ref[...] = jnp.dot(_rms(x_ref[...], g_ref[...]).astype(BF16), w_ref[...], preferred_element_type=F32)


def _norm_matmul(x, g, w, l, tm):
    rows = x.shape[0]
    return pl.pallas_call(
        _norm_matmul_kernel,
        grid=(rows // tm,),
        in_specs=[pl.BlockSpec((tm, x.shape[1]), lambda i: (i, 0)), _layer(g, l), _layer(w, l)],
        out_specs=pl.BlockSpec((tm, w.shape[2]), lambda i: (i, 0)),
        out_shape=jax.ShapeDtypeStruct((rows, w.shape[2]), F32),
        compiler_params=_params(1),
        name="norm_matmul",
    )(x, g, w)


def _xattn_heads(q, kv_ref, kv_rows):
    scale = XA_HEAD_DIM ** -0.5
    n_mem = kv_ref.shape[0] // (2 * XA_HEADS) if kv_rows else kv_ref.shape[0]
    outs = []
    for h in range(XA_HEADS):
        hs = slice(h * XA_HEAD_DIM, (h + 1) * XA_HEAD_DIM)
        if kv_rows:
            k_h = kv_ref[h * n_mem:(h + 1) * n_mem, :]
            v_h = kv_ref[(XA_HEADS + h) * n_mem:(XA_HEADS + h + 1) * n_mem, :]
        else:
            k_h = kv_ref[:, hs]
            v_h = kv_ref[:, D_MODEL + h * XA_HEAD_DIM:D_MODEL + (h + 1) * XA_HEAD_DIM]
        logits = _dot_nt(q[:, hs], k_h) * scale
        m = logits.max(axis=1, keepdims=True)
        p = jnp.exp(logits - m)
        outs.append(_dot(p / p.sum(axis=1, keepdims=True), v_h))
    return jnp.concatenate(outs, axis=1)


def _xattn_prompt_kernel(x_ref, kv_ref, g_ref, wq_ref, wo_ref, out_ref):
    for rs in _sub_tiles(x_ref.shape[0]):
        x = x_ref[rs, :]
        q = jnp.dot(_rms(x, g_ref[...]).astype(BF16), wq_ref[...], preferred_element_type=F32)
        out_ref[rs, :] = x + _dot(_xattn_heads(q, kv_ref, False), wo_ref[...])


def _xattn_prompt_call(x, kv, wts, l, *, n_tiles, tiles_per_seq):
    ta = x.shape[0]
    tm = TOKEN_TILE
    row = pl.BlockSpec((tm, D_MODEL), lambda i: (i, 0))
    ws = tuple(wts[k] for k in ("norm_xa", "xa_wq", "xa_wo"))
    return pl.pallas_call(
        _xattn_prompt_kernel,
        grid=(n_tiles,),
        in_specs=[row, pl.BlockSpec((None,) + kv.shape[1:], lambda i: (i // tiles_per_seq, 0, 0))]
        + [_layer(w, l) for w in ws],
        out_specs=row,
        out_shape=jax.ShapeDtypeStruct((ta, D_MODEL), F32),
        compiler_params=_params(1),
        name="xattn_prompt",
    )(x, kv, *ws)


def _xattn_sample_kernel(x_ref, kv_ref, g_ref, wq_ref, wo_ref, out_ref, q_sc, o_sc):
    b = pl.program_id(0)

    @pl.when(b == 0)
    def _():
        q_sc[...] = jnp.dot(_rms(x_ref[...], g_ref[...]).astype(BF16), wq_ref[...], preferred_element_type=F32)

    rows = pl.ds(pl.multiple_of(b * SAMPLE_ROWS, SAMPLE_ROWS), SAMPLE_ROWS)
    o_sc[rows, :] = _xattn_heads(q_sc[rows, :], kv_ref, True)

    @pl.when(b == pl.num_programs(0) - 1)
    def _():
        out_ref[...] = x_ref[...] + _dot(o_sc[...], wo_ref[...])


def _xattn_sample_call(x, kv, wts, l, *, nb, row0, prev):
    ta = x.shape[0]
    n_s = nb * SAMPLE_ROWS
    rows = pl.BlockSpec((n_s, D_MODEL), lambda b: (row0 // n_s, 0))
    ws = tuple(wts[k] for k in ("norm_xa", "xa_wq", "xa_wo"))
    in_specs = [rows, pl.BlockSpec((None, None) + kv.shape[2:], lambda b: (l, b, 0, 0))] + [_layer(w, l) for w in ws]
    args = [x, kv, *ws]
    kern, aliases = _with_aliased(_xattn_sample_kernel, in_specs, args, [(prev, 0)])
    return pl.pallas_call(
        kern,
        grid=(nb,),
        in_specs=in_specs,
        out_specs=rows,
        out_shape=jax.ShapeDtypeStruct((ta, D_MODEL), F32),
        scratch_shapes=[pltpu.VMEM((n_s, D_MODEL), F32), pltpu.VMEM((n_s, D_MODEL), F32)],
        input_output_aliases=aliases,
        compiler_params=_params(1),
        name="xattn_sample",
    )(*args)


def _ffn_kernel(x_ref, g_ref, w1_ref, w2_ref, out_ref):
    for rs in _sub_tiles(x_ref.shape[0]):
        x = x_ref[rs, :]
        a = jnp.dot(_rms(x, g_ref[...]).astype(BF16), w1_ref[...], preferred_element_type=F32)
        a = jnp.square(jnp.maximum(a, 0.0))
        out_ref[rs, :] = x + _dot(a, w2_ref[...])


def _ffn_call(x, wts, l):
    ta = x.shape[0]
    tm = TOKEN_TILE
    row = pl.BlockSpec((tm, D_MODEL), lambda i: (i, 0))
    ws = tuple(wts[k] for k in ("norm_mlp", "w_ff1", "w_ff2"))
    return pl.pallas_call(
        _ffn_kernel,
        grid=(ta // tm,),
        in_specs=[row] + [_layer(w, l) for w in ws],
        out_specs=row,
        out_shape=jax.ShapeDtypeStruct(x.shape, F32),
        compiler_params=_params(1),
        name="ffn",
    )(x, *ws)


def _ffn_final_kernel(x_ref, g_ref, w1_ref, w2_ref, gf_ref, yp_ref, ys_ref, *, prompt_tiles):
    i = pl.program_id(0)
    ys = []
    for rs in _sub_tiles(x_ref.shape[0]):
        x = x_ref[rs, :]
        a = jnp.dot(_rms(x, g_ref[...]).astype(BF16), w1_ref[...], preferred_element_type=F32)
        a = jnp.square(jnp.maximum(a, 0.0))
        ys.append(_rms(x + _dot(a, w2_ref[...]), gf_ref[...]))
    y = jnp.concatenate(ys, axis=0)

    @pl.when(i < prompt_tiles)
    def _():
        yp_ref[...] = y

    @pl.when(i >= prompt_tiles)
    def _():
        ys_ref[...] = y


def _ffn_final_call(x, wts, l, g_final, n_p):
    ta = x.shape[0]
    tm = TOKEN_TILE
    n_pt = n_p // tm
    assert n_p % tm == 0 and ta - n_p == tm
    row = pl.BlockSpec((tm, D_MODEL), lambda i: (i, 0))
    ws = tuple(wts[k] for k in ("norm_mlp", "w_ff1", "w_ff2"))
    return pl.pallas_call(
        functools.partial(_ffn_final_kernel, prompt_tiles=n_pt),
        grid=(ta // tm,),
        in_specs=[row] + [_layer(w, l) for w in ws] + [_full(g_final.shape)],
        out_specs=[pl.BlockSpec((tm, D_MODEL), lambda i: (jnp.minimum(i, n_pt - 1), 0)),
                   pl.BlockSpec((tm, D_MODEL), lambda i: (0, 0))],
        out_shape=[jax.ShapeDtypeStruct((n_p, D_MODEL), F32), jax.ShapeDtypeStruct((tm, D_MODEL), F32)],
        compiler_params=_params(1),
        name="ffn_final",
    )(x, *ws, g_final)


def _prep_weights(w):
    o = IN_OFFS
    w_in = w["w_in"]
    small = jnp.concatenate([w_in[:, :, o[2]:o[3]], w_in[:, :, o[11]:o[13]]], axis=2)
    row = lambda v: v[:, None, :].astype(F32)
    col = lambda v: v[:, :, None].astype(F32)
    bf = lambda v: v.astype(BF16)
    return {
        "norm_mix": row(w["norm_mix"]),
        "w_main": jnp.concatenate([w_in[:, :, o[0]:o[2]], w_in[:, :, o[3]:o[11]], w_in[:, :, o[13]:o[14]]],
                                  axis=2).astype(BF16),
        "w_small": jnp.pad(small, ((0, 0), (0, 0), (0, SMALL_COLS - small.shape[2]))).astype(BF16),
        "w_small_t": jnp.transpose(small, (0, 2, 1)).astype(BF16),
        "ssd_conv_w": w["ssd_conv_w"], "ssd_conv_b": row(w["ssd_conv_b"]),
        "dtb_c": row(w["ssd_dt_bias"]), "dtb_r": col(w["ssd_dt_bias"]),
        "alog_c": row(w["ssd_a_log"]), "alog_r": col(w["ssd_a_log"]),
        "ssd_dskip": row(jnp.repeat(w["ssd_d"], SSD_HEAD_DIM, axis=1)), "ssd_norm": row(w["ssd_norm"]),
        "ib_c": row(w["mlstm_i_bias"]), "ib_r": col(w["mlstm_i_bias"]),
        "fb_c": row(w["mlstm_f_bias"]), "fb_r": col(w["mlstm_f_bias"]),
        "mlstm_norm": row(w["mlstm_norm"]),
        "s5_d": row(w["s5_d"]), "s5_w_glu": bf(w["s5_w_glu"]),
        "w_br_ssd": bf(w["w_br_ssd"]), "w_br_swa": bf(w["w_br_swa"]), "w_br_s5": bf(w["w_br_s5"]),
        "w_br_mlstm": bf(w["w_br_mlstm"]), "w_out": bf(w["w_out"]),
        "norm_xa": row(w["norm_xa"]), "xa_wq": bf(w["xa_wq"]), "xa_wo": bf(w["xa_wo"]),
        "norm_mem": row(w["norm_mem"]),
        "xa_wkv": jnp.concatenate([w["xa_wk"], w["xa_wv"]], axis=2).astype(BF16),
        "norm_mlp": row(w["norm_mlp"]), "w_ff1": bf(w["w_ff1"]), "w_ff2": bf(w["w_ff2"]),
    }


def kernel(x_prompt, x_sample, state_ssd, state_ssd_conv, cache_swa_w128, cache_swa_w512, cache_swa_w2048, state_s5, state_mlstm_c, state_mlstm_n, state_mlstm_m, cache_mem_kv, mem_prompt, norm_mix, w_in, ssd_conv_w, ssd_conv_b, ssd_dt_bias, ssd_a_log, ssd_d, ssd_norm, rel_bias, s5_a_re, s5_a_im, s5_log_dt, s5_b_re, s5_b_im, s5_c_re, s5_c_im, s5_d, s5_w_glu, mlstm_i_bias, mlstm_f_bias, mlstm_norm, w_br_ssd, w_br_swa, w_br_s5, w_br_mlstm, w_out, norm_xa, norm_mem, xa_wq, xa_wk, xa_wv, xa_wo, norm_mlp, w_ff1, w_ff2, norm_final):
    wts = _prep_weights(dict(
        norm_mix=norm_mix, w_in=w_in, ssd_conv_w=ssd_conv_w, ssd_conv_b=ssd_conv_b, ssd_dt_bias=ssd_dt_bias,
        ssd_a_log=ssd_a_log, ssd_d=ssd_d, ssd_norm=ssd_norm, s5_d=s5_d, s5_w_glu=s5_w_glu,
        mlstm_i_bias=mlstm_i_bias, mlstm_f_bias=mlstm_f_bias, mlstm_norm=mlstm_norm, w_br_ssd=w_br_ssd,
        w_br_swa=w_br_swa, w_br_s5=w_br_s5, w_br_mlstm=w_br_mlstm, w_out=w_out, norm_xa=norm_xa, norm_mem=norm_mem,
        xa_wq=xa_wq, xa_wk=xa_wk, xa_wv=xa_wv, xa_wo=xa_wo, norm_mlp=norm_mlp, w_ff1=w_ff1, w_ff2=w_ff2))
    nb_p, seq, _ = x_prompt.shape
    nb_s, t_new, _ = x_sample.shape
    depth = w_in.shape[0]
    n_mem = mem_prompt.shape[1]
    rows_s = SAMPLE_ROWS
    n_p = nb_p * seq
    n_s = nb_s * rows_s
    ta = n_p + n_s
    assert seq % (SWA_BLOCK * SWA_PATTERN[-1][1]) == 0 and ta % TOKEN_TILE == 0 and t_new <= rows_s
    assert n_p % n_s == 0 and (seq // S5_CHUNK) & (seq // S5_CHUNK - 1) == 0
    caches = (cache_swa_w128, cache_swa_w512, cache_swa_w2048)

    xs_pad = jnp.pad(x_sample, ((0, 0), (0, rows_s - t_new), (0, 0))).reshape(n_s, D_MODEL)
    x = jnp.concatenate([x_prompt.reshape(n_p, D_MODEL), xs_pad], axis=0)

    steps = [_bias_steps(rel_bias, g, dil) for g, (_, dil) in enumerate(SWA_PATTERN)]
    prompt_bias = [_swa_prompt_bias(s) for s in steps]
    sample_bias = [_swa_sample_bias(steps[g], win, dil, caches[g].shape[2], t_new)
                   for g, (win, dil) in enumerate(SWA_PATTERN)]
    caches_t = [jnp.transpose(c, (0, 1, 3, 4, 5, 2)).reshape(depth, nb_s, 2 * SWA_GROUP_WIDTH, c.shape[2])
                for c in caches]
    mem_kv_s = jnp.transpose(cache_mem_kv, (0, 1, 3, 4, 2, 5)).reshape(depth, nb_s, 2 * XA_HEADS * n_mem, XA_HEAD_DIM)
    s5_tab = _s5_tables(s5_a_re, s5_a_im, s5_log_dt, s5_b_re, s5_b_im, s5_c_re, s5_c_im, t_new)
    s5_h0_s = jnp.transpose(state_s5.reshape(depth, nb_s, S5_SUPER, S5_SUPER_GROUPS, S5_STATE, 2),
                            (0, 2, 1, 5, 3, 4)).reshape(depth, S5_SUPER, 1, nb_s, S5_SUPER_STATE)

    zeros = lambda *s: jnp.zeros(s, F32)
    zero_conv = zeros(1, nb_p, 8, SSD_CONV_DIM)
    zero_ssd = zeros(1, nb_p, SSD_HEADS, SSD_HEAD_DIM, SSD_STATE)
    zero_c = zeros(1, nb_p, ML_HEADS, ML_HEAD_DIM, ML_HEAD_DIM)
    zero_vec = zeros(1, nb_p, ML_HEADS, 1, ML_HEAD_DIM)
    zero_s5 = zeros(S5_SUPER, nb_p, 1, S5_SUPER_STATE)
    conv_init_s = jnp.pad(state_ssd_conv, ((0, 0), (0, 0), (8 - (SSD_CONV - 1), 0), (0, 0)))
    n_init_s = state_mlstm_n[:, :, :, None, :]
    m_init_s = jnp.broadcast_to(state_mlstm_m[:, :, :, None, None], (depth, nb_s, ML_HEADS, 1, ML_HEAD_DIM))

    outs = {k: [] for k in ("ssd_p", "ssd_s", "conv_p", "conv_s", "s5_p", "s5_s", "c_p", "c_s", "n_p", "n_s",
                            "m_p", "m_s", "kv_p")}
    kvt_p = [None] * len(SWA_PATTERN)
    cache_out = [None] * len(SWA_PATTERN)
    ssd_q, ml_q, nsub = 128, 128, 4
    nchunk = seq // S5_CHUNK

    for l in range(depth):
        p_ssd, p_swa, p_s5, p_ml, p_small, p_small_t = _inproj(x, wts, l)
        small_t_s = jnp.transpose(p_small[n_p:, :SMALL_ROWS].reshape(nb_s, rows_s, SMALL_ROWS), (0, 2, 1))

        y_ssd, st_p = _ssd_call(p_ssd, p_small, p_small_t, zero_conv, zero_ssd, 0, wts, l, nb=nb_p,
                                nc=seq // (ssd_q * nsub), q=ssd_q, nsub=nsub, valid=ssd_q, row0=0, prev=None)
        y_ssd, st_s = _ssd_call(p_ssd, p_small, small_t_s, conv_init_s, state_ssd, l, wts, l,
                                nb=nb_s, nc=1, q=rows_s, nsub=1, valid=t_new, row0=n_p, prev=y_ssd)
        outs["ssd_p"].append(st_p)
        outs["ssd_s"].append(st_s)
        tail = SSD_CONV - 1
        xbc_s = p_ssd[n_p:, SSD_D_INNER:].reshape(nb_s, rows_s, SSD_CONV_DIM)[:, :t_new]
        outs["conv_p"].append(jnp.stack([p_ssd[(b + 1) * seq - tail:(b + 1) * seq, SSD_D_INNER:]
                                         for b in range(nb_p)]))
        outs["conv_s"].append(jnp.concatenate([state_ssd_conv[l], xbc_s], axis=1)[:, -(SSD_CONV - 1):])

        swa_acts = []
        for g, (win, dil) in enumerate(SWA_PATTERN):
            o_g, lse_g, kvt_p[g] = _swa_prompt_call(p_swa, prompt_bias[g], kvt_p[g], l, depth, grp=g, dil=dil,
                                                    nb=nb_p, seq=seq)
            o_g, lse_g, cache_out[g] = _swa_sample_call(p_swa, caches_t[g], sample_bias[g][0], sample_bias[g][1],
                                                        o_g, lse_g, cache_out[g], l, grp=g, nb=nb_s, row0=n_p,
                                                        t_new=t_new)
            swa_acts += [o_g, lse_g]

        y_s5, h_p = _s5_call(p_s5, zero_s5, s5_tab, l, nseq=nb_p, rows=nchunk, scan=True, row0=0, prev=None)
        y_s5, h_s = _s5_call(p_s5, s5_h0_s[l], s5_tab, l, nseq=1, rows=nb_s, scan=False, row0=n_p, prev=y_s5)
        outs["s5_p"].append(h_p)
        outs["s5_s"].append(h_s)

        y_ml, c_p, nn_p, m_p = _mlstm_call(p_ml, p_small, p_small_t, zero_c, zero_vec, zero_vec, 0, wts, l, nb=nb_p,
                                           nc=seq // (ml_q * nsub), q=ml_q, nsub=nsub, valid=ml_q, row0=0,
                                           prev=None)
        y_ml, c_s, nn_s, m_s = _mlstm_call(p_ml, p_small, small_t_s, state_mlstm_c, n_init_s, m_init_s, l, wts, l,
                                           nb=nb_s, nc=1, q=rows_s, nsub=1, valid=t_new, row0=n_p, prev=y_ml)
        for key, val in (("c_p", c_p), ("c_s", c_s), ("n_p", nn_p[:, :, 0]), ("n_s", nn_s[:, :, 0]),
                         ("m_p", m_p[:, :, 0, 0]), ("m_s", m_s[:, :, 0, 0])):
            outs[key].append(val)

        x = _merge_call(x, (y_ssd, *swa_acts, y_s5, p_s5, y_ml), wts, l)

        kv_p = _norm_matmul(mem_prompt.reshape(nb_p * n_mem, D_MODEL), wts["norm_mem"], wts["xa_wkv"], l, n_mem)
        outs["kv_p"].append(kv_p.reshape(nb_p, n_mem, 2, XA_HEADS, XA_HEAD_DIM))
        x_new = _xattn_prompt_call(x, kv_p.reshape(nb_p, n_mem, 2 * D_MODEL), wts, l, n_tiles=n_p // TOKEN_TILE,
                                   tiles_per_seq=seq // TOKEN_TILE)
        x = _xattn_sample_call(x, mem_kv_s, wts, l, nb=nb_s, row0=n_p, prev=x_new)

        if l + 1 < depth:
            x = _ffn_call(x, wts, l)
        else:
            y_p, y_s = _ffn_final_call(x, wts, l, norm_final[None, :], n_p)

    y_prompt = y_p.reshape(nb_p, seq, D_MODEL)
    y_sample = y_s.reshape(nb_s, rows_s, D_MODEL)[:, :t_new]
    st = lambda k: jnp.stack(outs[k])

    def s5_state(k, nb):
        h = st(k).reshape(depth, S5_SUPER, nb, 2, S5_SUPER_GROUPS, S5_STATE)
        return jnp.transpose(h, (0, 2, 1, 4, 5, 3)).reshape(depth, nb, S5_GROUPS, S5_STATE, 2)

    swa_out = []
    for g in range(len(SWA_PATTERN)):
        width_p = kvt_p[g].shape[-1]
        kp = kvt_p[g].reshape(depth, nb_p, 2, SWA_GROUP_HEADS, SWA_HEAD_DIM, width_p)
        swa_out.append(jnp.transpose(kp, (0, 1, 5, 2, 3, 4)))
        cs = cache_out[g].reshape(depth, nb_s, 2, SWA_GROUP_HEADS, SWA_HEAD_DIM, caches[g].shape[2])
        swa_out.append(jnp.transpose(cs, (0, 1, 5, 2, 3, 4)))
    return (y_prompt, y_sample, st("ssd_p"), st("ssd_s"), st("conv_p"), st("conv_s"), *swa_out,
            s5_state("s5_p", nb_p), s5_state("s5_s", nb_s), st("c_p"), st("c_s"),
            st("n_p"), st("n_s"), st("m_p"), st("m_s"), st("kv_p"))
```

```python
import functools
import math

import numpy as np
import jax
import jax.numpy as jnp
from jax import lax
from jax.experimental import pallas as pl
from jax.experimental.pallas import tpu as pltpu

F32 = jnp.float32
BF16 = jnp.bfloat16

D_MODEL = 1024
RMS_EPS = 1e-6
N_BRANCH = 4

SSD_D_INNER = 512
SSD_HEAD_DIM = 64
SSD_HEADS = 8
SSD_GROUPS = 2
SSD_STATE = 64
SSD_CONV = 4
SSD_CONV_DIM = SSD_D_INNER + 2 * SSD_GROUPS * SSD_STATE
SSD_SEG = SSD_D_INNER + SSD_CONV_DIM

SWA_PATTERN = ((128, 1), (512, 4), (2048, 16))
SWA_GROUP_HEADS = 4
SWA_HEAD_DIM = 64
SWA_GROUP_WIDTH = SWA_GROUP_HEADS * SWA_HEAD_DIM
SWA_WIDTH = 3 * SWA_GROUP_WIDTH
SWA_BLOCK = 128
REL_BUCKETS = 32
REL_MAX_EXACT = 16
REL_MAX_DIST = 2048

S5_WIDTH = 512
S5_GROUP = 16
S5_GROUPS = 32
S5_STATE = 64
S5_CHUNK = 16
S5_LANES = 128
S5_SUPER = S5_WIDTH // S5_LANES
S5_SUPER_GROUPS = S5_GROUPS // S5_SUPER
S5_SUPER_STATE = 2 * S5_SUPER_GROUPS * S5_STATE
S5_DOUBLINGS = 12

ML_WIDTH = 512
ML_HEADS = 4
ML_HEAD_DIM = 128

XA_HEADS = 4
XA_HEAD_DIM = 256
D_FF = 4096

MAIN_SSD = (0, SSD_SEG)
MAIN_SWA = (SSD_SEG, SSD_SEG + 3 * SWA_WIDTH)
MAIN_S5 = (MAIN_SWA[1], MAIN_SWA[1] + S5_WIDTH)
MAIN_ML = (MAIN_S5[1], MAIN_S5[1] + 4 * ML_WIDTH)
MAIN_COLS = MAIN_ML[1]
GATE_COLS = N_BRANCH * D_MODEL
SMALL_COLS = 128
SMALL_ROWS = 16

SAMPLE_ROWS = 16
TOKEN_TILE = 512
SUB_TILE = 256
MASKED = -1e30

VMEM_LIMIT = 56 * 1024 * 1024

IN_SIZES = (SSD_D_INNER, SSD_CONV_DIM, SSD_HEADS, SWA_WIDTH, SWA_WIDTH, SWA_WIDTH, S5_WIDTH,
            ML_WIDTH, ML_WIDTH, ML_WIDTH, ML_WIDTH, ML_HEADS, ML_HEADS, N_BRANCH * D_MODEL)
IN_OFFS = tuple(int(v) for v in np.concatenate([[0], np.cumsum(IN_SIZES)]))


def _params(n_axes):
    return pltpu.CompilerParams(dimension_semantics=("arbitrary",) * n_axes, vmem_limit_bytes=VMEM_LIMIT)


def _full(shape):
    nd = len(shape)
    return pl.BlockSpec(shape, lambda *_: (0,) * nd)


def _layer(a, l):
    nd = a.ndim
    return pl.BlockSpec((None,) + a.shape[1:], lambda *_: (l,) + (0,) * (nd - 1), pipeline_mode=pl.Buffered(1))


def _sub_tiles(rows):
    step = min(SUB_TILE, rows)
    return [slice(r, r + step) for r in range(0, rows, step)]


def _with_aliased(kern, in_specs, args, aliased):
    n = len(args)
    aliases = {}
    for arr, out_idx in aliased:
        in_specs.append(pl.BlockSpec(memory_space=pl.ANY))
        aliases[len(args)] = out_idx
        args.append(arr)
    k = len(aliased)
    if k == 0:
        return kern, aliases

    def wrapped(*refs):
        return kern(*refs[:n], *refs[n + k:])
    return wrapped, aliases


def _dot(a, b):
    return jnp.dot(a.astype(BF16), b.astype(BF16), preferred_element_type=F32)


def _dot_nt(a, b):
    return lax.dot_general(a.astype(BF16), b.astype(BF16), (((1,), (1,)), ((), ())), preferred_element_type=F32)


def _dot_tn(a, b):
    return lax.dot_general(a.astype(BF16), b.astype(BF16), (((0,), (0,)), ((), ())), preferred_element_type=F32)


def _split3(x):
    hi = x.astype(BF16)
    r = x - hi.astype(F32)
    mid = r.astype(BF16)
    lo = (r - mid.astype(F32)).astype(BF16)
    return hi, mid, lo


def _dot_exact_l(ones_bf16, x):
    hi, mid, lo = _split3(x)
    f = lambda p: jnp.dot(ones_bf16, p, preferred_element_type=F32)
    return f(hi) + f(mid) + f(lo)


def _dot_exact_r(x, ones_bf16):
    hi, mid, lo = _split3(x)
    f = lambda p: jnp.dot(p, ones_bf16, preferred_element_type=F32)
    return f(hi) + f(mid) + f(lo)


def _rms(x, g):
    return x * lax.rsqrt(jnp.mean(x * x, axis=-1, keepdims=True) + RMS_EPS) * g


def _sigmoid(x):
    return 1.0 / (1.0 + jnp.exp(-x))


def _softplus(x):
    return jnp.maximum(x, 0.0) + jnp.log(1.0 + jnp.exp(-jnp.abs(x)))


def _tri(q, lower):
    r = lax.broadcasted_iota(jnp.int32, (q, q), 0)
    c = lax.broadcasted_iota(jnp.int32, (q, q), 1)
    return (r >= c) if lower else (r <= c)


def _inproj_kernel(x_ref, g_ref, wm_ref, ws_ref, wst_ref, oa_ref, ob_ref, oc_ref, od_ref, os_ref, ost_ref):
    for rs in _sub_tiles(x_ref.shape[0]):
        h = _rms(x_ref[rs, :], g_ref[...]).astype(BF16)
        for o_ref, (lo, hi) in ((oa_ref, MAIN_SSD), (ob_ref, MAIN_SWA), (oc_ref, MAIN_S5), (od_ref, MAIN_ML)):
            o_ref[rs, :] = jnp.dot(h, wm_ref[:, lo:hi], preferred_element_type=F32)
        os_ref[rs, :] = jnp.dot(h, ws_ref[...], preferred_element_type=F32)
        ost_ref[0, :, rs] = lax.dot_general(wst_ref[...], h, (((1,), (1,)), ((), ())), preferred_element_type=F32)


def _inproj(x, wts, l):
    ta = x.shape[0]
    tm = TOKEN_TILE
    widths = [hi - lo for lo, hi in (MAIN_SSD, MAIN_SWA, MAIN_S5, MAIN_ML)] + [SMALL_COLS]
    row = lambda n: pl.BlockSpec((tm, n), lambda i: (i, 0))
    ws = (wts["norm_mix"], wts["w_main"], wts["w_small"], wts["w_small_t"])
    mixer_cols = pl.BlockSpec((None, D_MODEL, MAIN_COLS), lambda i: (l, 0, 0), pipeline_mode=pl.Buffered(1))
    return pl.pallas_call(
        _inproj_kernel,
        grid=(ta // tm,),
        in_specs=[row(D_MODEL), _layer(ws[0], l), mixer_cols, _layer(ws[2], l), _layer(ws[3], l)],
        out_specs=[row(n) for n in widths] + [pl.BlockSpec((1, SMALL_ROWS, tm), lambda i: (0, 0, i))],
        out_shape=[jax.ShapeDtypeStruct((ta, n), F32) for n in widths]
        + [jax.ShapeDtypeStruct((1, SMALL_ROWS, ta), F32)],
        compiler_params=_params(1),
        name="inproj",
    )(x, *ws)


def _ssd_kernel(p_ref, sm_ref, smt_ref, cinit_ref, sinit_ref, cw_ref, cb_ref, dtb_c_ref, dtb_r_ref,
                alog_c_ref, alog_r_ref, dskip_ref, g_ref, y_ref, sout_ref, ext_sc, st_sc, *, q, nsub, valid):
    c = pl.program_id(1)
    rows = q * nsub

    @pl.when(c == 0)
    def _():
        ext_sc[0:8, :] = cinit_ref[...]
        st_sc[...] = sinit_ref[...].reshape(st_sc.shape)

    ext_sc[8:8 + rows, :] = p_ref[:, SSD_D_INNER:SSD_SEG]
    conv = cb_ref[...] + cw_ref[0:1, :] * ext_sc[5:5 + rows, :]
    for j in range(1, SSD_CONV):
        conv = conv + cw_ref[j:j + 1, :] * ext_sc[5 + j:5 + j + rows, :]
    ext_sc[0:8, :] = ext_sc[rows:rows + 8, :]
    xbc_all = conv * _sigmoid(conv)
    z_all = p_ref[:, :SSD_D_INNER]

    dt_c_all = _softplus(sm_ref[:, 0:SSD_HEADS] + dtb_c_ref[...])
    dt_r_all = _softplus(smt_ref[0, 0:SSD_HEADS, :] + dtb_r_ref[...])
    if valid < q:
        dt_c_all = jnp.where(lax.broadcasted_iota(jnp.int32, dt_c_all.shape, 0) < valid, dt_c_all, 0.0)
        dt_r_all = jnp.where(lax.broadcasted_iota(jnp.int32, dt_r_all.shape, 1) < valid, dt_r_all, 0.0)
    da_c_all = dt_c_all * (-jnp.exp(alog_c_ref[...]))
    da_r_all = dt_r_all * (-jnp.exp(alog_r_ref[...]))
    causal = _tri(q, True)
    lower = jnp.where(causal, 1.0, 0.0).astype(BF16)
    upper = jnp.where(_tri(q, False), 1.0, 0.0).astype(BF16)

    rep = SSD_HEADS // SSD_GROUPS
    c_off = SSD_D_INNER + SSD_GROUPS * SSD_STATE
    hd = SSD_HEAD_DIM
    states = [st_sc[g] for g in range(SSD_GROUPS)]
    for j in range(nsub):
        rs = slice(j * q, (j + 1) * q)
        xbc, dt_c, dt_r = xbc_all[rs], dt_c_all[rs], dt_r_all[:, rs]
        xs = xbc[:, :SSD_D_INNER]
        cum_c = _dot_exact_l(lower, da_c_all[rs])
        cum_r = _dot_exact_r(da_r_all[:, rs], upper)
        groups = []
        for g in range(SSD_GROUPS):
            b_g = xbc[:, SSD_D_INNER + g * SSD_STATE:SSD_D_INNER + (g + 1) * SSD_STATE]
            c_g = xbc[:, c_off + g * SSD_STATE:c_off + (g + 1) * SSD_STATE]
            groups.append((b_g, _dot_nt(c_g, b_g), _dot_nt(c_g, states[g])))
        ws = []
        for h in range(SSD_HEADS):
            seg = jnp.where(causal, cum_c[:, h:h + 1] - cum_r[h:h + 1, :], MASKED)
            ws.append(groups[h // rep][1] * jnp.exp(seg) * dt_r[h:h + 1, :])
        ys = []
        for h in range(SSD_HEADS):
            hl = h % rep
            x_h = xs[:, h * hd:(h + 1) * hd]
            ys.append(_dot(ws[h], x_h) + jnp.exp(cum_c[:, h:h + 1]) * groups[h // rep][2][:, hl * hd:(hl + 1) * hd])
        z = z_all[rs]
        y = (jnp.concatenate(ys, axis=1) + dskip_ref[...] * xs) * (z * _sigmoid(z))
        y_ref[rs, :] = _rms(y, g_ref[...])
        for g in range(SSD_GROUPS):
            scaled, decay = [], []
            for h in range(g * rep, (g + 1) * rep):
                last = cum_c[q - 1:q, h:h + 1]
                tail = jnp.exp(last - cum_c[:, h:h + 1]) * dt_c[:, h:h + 1]
                scaled.append(xs[:, h * hd:(h + 1) * hd] * tail)
                decay.append(jnp.broadcast_to(jnp.exp(last), (hd, SSD_STATE)))
            states[g] = (states[g] * jnp.concatenate(decay, axis=0)
                         + _dot_tn(jnp.concatenate(scaled, axis=1), groups[g][0]))
    for g in range(SSD_GROUPS):
        st_sc[g] = states[g]

    @pl.when(c == pl.num_programs(1) - 1)
    def _():
        sout_ref[0] = st_sc[...].reshape(SSD_HEADS, SSD_HEAD_DIM, SSD_STATE)


def _seq_specs(rows_blk, nc, blk0, p_small_t):
    rows = lambda n: pl.BlockSpec((rows_blk, n), lambda b, c: (blk0 + b * nc + c, 0))
    if p_small_t.shape[0] == 1:
        smt_spec = pl.BlockSpec((1, SMALL_ROWS, rows_blk), lambda b, c: (0, 0, blk0 + b * nc + c))
    else:
        smt_spec = pl.BlockSpec((1, SMALL_ROWS, rows_blk), lambda b, c: (b, 0, 0))
    return rows, smt_spec


def _state_spec(a, sl):
    nd = a.ndim
    return pl.BlockSpec((None, None) + a.shape[2:], lambda b, c: (sl, b) + (0,) * (nd - 2))


def _ssd_call(p_ssd, p_small, p_small_t, conv_init, state_init, sl, wts, l, *, nb, nc, q, nsub, valid, row0, prev):
    ta = p_ssd.shape[0]
    blk = q * nsub
    assert row0 % blk == 0 and (valid == q or nsub == 1)
    rows, smt_spec = _seq_specs(blk, nc, row0 // blk, p_small_t)
    ws = tuple(wts[k] for k in ("ssd_conv_w", "ssd_conv_b", "dtb_c", "dtb_r", "alog_c", "alog_r", "ssd_dskip",
                                "ssd_norm"))
    in_specs = [rows(SSD_SEG), rows(SMALL_COLS), smt_spec, _state_spec(conv_init, sl), _state_spec(state_init, sl)]
    in_specs += [_layer(w, l) for w in ws]
    args = [p_ssd, p_small, p_small_t, conv_init, state_init, *ws]
    kern, aliases = _with_aliased(functools.partial(_ssd_kernel, q=q, nsub=nsub, valid=valid), in_specs, args,
                                  [] if prev is None else [(prev, 0)])
    return pl.pallas_call(
        kern,
        grid=(nb, nc),
        in_specs=in_specs,
        out_specs=[rows(SSD_D_INNER),
                   pl.BlockSpec((1, SSD_HEADS, SSD_HEAD_DIM, SSD_STATE), lambda b, c: (b, 0, 0, 0))],
        out_shape=[jax.ShapeDtypeStruct((ta, SSD_D_INNER), F32),
                   jax.ShapeDtypeStruct((nb, SSD_HEADS, SSD_HEAD_DIM, SSD_STATE), F32)],
        scratch_shapes=[pltpu.VMEM((blk + 8, SSD_CONV_DIM), F32),
                        pltpu.VMEM((SSD_GROUPS, SSD_HEADS // SSD_GROUPS * SSD_HEAD_DIM, SSD_STATE), F32)],
        input_output_aliases=aliases,
        compiler_params=_params(2),
        name="ssd_q%d" % q,
    )(*args)


def _mlstm_kernel(p_ref, sm_ref, smt_ref, cinit_ref, ninit_ref, minit_ref, ib_c_ref, ib_r_ref, fb_c_ref, fb_r_ref,
                  g_ref, y_ref, cout_ref, nout_ref, mout_ref, c_sc, n_sc, m_sc, *, q, nsub, valid):
    c = pl.program_id(1)

    @pl.when(c == 0)
    def _():
        c_sc[...] = cinit_ref[...]
        n_sc[...] = ninit_ref[...]
        m_sc[...] = minit_ref[...]

    w = ML_WIDTH
    i0 = SSD_HEADS
    f0 = SSD_HEADS + ML_HEADS
    ig_c_all = sm_ref[:, i0:i0 + ML_HEADS] + ib_c_ref[...]
    fg_c = sm_ref[:, f0:f0 + ML_HEADS] + fb_c_ref[...]
    ig_r_all = smt_ref[0, i0:i0 + ML_HEADS, :] + ib_r_ref[...]
    fg_r = smt_ref[0, f0:f0 + ML_HEADS, :] + fb_r_ref[...]
    lf_c_all = -_softplus(-fg_c)
    lf_r_all = -_softplus(-fg_r)
    if valid < q:
        ok_c = lax.broadcasted_iota(jnp.int32, ig_c_all.shape, 0) < valid
        ok_r = lax.broadcasted_iota(jnp.int32, ig_r_all.shape, 1) < valid
        ig_c_all = jnp.where(ok_c, ig_c_all, MASKED)
        ig_r_all = jnp.where(ok_r, ig_r_all, MASKED)
        lf_c_all = jnp.where(ok_c, lf_c_all, 0.0)
        lf_r_all = jnp.where(ok_r, lf_r_all, 0.0)
    causal = _tri(q, True)
    lower = jnp.where(causal, 1.0, 0.0).astype(BF16)
    upper = jnp.where(_tri(q, False), 1.0, 0.0).astype(BF16)

    scale = ML_HEAD_DIM ** -0.5
    c_st = [c_sc[h] for h in range(ML_HEADS)]
    n_st = [n_sc[h] for h in range(ML_HEADS)]
    m_st = [m_sc[h][:, 0:1] for h in range(ML_HEADS)]
    heads = range(ML_HEADS)
    col = lambda ref_rows, base, h: p_ref[ref_rows, base + h * ML_HEAD_DIM:base + (h + 1) * ML_HEAD_DIM]
    pre = []
    for j in range(nsub):
        rs = slice(j * q, (j + 1) * q)
        b_c = _dot_exact_l(lower, lf_c_all[rs])
        b_r = _dot_exact_r(lf_r_all[:, rs], upper)
        ig_r = ig_r_all[:, rs]
        per_head = []
        for h in heads:
            bc = b_c[:, h:h + 1]
            intra = jnp.where(causal, bc - b_r[h:h + 1, :] + ig_r[h:h + 1, :], MASKED)
            qk = _dot_nt(col(rs, 0, h), col(rs, w, h) * scale)
            per_head.append((bc, intra, jnp.max(intra, axis=1, keepdims=True), qk))
        pre.append((rs, b_c, per_head))
    for j, (rs, b_c, per_head) in enumerate(pre):
        ig_c = ig_c_all[rs]
        mid = []
        for h in heads:
            bc, intra, row_max, qk = per_head[h]
            q_h, v_h = col(rs, 0, h), col(rs, 2 * w, h)
            inter = bc + m_st[h]
            m_t = jnp.maximum(inter, row_max)
            wgt = qk * jnp.exp(intra - m_t)
            w_inter = jnp.exp(inter - m_t)
            num = _dot(wgt, v_h) + w_inter * _dot(q_h, c_st[h])
            den = jnp.sum(wgt, axis=1, keepdims=True) + w_inter * jnp.sum(q_h * n_st[h], axis=1, keepdims=True)
            mid.append((m_t, num, den))
        last = []
        for h in heads:
            m_t, num, den = mid[h]
            hh = num / jnp.maximum(jnp.abs(den), jnp.exp(-m_t))
            last.append((hh, jnp.mean(hh * hh, axis=-1, keepdims=True)))
        ys = []
        for h in heads:
            hs = slice(h * ML_HEAD_DIM, (h + 1) * ML_HEAD_DIM)
            bc, m_t = per_head[h][0], mid[h][0]
            hh, ms = last[h]
            ys.append(_sigmoid(col(rs, 3 * w, h)) * (hh * lax.rsqrt(ms + RMS_EPS) * g_ref[:, hs]))
            m_new = m_t[q - 1:q, :]
            b_last = b_c[q - 1:q, h:h + 1]
            wk = jnp.exp(b_last - bc + ig_c[:, h:h + 1] - m_new)
            decay = jnp.exp(b_last + m_st[h] - m_new)
            kw = col(rs, w, h) * scale * wk
            c_st[h] = decay * c_st[h] + _dot_tn(kw, col(rs, 2 * w, h))
            n_st[h] = decay * n_st[h] + jnp.sum(kw, axis=0, keepdims=True)
            m_st[h] = m_new
        y_ref[rs, :] = jnp.concatenate(ys, axis=1)
    for h in range(ML_HEADS):
        c_sc[h] = c_st[h]
        n_sc[h] = n_st[h]
        m_sc[h] = jnp.broadcast_to(m_st[h], (1, ML_HEAD_DIM))

    @pl.when(c == pl.num_programs(1) - 1)
    def _():
        cout_ref[0] = c_sc[...]
        nout_ref[0] = n_sc[...]
        mout_ref[0] = m_sc[...]


def _mlstm_call(p_ml, p_small, p_small_t, c_init, n_init, m_init, sl, wts, l, *, nb, nc, q, nsub, valid, row0,
                prev):
    ta = p_ml.shape[0]
    blk = q * nsub
    assert row0 % blk == 0 and (valid == q or nsub == 1)
    rows, smt_spec = _seq_specs(blk, nc, row0 // blk, p_small_t)
    ws = tuple(wts[k] for k in ("ib_c", "ib_r", "fb_c", "fb_r", "mlstm_norm"))
    c_spec = pl.BlockSpec((1, ML_HEADS, ML_HEAD_DIM, ML_HEAD_DIM), lambda b, c: (b, 0, 0, 0))
    v_spec = pl.BlockSpec((1, ML_HEADS, 1, ML_HEAD_DIM), lambda b, c: (b, 0, 0, 0))
    in_specs = [rows(4 * ML_WIDTH), rows(SMALL_COLS), smt_spec, _state_spec(c_init, sl), _state_spec(n_init, sl),
                _state_spec(m_init, sl)]
    in_specs += [_layer(w, l) for w in ws]
    args = [p_ml, p_small, p_small_t, c_init, n_init, m_init, *ws]
    kern, aliases = _with_aliased(functools.partial(_mlstm_kernel, q=q, nsub=nsub, valid=valid), in_specs, args,
                                  [] if prev is None else [(prev, 0)])
    vec = jax.ShapeDtypeStruct((nb, ML_HEADS, 1, ML_HEAD_DIM), F32)
    return pl.pallas_call(
        kern,
        grid=(nb, nc),
        in_specs=in_specs,
        out_specs=[rows(ML_WIDTH), c_spec, v_spec, v_spec],
        out_shape=[jax.ShapeDtypeStruct((ta, ML_WIDTH), F32),
                   jax.ShapeDtypeStruct((nb, ML_HEADS, ML_HEAD_DIM, ML_HEAD_DIM), F32), vec, vec],
        scratch_shapes=[pltpu.VMEM((ML_HEADS, ML_HEAD_DIM, ML_HEAD_DIM), F32),
                        pltpu.VMEM((ML_HEADS, 1, ML_HEAD_DIM), F32),
                        pltpu.VMEM((ML_HEADS, 1, ML_HEAD_DIM), F32)],
        input_output_aliases=aliases,
        compiler_params=_params(2),
        name="mlstm_q%d" % q,
    )(*args)


def _s5_kernel(u_ref, h0_ref, btc_ref, smc_ref, cmc_ref, e_ref, pa_ref, pb_ref, va_ref, vb_ref,
               y_ref, hout_ref, tz_ref, sm_ref, cm_ref, *, rows, scan):
    half = S5_SUPER_STATE // 2
    q, lanes, ng = S5_CHUNK, S5_LANES, S5_SUPER_GROUPS

    @pl.when(pl.program_id(1) == 0)
    def _():
        def expand(compact):
            hi = compact.astype(BF16)
            lo = (compact - hi.astype(F32)).astype(BF16)
            full = (jnp.dot(hi, e_ref[...], preferred_element_type=F32)
                    + jnp.dot(lo, e_ref[...], preferred_element_type=F32))
            rg = (lax.broadcasted_iota(jnp.int32, full.shape, 0) >> 4) & (ng - 1)
            cg = (lax.broadcasted_iota(jnp.int32, full.shape, 1) >> 6) & (ng - 1)
            return jnp.where(rg == cg, full, 0.0)

        def split(x):
            hi = x.astype(BF16)
            return hi, (x - hi.astype(F32)).astype(BF16)

        for r0 in range(0, q * lanes, 4 * lanes):
            sm_ref[r0:r0 + 4 * lanes, :] = expand(smc_ref[r0:r0 + 4 * lanes, :]).astype(BF16)
        bt_hi, bt_lo = split(expand(btc_ref[...]))
        nt = lambda a, b: lax.dot_general(a, b, (((1,), (1,)), ((), ())), preferred_element_type=F32)
        zero = jnp.zeros((lanes, lanes), BF16)
        per = 4
        for lag0 in range(0, q + 1, per):
            nl = min(per, q + 1 - lag0)
            blk = expand(cmc_ref[lag0 * lanes:(lag0 + nl) * lanes, :])
            cm_ref[lag0 * lanes:(lag0 + nl) * lanes, :] = blk.astype(BF16)
            if lag0 >= q:
                continue
            c_hi, c_lo = split(blk)
            k_all = (nt(bt_hi, c_hi) + nt(bt_hi, c_lo) + nt(bt_lo, c_hi)).astype(BF16)
            for lag in range(lag0, min(lag0 + nl, q)):
                k_lag = k_all[:, (lag - lag0) * lanes:(lag - lag0 + 1) * lanes]
                for s in range(q - lag):
                    t = s + lag
                    tz_ref[s * lanes:(s + 1) * lanes, t * lanes:(t + 1) * lanes] = k_lag
                    if lag > 0:
                        tz_ref[t * lanes:(t + 1) * lanes, s * lanes:(s + 1) * lanes] = zero

    ucat = jnp.concatenate([u_ref[pl.ds(s, rows, stride=S5_CHUNK), :] for s in range(S5_CHUNK)], axis=1)
    ub = ucat.astype(BF16)

    def cmul(a, b, x):
        return a * x + b * pltpu.roll(x, half, axis=1)

    contrib = jnp.dot(ub, sm_ref[...], preferred_element_type=F32)
    h0 = h0_ref[0]
    carried = cmul(va_ref[...], vb_ref[...], h0)
    if scan:
        ridx = lax.broadcasted_iota(jnp.int32, (rows, S5_SUPER_STATE), 0)
        x = contrib + jnp.where(ridx == 0, carried, 0.0)
        k = 0
        while (1 << k) < rows:
            s = 1 << k
            shifted = jnp.where(ridx >= s, pltpu.roll(x, s, axis=0), 0.0)
            x = x + cmul(pa_ref[k:k + 1, :], pb_ref[k:k + 1, :], shifted)
            k += 1
        hprev = jnp.where(ridx == 0, h0, pltpu.roll(x, 1, axis=0))
        hout_ref[0] = x[rows - 1:rows, :]
    else:
        x = contrib + carried
        hprev = h0
        hout_ref[0] = x
    y = jnp.dot(ub, tz_ref[...], preferred_element_type=F32) + _dot_nt(hprev, cm_ref[lanes:, :])
    for t in range(S5_CHUNK):
        y_ref[pl.ds(t, rows, stride=S5_CHUNK), :] = y[:, t * S5_LANES:(t + 1) * S5_LANES]


def _s5_call(p_s5, h0, tab, l, *, nseq, rows, scan, row0, prev):
    ta = p_s5.shape[0]
    blk_rows = rows * S5_CHUNK
    blk0 = row0 // blk_rows
    hrows = h0.shape[2]
    tok = pl.BlockSpec((blk_rows, S5_LANES), lambda sb, b: (blk0 + b, sb))
    hspec = pl.BlockSpec((None, 1, hrows, S5_SUPER_STATE), lambda sb, b: (sb, b, 0, 0))
    va, vb = (tab["va16"], tab["vb16"]) if scan else (tab["va"], tab["vb"])
    per_sb = lambda a: pl.BlockSpec((None, None) + a.shape[2:], lambda sb, b: (l, sb) + (0,) * (a.ndim - 2))
    tabs = [(tab["btc"], per_sb), (tab["smc16"] if scan else tab["smc"], per_sb), (tab["cmc"], per_sb),
            (tab["e"], lambda a: _full(a.shape)),
            (tab["pa"], per_sb), (tab["pb"], per_sb), (va, per_sb), (vb, per_sb)]
    in_specs = [tok, hspec] + [mk(a) for a, mk in tabs]
    args = [p_s5, h0] + [a for a, _ in tabs]
    kern, aliases = _with_aliased(functools.partial(_s5_kernel, rows=rows, scan=scan), in_specs, args,
                                  [] if prev is None else [(prev, 0)])
    folded = S5_CHUNK * S5_LANES
    return pl.pallas_call(
        kern,
        grid=(S5_SUPER, nseq),
        in_specs=in_specs,
        out_specs=[tok, hspec],
        out_shape=[jax.ShapeDtypeStruct((ta, S5_WIDTH), F32), jax.ShapeDtypeStruct(h0.shape, F32)],
        scratch_shapes=[pltpu.VMEM((folded, folded), BF16), pltpu.VMEM((folded, S5_SUPER_STATE), BF16),
                        pltpu.VMEM((folded + S5_LANES, S5_SUPER_STATE), BF16)],
        input_output_aliases=aliases,
        compiler_params=_params(2),
        name="s5_scan" if scan else "s5_step",
    )(*args)


def _s5_tables(a_re, a_im, log_dt, b_re, b_im, c_re, c_im, valid):
    q = S5_CHUNK
    depth = a_re.shape[0]
    nsb, ng = S5_SUPER, S5_SUPER_GROUPS
    dt = jnp.exp(log_dt)[..., None]
    mag = jnp.exp(a_re * dt)
    ab_re = mag * jnp.cos(a_im * dt)
    ab_im = mag * jnp.sin(a_im * dt)
    inv = 1.0 / (a_re * a_re + a_im * a_im)
    co_re = ((ab_re - 1.0) * a_re + ab_im * a_im) * inv
    co_im = (ab_im * a_re - (ab_re - 1.0) * a_im) * inv

    def cmul(x, y):
        return x[0] * y[0] - x[1] * y[1], x[0] * y[1] + x[1] * y[0]

    pw = [(jnp.ones_like(ab_re), jnp.zeros_like(ab_re))]
    for _ in range(q):
        pw.append(cmul(pw[-1], (ab_re, ab_im)))
    row = lambda v: v[:, :, None, :]
    bt_t = (jnp.transpose(b_re, (0, 1, 3, 2)), jnp.transpose(b_im, (0, 1, 3, 2)))
    bt_re = row(co_re) * bt_t[0] - row(co_im) * bt_t[1]
    bt_im = row(co_re) * bt_t[1] + row(co_im) * bt_t[0]

    def compact(blocks):
        x = jnp.stack(blocks, axis=2)
        x = x.reshape(depth, nsb, ng, len(blocks), S5_GROUP, 2 * S5_STATE)
        return jnp.transpose(x, (0, 1, 3, 2, 4, 5)).reshape(depth, nsb, len(blocks) * S5_LANES, 2 * S5_STATE)

    def times(p):
        pr, pi = row(p[0]), row(p[1])
        b = jnp.concatenate([pr * bt_re - pi * bt_im, pr * bt_im + pi * bt_re], axis=-1)
        c = jnp.concatenate([c_re * pr - c_im * pi, -(c_re * pi + c_im * pr)], axis=-1)
        return b, c

    def state_in(nvalid):
        zero = jnp.zeros((depth, S5_GROUPS, S5_GROUP, 2 * S5_STATE), F32)
        return compact([times(pw[nvalid - 1 - s])[0] if s < nvalid else zero for s in range(q)])

    btc = compact([times(pw[0])[0]])
    cmc = compact([times(p)[1] for p in pw])
    e = np.zeros((2, S5_STATE, 2, ng, S5_STATE), np.float32)
    for g in range(ng):
        e[:, :, :, g, :] = np.eye(2 * S5_STATE).reshape(2, S5_STATE, 2, S5_STATE)
    e = jnp.asarray(e.reshape(2 * S5_STATE, S5_SUPER_STATE), BF16)

    def packed(p):
        pr = p[0].reshape(depth, nsb, ng * S5_STATE)
        pi = p[1].reshape(depth, nsb, ng * S5_STATE)
        return jnp.concatenate([pr, pr], axis=-1), jnp.concatenate([-pi, pi], axis=-1)

    doubling = [pw[q]]
    for _ in range(S5_DOUBLINGS - 1):
        doubling.append(cmul(doubling[-1], doubling[-1]))
    pa = jnp.stack([packed(p)[0] for p in doubling], axis=2)
    pb = jnp.stack([packed(p)[1] for p in doubling], axis=2)
    va16, vb16 = packed(pw[q])
    va, vb = packed(pw[valid])
    ex = lambda a: a[:, :, None, :]
    return {"btc": btc, "smc16": state_in(q), "smc": state_in(valid), "cmc": cmc, "e": e,
            "pa": pa, "pb": pb, "va16": ex(va16), "vb16": ex(vb16), "va": ex(va), "vb": ex(vb)}


def _t5_bucket(dist):
    dist = np.asarray(dist)
    large = REL_MAX_EXACT + (np.log(np.maximum(dist, 1) / REL_MAX_EXACT)
                             / math.log(REL_MAX_DIST / REL_MAX_EXACT)
                             * (REL_BUCKETS - REL_MAX_EXACT)).astype(np.int32)
    large = np.minimum(large, REL_BUCKETS - 1)
    return np.where(dist < REL_MAX_EXACT, dist, large).astype(np.int32)


def _bias_steps(rel_bias, grp, dil):
    heads = slice(grp * SWA_GROUP_HEADS, (grp + 1) * SWA_GROUP_HEADS)
    buckets = _t5_bucket(dil * np.arange(SWA_BLOCK + 1))
    onehot = np.zeros((SWA_BLOCK + 1, REL_BUCKETS), np.float32)
    onehot[np.arange(SWA_BLOCK + 1), buckets] = 1.0
    steps = jnp.einsum("jb,bh->hj", onehot, rel_bias[:, heads], precision=lax.Precision.HIGHEST)
    return steps.astype(F32)


def _softmax_pieces(logits):
    m = logits[0].max(axis=1, keepdims=True)
    for s in logits[1:]:
        m = jnp.maximum(m, s.max(axis=1, keepdims=True))
    ps = [jnp.exp(s - m) for s in logits]
    den = ps[0].sum(axis=1, keepdims=True)
    for p in ps[1:]:
        den = den + p.sum(axis=1, keepdims=True)
    return ps, den, m + jnp.log(den)


SWA_SAMPLE_LANES = 2048
SWA_UNROLL = 8
SWA_DENSE_BLOCKS = 8


def _swa_prompt_kernel(q_ref, k_ref, v_ref, kp_ref, vp_ref, bias_ref, o_ref, lse_ref, kvt_ref, *, dil):
    n = SWA_BLOCK
    pair = pl.program_id(1)
    first = pl.program_id(2) == 0
    scale = SWA_HEAD_DIM ** -0.5

    ones = jnp.ones((n, SWA_HEAD_DIM), BF16)

    def logits(sl, kp, vp, mask_prev):
        qq, kk = q_ref[sl, :], k_ref[sl, :]
        heads = []
        for j in range(2):
            hs = slice(j * SWA_HEAD_DIM, (j + 1) * SWA_HEAD_DIM)
            bias = bias_ref[2 * pair + j]
            bias_prev = bias[:, 0:n] if mask_prev is None else jnp.where(mask_prev, MASKED, bias[:, 0:n])
            q_h = qq[:, hs] * scale
            s_cur = _dot_nt(q_h, kk[:, hs]) + bias[:, n:2 * n]
            s_prev = _dot_nt(q_h, kp[:, hs]) + bias_prev
            heads.append((s_cur, s_prev, jnp.max(jnp.maximum(s_cur, s_prev), axis=1, keepdims=True)))
        return sl, vp, heads

    def finish(sl, vp, heads):
        vv = v_ref[sl, :]
        outs, lses = [], []
        for j, (s_cur, s_prev, m) in enumerate(heads):
            hs = slice(j * SWA_HEAD_DIM, (j + 1) * SWA_HEAD_DIM)
            p_cur = jnp.exp(s_cur - m).astype(BF16)
            p_prev = jnp.exp(s_prev - m).astype(BF16)
            den = jnp.dot(p_cur, ones, preferred_element_type=F32) + jnp.dot(p_prev, ones, preferred_element_type=F32)
            outs.append((_dot(p_cur, vv[:, hs]) + _dot(p_prev, vp[:, hs])) / den)
            lses.append(m + jnp.log(den))
        o_ref[sl, :] = jnp.concatenate(outs, axis=1)
        lse_ref[sl, :] = jnp.concatenate(lses, axis=1)

    def run(tiles):
        for t in [logits(*a) for a in tiles]:
            finish(*t)

    if dil == 1:
        tiles = []
        for j in range(SWA_DENSE_BLOCKS):
            if j == 0:
                tiles.append((slice(0, n), kp_ref[...], vp_ref[...], first))
            else:
                before = slice((j - 1) * n, j * n)
                tiles.append((slice(j * n, (j + 1) * n), k_ref[before, :], v_ref[before, :], None))
        run(tiles)
    else:
        unroll = min(dil, SWA_UNROLL)

        def group(i, carry):
            tiles = []
            for u in range(unroll):
                sl = pl.ds(i * unroll + u, n, stride=dil)
                tiles.append((sl, kp_ref[sl, :], vp_ref[sl, :], first))
            run(tiles)
            return carry
        if dil == unroll:
            group(0, 0)
        else:
            lax.fori_loop(0, dil // unroll, group, 0)
    rows = k_ref.shape[0]
    kvt_ref[0] = k_ref[rows - n * dil:rows, :].T
    kvt_ref[1] = v_ref[rows - n * dil:rows, :].T


def _swa_prompt_call(p_swa, bias, kvt_prev, l, depth, *, grp, dil, nb, seq):
    ta = p_swa.shape[0]
    win = SWA_BLOCK * dil
    sb = SWA_BLOCK * SWA_DENSE_BLOCKS if dil == 1 else win
    assert dil == 1 or dil % min(dil, SWA_UNROLL) == 0
    nsb = seq // sb
    lanes = 2 * SWA_HEAD_DIM
    npair = SWA_GROUP_HEADS // 2
    cur = lambda col: pl.BlockSpec((sb, lanes), lambda b, p, c: (b * nsb + c, 2 * col + p))
    if dil == 1:
        per = sb // SWA_BLOCK
        prv = lambda col: pl.BlockSpec(
            (SWA_BLOCK, lanes), lambda b, p, c: (jnp.maximum((b * nsb + c) * per - 1, 0), 2 * col + p))
    else:
        prv = lambda col: pl.BlockSpec((sb, lanes), lambda b, p, c: (b * nsb + jnp.maximum(c - 1, 0), 2 * col + p))
    out = pl.BlockSpec((sb, lanes), lambda b, p, c: (b * nsb + c, p))
    kvt_spec = pl.BlockSpec((None, None, 2, None, lanes, win), lambda b, p, c: (l, b, 0, p, 0, 0))
    shape = jax.ShapeDtypeStruct((ta, SWA_GROUP_WIDTH), F32)
    in_specs = [cur(grp), cur(3 + grp), cur(6 + grp), prv(3 + grp), prv(6 + grp), _full(bias.shape)]
    args = [p_swa, p_swa, p_swa, p_swa, p_swa, bias]
    kern, aliases = _with_aliased(functools.partial(_swa_prompt_kernel, dil=dil), in_specs, args,
                                  [] if kvt_prev is None else [(kvt_prev, 2)])
    return pl.pallas_call(
        kern,
        grid=(nb, npair, nsb),
        in_specs=in_specs,
        out_specs=[out, out, kvt_spec],
        out_shape=[shape, shape, jax.ShapeDtypeStruct((depth, nb, 2, npair, lanes, win), F32)],
        input_output_aliases=aliases,
        compiler_params=_params(3),
        name="swa_prompt_d%d" % dil,
    )(*args)


def _swa_prompt_bias(steps):
    n = SWA_BLOCK
    period = 3 * n + 1
    f = jnp.concatenate([steps[:, ::-1], jnp.full((steps.shape[0], period - (n + 1)), MASKED, F32)], axis=1)
    tiled = jnp.tile(f, (1, n))[:, :n * (period - 1)]
    return tiled.reshape(steps.shape[0], n, period - 1)[:, :, :2 * n]


def _swa_sample_kernel(q_ref, k_ref, v_ref, buf_ref, bias_buf_ref, bias_new_ref, o_ref, lse_ref, cache_ref, *,
                       width, t_new):
    scale = SWA_HEAD_DIM ** -0.5
    w = SWA_GROUP_WIDTH
    for i in range(buf_ref.shape[0]):
        rs = slice(i * SAMPLE_ROWS, (i + 1) * SAMPLE_ROWS)
        qq, kn, vn = q_ref[rs, :], k_ref[rs, :], v_ref[rs, :]
        outs, lses = [], []
        for h in range(SWA_GROUP_HEADS):
            hs = slice(h * SWA_HEAD_DIM, (h + 1) * SWA_HEAD_DIM)
            k_t = buf_ref[i, h * SWA_HEAD_DIM:(h + 1) * SWA_HEAD_DIM, :]
            v_t = buf_ref[i, w + h * SWA_HEAD_DIM:w + (h + 1) * SWA_HEAD_DIM, :]
            q_h = qq[:, hs] * scale
            ps, den, lse = _softmax_pieces([_dot(q_h, k_t) + bias_buf_ref[h],
                                            _dot_nt(q_h, kn[:, hs]) + bias_new_ref[h]])
            outs.append((_dot_nt(ps[0], v_t) + _dot(ps[1], vn[:, hs])) / den)
            lses.append(jnp.broadcast_to(lse, (SAMPLE_ROWS, SWA_HEAD_DIM)))
        o_ref[rs, :] = jnp.concatenate(outs, axis=1)
        lse_ref[rs, :] = jnp.concatenate(lses, axis=1)
        new_t = jnp.concatenate([kn, vn], axis=1).T
        cache_ref[i] = pltpu.roll(buf_ref[i], width - t_new, axis=1)
        cache_ref[i, :, width - t_new:width] = new_t[:, 0:t_new]


def _swa_sample_call(p_swa, cache_t, bias_buf, bias_new, prev_o, prev_lse, prev_cache, l, *, grp, nb, row0, t_new):
    ta = p_swa.shape[0]
    width = cache_t.shape[3]
    w = SWA_GROUP_WIDTH
    per = max(1, min(nb, SWA_SAMPLE_LANES // width))
    while nb % per:
        per -= 1
    assert row0 % (SAMPLE_ROWS * per) == 0
    blk0 = row0 // (SAMPLE_ROWS * per)
    nb = nb // per
    tok = lambda col: pl.BlockSpec((SAMPLE_ROWS * per, w), lambda b: (blk0 + b, col))
    cache_spec = pl.BlockSpec((None, per, 2 * w, width), lambda b: (l, b, 0, 0))
    in_specs = [tok(grp), tok(3 + grp), tok(6 + grp), cache_spec, _full(bias_buf.shape), _full(bias_new.shape)]
    args = [p_swa, p_swa, p_swa, cache_t, bias_buf, bias_new]
    aliased = [(prev_o, 0), (prev_lse, 1)] + ([] if prev_cache is None else [(prev_cache, 2)])
    kern, aliases = _with_aliased(functools.partial(_swa_sample_kernel, width=width, t_new=t_new), in_specs, args,
                                  aliased)
    return pl.pallas_call(
        kern,
        grid=(nb,),
        in_specs=in_specs,
        out_specs=[tok(0), tok(0), cache_spec],
        out_shape=[jax.ShapeDtypeStruct((ta, w), F32), jax.ShapeDtypeStruct((ta, w), F32),
                   jax.ShapeDtypeStruct(cache_t.shape, F32)],
        input_output_aliases=aliases,
        compiler_params=_params(1),
        name="swa_sample_w%d" % width,
    )(*args)


def _swa_sample_bias(steps, window, dil, width, t_new):
    n = window // dil
    nh = steps.shape[0]
    g = jnp.pad(steps[:, :, None], ((0, 0), (0, 0), (0, dil - 1)), constant_values=MASKED).reshape(nh, (n + 1) * dil)
    g = jnp.pad(g, ((0, 0), (0, width + SAMPLE_ROWS)), constant_values=MASKED)
    masked_row = jnp.full((nh, 1, width), MASKED, F32)
    rows = [g[:, t + 1:t + 1 + width][:, None, ::-1] if t < t_new else masked_row for t in range(SAMPLE_ROWS)]
    b_buf = jnp.concatenate(rows, axis=1)
    new_rows = []
    for t in range(SAMPLE_ROWS):
        if t < t_new:
            row = jnp.concatenate([g[:, 0:t + 1][:, ::-1], jnp.full((nh, SAMPLE_ROWS - t - 1), MASKED, F32)], axis=1)
        else:
            row = jnp.full((nh, SAMPLE_ROWS), MASKED, F32)
        new_rows.append(row[:, None, :])
    return b_buf, jnp.concatenate(new_rows, axis=1)


def _merge_kernel(x_ref, yssd_ref, o0_ref, l0_ref, o1_ref, l1_ref, o2_ref, l2_ref, ys5_ref, u_ref, yml_ref,
                  g_ref, wg01_ref, wg23_ref, wssd_ref, wswa_ref, ws5_ref, wml_ref, wglu_ref, wout_ref, d_ref, out_ref):
    for rs in _sub_tiles(x_ref.shape[0]):
        x = x_ref[rs, :]
        h = _rms(x, g_ref[...]).astype(BF16)
        l0, l1, l2 = l0_ref[rs, :], l1_ref[rs, :], l2_ref[rs, :]
        m = jnp.maximum(jnp.maximum(l0, l1), l2)
        e0, e1, e2 = jnp.exp(l0 - m), jnp.exp(l1 - m), jnp.exp(l2 - m)
        y_swa = (e0 * o0_ref[rs, :] + e1 * o1_ref[rs, :] + e2 * o2_ref[rs, :]) / (e0 + e1 + e2)
        y5 = ys5_ref[rs, :] + d_ref[...] * u_ref[rs, :]
        y_s5 = y5 * _sigmoid(_dot(y5, wglu_ref[...]))
        branches = ((yssd_ref[rs, :], wssd_ref), (y_swa, wswa_ref), (y_s5, ws5_ref), (yml_ref[rs, :], wml_ref))
        merged = None
        for i, (y, w_ref) in enumerate(branches):
            wg_ref = wg01_ref if i < 2 else wg23_ref
            gate = _sigmoid(jnp.dot(h, wg_ref[:, (i % 2) * D_MODEL:(i % 2 + 1) * D_MODEL],
                                    preferred_element_type=F32))
            term = gate * _dot(y, w_ref[...])
            merged = term if merged is None else merged + term
        out_ref[rs, :] = x + _dot(merged, wout_ref[...])


def _merge_call(x, acts, wts, l):
    ta = x.shape[0]
    tm = TOKEN_TILE
    row = lambda a: pl.BlockSpec((tm, a.shape[1]), lambda i: (i, 0))
    ws = tuple(wts[k] for k in ("w_br_ssd", "w_br_swa", "w_br_s5", "w_br_mlstm", "s5_w_glu", "w_out", "s5_d"))
    half = GATE_COLS // 2
    gate = lambda k: pl.BlockSpec((None, D_MODEL, half), lambda i: (l, 0, MAIN_COLS // half + k),
                                  pipeline_mode=pl.Buffered(1))
    w_all = wts["w_main"]
    return pl.pallas_call(
        _merge_kernel,
        grid=(ta // tm,),
        in_specs=[row(a) for a in (x, *acts)] + [_layer(wts["norm_mix"], l), gate(0), gate(1)]
        + [_layer(w, l) for w in ws],
        out_specs=row(x),
        out_shape=jax.ShapeDtypeStruct(x.shape, F32),
        compiler_params=_params(1),
        name="merge",
    )(x, *acts, wts["norm_mix"], w_all, w_all, *ws)


def _norm_matmul_kernel(x_ref, g_ref, w_ref, o_ref):
    o_ref[...] = jnp.dot(_rms(x_ref[...], g_ref[...]).astype(BF16), w_ref[...], preferred_element_type=F32)


def _norm_matmul(x, g, w, l, tm):
    rows = x.shape[0]
    return pl.pallas_call(
        _norm_matmul_kernel,
        grid=(rows // tm,),
        in_specs=[pl.BlockSpec((tm, x.shape[1]), lambda i: (i, 0)), _layer(g, l), _layer(w, l)],
        out_specs=pl.BlockSpec((tm, w.shape[2]), lambda i: (i, 0)),
        out_shape=jax.ShapeDtypeStruct((rows, w.shape[2]), F32),
        compiler_params=_params(1),
        name="norm_matmul",
    )(x, g, w)


def _xattn_heads(qs, kv_ref, kv_rows):
    scale = XA_HEAD_DIM ** -0.5
    n_mem = kv_ref.shape[0] // (2 * XA_HEADS) if kv_rows else kv_ref.shape[0]
    scored = []
    for q in qs:
        for h in range(XA_HEADS):
            hs = slice(h * XA_HEAD_DIM, (h + 1) * XA_HEAD_DIM)
            k_h = kv_ref[h * n_mem:(h + 1) * n_mem, :] if kv_rows else kv_ref[:, hs]
            logits = _dot_nt(q[:, hs], k_h) * scale
            scored.append((logits, logits.max(axis=1, keepdims=True)))
    outs = []
    for i, (logits, m) in enumerate(scored):
        h = i % XA_HEADS
        if kv_rows:
            v_h = kv_ref[(XA_HEADS + h) * n_mem:(XA_HEADS + h + 1) * n_mem, :]
        else:
            v_h = kv_ref[:, D_MODEL + h * XA_HEAD_DIM:D_MODEL + (h + 1) * XA_HEAD_DIM]
        p = jnp.exp(logits - m)
        outs.append(_dot(p / p.sum(axis=1, keepdims=True), v_h))
    return [jnp.concatenate(outs[i * XA_HEADS:(i + 1) * XA_HEADS], axis=1) for i in range(len(qs))]


def _xattn_prompt_kernel(x_ref, kv_ref, g_ref, wq_ref, wo_ref, out_ref):
    subs = _sub_tiles(x_ref.shape[0])
    xs = [x_ref[rs, :] for rs in subs]
    qs = [jnp.dot(_rms(x, g_ref[...]).astype(BF16), wq_ref[...], preferred_element_type=F32) for x in xs]
    for rs, x, a in zip(subs, xs, _xattn_heads(qs, kv_ref, False)):
        out_ref[rs, :] = x + _dot(a, wo_ref[...])


def _xattn_prompt_call(x, kv, wts, l, *, n_tiles, tiles_per_seq):
    ta = x.shape[0]
    tm = TOKEN_TILE
    row = pl.BlockSpec((tm, D_MODEL), lambda i: (i, 0))
    ws = tuple(wts[k] for k in ("norm_xa", "xa_wq", "xa_wo"))
    return pl.pallas_call(
        _xattn_prompt_kernel,
        grid=(n_tiles,),
        in_specs=[row, pl.BlockSpec((None,) + kv.shape[1:], lambda i: (i // tiles_per_seq, 0, 0))]
        + [_layer(w, l) for w in ws],
        out_specs=row,
        out_shape=jax.ShapeDtypeStruct((ta, D_MODEL), F32),
        compiler_params=_params(1),
        name="xattn_prompt",
    )(x, kv, *ws)


def _xattn_sample_kernel(x_ref, kv_ref, g_ref, wq_ref, wo_ref, out_ref, q_sc, o_sc):
    b = pl.program_id(0)

    @pl.when(b == 0)
    def _():
        q_sc[...] = jnp.dot(_rms(x_ref[...], g_ref[...]).astype(BF16), wq_ref[...], preferred_element_type=F32)

    rows = pl.ds(pl.multiple_of(b * SAMPLE_ROWS, SAMPLE_ROWS), SAMPLE_ROWS)
    o_sc[rows, :] = _xattn_heads([q_sc[rows, :]], kv_ref, True)[0]

    @pl.when(b == pl.num_programs(0) - 1)
    def _():
        out_ref[...] = x_ref[...] + _dot(o_sc[...], wo_ref[...])


def _xattn_sample_call(x, kv, wts, l, *, nb, row0, prev):
    ta = x.shape[0]
    n_s = nb * SAMPLE_ROWS
    rows = pl.BlockSpec((n_s, D_MODEL), lambda b: (row0 // n_s, 0))
    ws = tuple(wts[k] for k in ("norm_xa", "xa_wq", "xa_wo"))
    in_specs = [rows, pl.BlockSpec((None, None) + kv.shape[2:], lambda b: (l, b, 0, 0))] + [_layer(w, l) for w in ws]
    args = [x, kv, *ws]
    kern, aliases = _with_aliased(_xattn_sample_kernel, in_specs, args, [(prev, 0)])
    return pl.pallas_call(
        kern,
        grid=(nb,),
        in_specs=in_specs,
        out_specs=rows,
        out_shape=jax.ShapeDtypeStruct((ta, D_MODEL), F32),
        scratch_shapes=[pltpu.VMEM((n_s, D_MODEL), F32), pltpu.VMEM((n_s, D_MODEL), F32)],
        input_output_aliases=aliases,
        compiler_params=_params(1),
        name="xattn_sample",
    )(*args)


def _ffn_kernel(x_ref, g_ref, w1_ref, w2_ref, out_ref):
    for rs in _sub_tiles(x_ref.shape[0]):
        x = x_ref[rs, :]
        a = jnp.dot(_rms(x, g_ref[...]).astype(BF16), w1_ref[...], preferred_element_type=F32)
        a = jnp.square(jnp.maximum(a, 0.0))
        out_ref[rs, :] = x + _dot(a, w2_ref[...])


def _ffn_call(x, wts, l):
    ta = x.shape[0]
    tm = TOKEN_TILE
    row = pl.BlockSpec((tm, D_MODEL), lambda i: (i, 0))
    ws = tuple(wts[k] for k in ("norm_mlp", "w_ff1", "w_ff2"))
    return pl.pallas_call(
        _ffn_kernel,
        grid=(ta // tm,),
        in_specs=[row] + [_layer(w, l) for w in ws],
        out_specs=row,
        out_shape=jax.ShapeDtypeStruct(x.shape, F32),
        compiler_params=_params(1),
        name="ffn",
    )(x, *ws)


def _ffn_final_kernel(x_ref, g_ref, w1_ref, w2_ref, gf_ref, yp_ref, ys_ref, *, prompt_tiles):
    i = pl.program_id(0)
    ys = []
    for rs in _sub_tiles(x_ref.shape[0]):
        x = x_ref[rs, :]
        a = jnp.dot(_rms(x, g_ref[...]).astype(BF16), w1_ref[...], preferred_element_type=F32)
        a = jnp.square(jnp.maximum(a, 0.0))
        ys.append(_rms(x + _dot(a, w2_ref[...]), gf_ref[...]))
    y = jnp.concatenate(ys, axis=0)

    @pl.when(i < prompt_tiles)
    def _():
        yp_ref[...] = y

    @pl.when(i >= prompt_tiles)
    def _():
        ys_ref[...] = y


def _ffn_final_call(x, wts, l, g_final, n_p):
    ta = x.shape[0]
    tm = TOKEN_TILE
    n_pt = n_p // tm
    assert n_p % tm == 0 and ta - n_p == tm
    row = pl.BlockSpec((tm, D_MODEL), lambda i: (i, 0))
    ws = tuple(wts[k] for k in ("norm_mlp", "w_ff1", "w_ff2"))
    return pl.pallas_call(
        functools.partial(_ffn_final_kernel, prompt_tiles=n_pt),
        grid=(ta // tm,),
        in_specs=[row] + [_layer(w, l) for w in ws] + [_full(g_final.shape)],
        out_specs=[pl.BlockSpec((tm, D_MODEL), lambda i: (jnp.minimum(i, n_pt - 1), 0)),
                   pl.BlockSpec((tm, D_MODEL), lambda i: (0, 0))],
        out_shape=[jax.ShapeDtypeStruct((n_p, D_MODEL), F32), jax.ShapeDtypeStruct((tm, D_MODEL), F32)],
        compiler_params=_params(1),
        name="ffn_final",
    )(x, *ws, g_final)


def _prep_weights(w):
    o = IN_OFFS
    w_in = w["w_in"]
    small = jnp.concatenate([w_in[:, :, o[2]:o[3]], w_in[:, :, o[11]:o[13]]], axis=2)
    row = lambda v: v[:, None, :].astype(F32)
    col = lambda v: v[:, :, None].astype(F32)
    bf = lambda v: v.astype(BF16)
    return {
        "norm_mix": row(w["norm_mix"]),
        "w_main": jnp.concatenate([w_in[:, :, o[0]:o[2]], w_in[:, :, o[3]:o[11]], w_in[:, :, o[13]:o[14]]],
                                  axis=2).astype(BF16),
        "w_small": jnp.pad(small, ((0, 0), (0, 0), (0, SMALL_COLS - small.shape[2]))).astype(BF16),
        "w_small_t": jnp.transpose(small, (0, 2, 1)).astype(BF16),
        "ssd_conv_w": w["ssd_conv_w"], "ssd_conv_b": row(w["ssd_conv_b"]),
        "dtb_c": row(w["ssd_dt_bias"]), "dtb_r": col(w["ssd_dt_bias"]),
        "alog_c": row(w["ssd_a_log"]), "alog_r": col(w["ssd_a_log"]),
        "ssd_dskip": row(jnp.repeat(w["ssd_d"], SSD_HEAD_DIM, axis=1)), "ssd_norm": row(w["ssd_norm"]),
        "ib_c": row(w["mlstm_i_bias"]), "ib_r": col(w["mlstm_i_bias"]),
        "fb_c": row(w["mlstm_f_bias"]), "fb_r": col(w["mlstm_f_bias"]),
        "mlstm_norm": row(w["mlstm_norm"]),
        "s5_d": row(w["s5_d"]), "s5_w_glu": bf(w["s5_w_glu"]),
        "w_br_ssd": bf(w["w_br_ssd"]), "w_br_swa": bf(w["w_br_swa"]), "w_br_s5": bf(w["w_br_s5"]),
        "w_br_mlstm": bf(w["w_br_mlstm"]), "w_out": bf(w["w_out"]),
        "norm_xa": row(w["norm_xa"]), "xa_wq": bf(w["xa_wq"]), "xa_wo": bf(w["xa_wo"]),
        "norm_mem": row(w["norm_mem"]),
        "xa_wkv": jnp.concatenate([w["xa_wk"], w["xa_wv"]], axis=2).astype(BF16),
        "norm_mlp": row(w["norm_mlp"]), "w_ff1": bf(w["w_ff1"]), "w_ff2": bf(w["w_ff2"]),
    }


def kernel(x_prompt, x_sample, state_ssd, state_ssd_conv, cache_swa_w128, cache_swa_w512, cache_swa_w2048, state_s5, state_mlstm_c, state_mlstm_n, state_mlstm_m, cache_mem_kv, mem_prompt, norm_mix, w_in, ssd_conv_w, ssd_conv_b, ssd_dt_bias, ssd_a_log, ssd_d, ssd_norm, rel_bias, s5_a_re, s5_a_im, s5_log_dt, s5_b_re, s5_b_im, s5_c_re, s5_c_im, s5_d, s5_w_glu, mlstm_i_bias, mlstm_f_bias, mlstm_norm, w_br_ssd, w_br_swa, w_br_s5, w_br_mlstm, w_out, norm_xa, norm_mem, xa_wq, xa_wk, xa_wv, xa_wo, norm_mlp, w_ff1, w_ff2, norm_final):
    wts = _prep_weights(dict(
        norm_mix=norm_mix, w_in=w_in, ssd_conv_w=ssd_conv_w, ssd_conv_b=ssd_conv_b, ssd_dt_bias=ssd_dt_bias,
        ssd_a_log=ssd_a_log, ssd_d=ssd_d, ssd_norm=ssd_norm, s5_d=s5_d, s5_w_glu=s5_w_glu,
        mlstm_i_bias=mlstm_i_bias, mlstm_f_bias=mlstm_f_bias, mlstm_norm=mlstm_norm, w_br_ssd=w_br_ssd,
        w_br_swa=w_br_swa, w_br_s5=w_br_s5, w_br_mlstm=w_br_mlstm, w_out=w_out, norm_xa=norm_xa, norm_mem=norm_mem,
        xa_wq=xa_wq, xa_wk=xa_wk, xa_wv=xa_wv, xa_wo=xa_wo, norm_mlp=norm_mlp, w_ff1=w_ff1, w_ff2=w_ff2))
    nb_p, seq, _ = x_prompt.shape
    nb_s, t_new, _ = x_sample.shape
    depth = w_in.shape[0]
    n_mem = mem_prompt.shape[1]
    rows_s = SAMPLE_ROWS
    n_p = nb_p * seq
    n_s = nb_s * rows_s
    ta = n_p + n_s
    assert seq % (SWA_BLOCK * SWA_PATTERN[-1][1]) == 0 and ta % TOKEN_TILE == 0 and t_new <= rows_s
    assert n_p % n_s == 0 and (seq // S5_CHUNK) & (seq // S5_CHUNK - 1) == 0
    caches = (cache_swa_w128, cache_swa_w512, cache_swa_w2048)

    xs_pad = jnp.pad(x_sample, ((0, 0), (0, rows_s - t_new), (0, 0))).reshape(n_s, D_MODEL)
    x = jnp.concatenate([x_prompt.reshape(n_p, D_MODEL), xs_pad], axis=0)

    steps = [_bias_steps(rel_bias, g, dil) for g, (_, dil) in enumerate(SWA_PATTERN)]
    prompt_bias = [_swa_prompt_bias(s) for s in steps]
    sample_bias = [_swa_sample_bias(steps[g], win, dil, caches[g].shape[2], t_new)
                   for g, (win, dil) in enumerate(SWA_PATTERN)]
    caches_t = [jnp.transpose(c, (0, 1, 3, 4, 5, 2)).reshape(depth, nb_s, 2 * SWA_GROUP_WIDTH, c.shape[2])
                for c in caches]
    mem_kv_s = jnp.transpose(cache_mem_kv, (0, 1, 3, 4, 2, 5)).reshape(depth, nb_s, 2 * XA_HEADS * n_mem, XA_HEAD_DIM)
    s5_tab = _s5_tables(s5_a_re, s5_a_im, s5_log_dt, s5_b_re, s5_b_im, s5_c_re, s5_c_im, t_new)
    s5_h0_s = jnp.transpose(state_s5.reshape(depth, nb_s, S5_SUPER, S5_SUPER_GROUPS, S5_STATE, 2),
                            (0, 2, 1, 5, 3, 4)).reshape(depth, S5_SUPER, 1, nb_s, S5_SUPER_STATE)

    zeros = lambda *s: jnp.zeros(s, F32)
    zero_conv = zeros(1, nb_p, 8, SSD_CONV_DIM)
    zero_ssd = zeros(1, nb_p, SSD_HEADS, SSD_HEAD_DIM, SSD_STATE)
    zero_c = zeros(1, nb_p, ML_HEADS, ML_HEAD_DIM, ML_HEAD_DIM)
    zero_vec = zeros(1, nb_p, ML_HEADS, 1, ML_HEAD_DIM)
    zero_s5 = zeros(S5_SUPER, nb_p, 1, S5_SUPER_STATE)
    conv_init_s = jnp.pad(state_ssd_conv, ((0, 0), (0, 0), (8 - (SSD_CONV - 1), 0), (0, 0)))
    n_init_s = state_mlstm_n[:, :, :, None, :]
    m_init_s = jnp.broadcast_to(state_mlstm_m[:, :, :, None, None], (depth, nb_s, ML_HEADS, 1, ML_HEAD_DIM))

    outs = {k: [] for k in ("ssd_p", "ssd_s", "conv_p", "conv_s", "s5_p", "s5_s", "c_p", "c_s", "n_p", "n_s",
                            "m_p", "m_s", "kv_p")}
    kvt_p = [None] * len(SWA_PATTERN)
    cache_out = [None] * len(SWA_PATTERN)
    ssd_q, ml_q, nsub = 128, 128, 4
    nchunk = seq // S5_CHUNK

    for l in range(depth):
        p_ssd, p_swa, p_s5, p_ml, p_small, p_small_t = _inproj(x, wts, l)
        small_t_s = jnp.transpose(p_small[n_p:, :SMALL_ROWS].reshape(nb_s, rows_s, SMALL_ROWS), (0, 2, 1))

        y_ssd, st_p = _ssd_call(p_ssd, p_small, p_small_t, zero_conv, zero_ssd, 0, wts, l, nb=nb_p,
                                nc=seq // (ssd_q * nsub), q=ssd_q, nsub=nsub, valid=ssd_q, row0=0, prev=None)
        y_ssd, st_s = _ssd_call(p_ssd, p_small, small_t_s, conv_init_s, state_ssd, l, wts, l,
                                nb=nb_s, nc=1, q=rows_s, nsub=1, valid=t_new, row0=n_p, prev=y_ssd)
        outs["ssd_p"].append(st_p)
        outs["ssd_s"].append(st_s)
        tail = SSD_CONV - 1
        xbc_s = p_ssd[n_p:, SSD_D_INNER:].reshape(nb_s, rows_s, SSD_CONV_DIM)[:, :t_new]
        outs["conv_p"].append(jnp.stack([p_ssd[(b + 1) * seq - tail:(b + 1) * seq, SSD_D_INNER:]
                                         for b in range(nb_p)]))
        outs["conv_s"].append(jnp.concatenate([state_ssd_conv[l], xbc_s], axis=1)[:, -(SSD_CONV - 1):])

        swa_acts = []
        for g, (win, dil) in enumerate(SWA_PATTERN):
            o_g, lse_g, kvt_p[g] = _swa_prompt_call(p_swa, prompt_bias[g], kvt_p[g], l, depth, grp=g, dil=dil,
                                                    nb=nb_p, seq=seq)
            o_g, lse_g, cache_out[g] = _swa_sample_call(p_swa, caches_t[g], sample_bias[g][0], sample_bias[g][1],
                                                        o_g, lse_g, cache_out[g], l, grp=g, nb=nb_s, row0=n_p,
                                                        t_new=t_new)
            swa_acts += [o_g, lse_g]

        y_s5, h_p = _s5_call(p_s5, zero_s5, s5_tab, l, nseq=nb_p, rows=nchunk, scan=True, row0=0, prev=None)
        y_s5, h_s = _s5_call(p_s5, s5_h0_s[l], s5_tab, l, nseq=1, rows=nb_s, scan=False, row0=n_p, prev=y_s5)
        outs["s5_p"].append(h_p)
        outs["s5_s"].append(h_s)

        y_ml, c_p, nn_p, m_p = _mlstm_call(p_ml, p_small, p_small_t, zero_c, zero_vec, zero_vec, 0, wts, l, nb=nb_p,
                                           nc=seq // (ml_q * nsub), q=ml_q, nsub=nsub, valid=ml_q, row0=0,
                                           prev=None)
        y_ml, c_s, nn_s, m_s = _mlstm_call(p_ml, p_small, small_t_s, state_mlstm_c, n_init_s, m_init_s, l, wts, l,
                                           nb=nb_s, nc=1, q=rows_s, nsub=1, valid=t_new, row0=n_p, prev=y_ml)
        for key, val in (("c_p", c_p), ("c_s", c_s), ("n_p", nn_p[:, :, 0]), ("n_s", nn_s[:, :, 0]),
                         ("m_p", m_p[:, :, 0, 0]), ("m_s", m_s[:, :, 0, 0])):
            outs[key].append(val)

        x = _merge_call(x, (y_ssd, *swa_acts, y_s5, p_s5, y_ml), wts, l)

        kv_p = _norm_matmul(mem_prompt.reshape(nb_p * n_mem, D_MODEL), wts["norm_mem"], wts["xa_wkv"], l, n_mem)
        outs["kv_p"].append(kv_p.reshape(nb_p, n_mem, 2, XA_HEADS, XA_HEAD_DIM))
        x_new = _xattn_prompt_call(x, kv_p.reshape(nb_p, n_mem, 2 * D_MODEL), wts, l, n_tiles=n_p // TOKEN_TILE,
                                   tiles_per_seq=seq // TOKEN_TILE)
        x = _xattn_sample_call(x, mem_kv_s, wts, l, nb=nb_s, row0=n_p, prev=x_new)

        if l + 1 < depth:
            x = _ffn_call(x, wts, l)
        else:
            y_p, y_s = _ffn_final_call(x, wts, l, norm_final[None, :], n_p)

    y_prompt = y_p.reshape(nb_p, seq, D_MODEL)
    y_sample = y_s.reshape(nb_s, rows_s, D_MODEL)[:, :t_new]
    st = lambda k: jnp.stack(outs[k])

    def s5_state(k, nb):
        h = st(k).reshape(depth, S5_SUPER, nb, 2, S5_SUPER_GROUPS, S5_STATE)
        return jnp.transpose(h, (0, 2, 1, 4, 5, 3)).reshape(depth, nb, S5_GROUPS, S5_STATE, 2)

    swa_out = []
    for g in range(len(SWA_PATTERN)):
        width_p = kvt_p[g].shape[-1]
        kp = kvt_p[g].reshape(depth, nb_p, 2, SWA_GROUP_HEADS, SWA_HEAD_DIM, width_p)
        swa_out.append(jnp.transpose(kp, (0, 1, 5, 2, 3, 4)))
        cs = cache_out[g].reshape(depth, nb_s, 2, SWA_GROUP_HEADS, SWA_HEAD_DIM, caches[g].shape[2])
        swa_out.append(jnp.transpose(cs, (0, 1, 5, 2, 3, 4)))
    return (y_prompt, y_sample, st("ssd_p"), st("ssd_s"), st("conv_p"), st("conv_s"), *swa_out,
            s5_state("s5_p", nb_p), s5_state("s5_s", nb_s), st("c_p"), st("c_s"),
            st("n_p"), st("n_s"), st("m_p"), st("m_s"), st("kv_p"))
```

```python
import functools
import math

import numpy as np
import jax
import jax.numpy as jnp
from jax import lax
from jax.experimental import pallas as pl
from jax.experimental.pallas import tpu as pltpu

F32 = jnp.float32
BF16 = jnp.bfloat16

D_MODEL = 1024
RMS_EPS = 1e-6
N_BRANCH = 4

SSD_D_INNER = 512
SSD_HEAD_DIM = 64
SSD_HEADS = 8
SSD_GROUPS = 2
SSD_STATE = 64
SSD_CONV = 4
SSD_CONV_DIM = SSD_D_INNER + 2 * SSD_GROUPS * SSD_STATE
SSD_SEG = SSD_D_INNER + SSD_CONV_DIM

SWA_PATTERN = ((128, 1), (512, 4), (2048, 16))
SWA_GROUP_HEADS = 4
SWA_HEAD_DIM = 64
SWA_GROUP_WIDTH = SWA_GROUP_HEADS * SWA_HEAD_DIM
SWA_WIDTH = 3 * SWA_GROUP_WIDTH
SWA_BLOCK = 128
REL_BUCKETS = 32
REL_MAX_EXACT = 16
REL_MAX_DIST = 2048

S5_WIDTH = 512
S5_GROUP = 16
S5_GROUPS = 32
S5_STATE = 64
S5_CHUNK = 16
S5_LANES = 128
S5_SUPER = S5_WIDTH // S5_LANES
S5_SUPER_GROUPS = S5_GROUPS // S5_SUPER
S5_SUPER_STATE = 2 * S5_SUPER_GROUPS * S5_STATE
S5_DOUBLINGS = 12

ML_WIDTH = 512
ML_HEADS = 4
ML_HEAD_DIM = 128

XA_HEADS = 4
XA_HEAD_DIM = 256
D_FF = 4096

MAIN_SSD = (0, SSD_SEG)
MAIN_SWA = (SSD_SEG, SSD_SEG + 3 * SWA_WIDTH)
MAIN_S5 = (MAIN_SWA[1], MAIN_SWA[1] + S5_WIDTH)
MAIN_ML = (MAIN_S5[1], MAIN_S5[1] + 4 * ML_WIDTH)
MAIN_COLS = MAIN_ML[1]
GATE_COLS = N_BRANCH * D_MODEL
SMALL_COLS = 128
SMALL_ROWS = 16

SAMPLE_ROWS = 16
TOKEN_TILE = 512
SUB_TILE = 256
MASKED = -1e30

VMEM_LIMIT = 56 * 1024 * 1024

IN_SIZES = (SSD_D_INNER, SSD_CONV_DIM, SSD_HEADS, SWA_WIDTH, SWA_WIDTH, SWA_WIDTH, S5_WIDTH,
            ML_WIDTH, ML_WIDTH, ML_WIDTH, ML_WIDTH, ML_HEADS, ML_HEADS, N_BRANCH * D_MODEL)
IN_OFFS = tuple(int(v) for v in np.concatenate([[0], np.cumsum(IN_SIZES)]))


def _params(n_axes):
    return pltpu.CompilerParams(dimension_semantics=("arbitrary",) * n_axes, vmem_limit_bytes=VMEM_LIMIT)


def _full(shape):
    nd = len(shape)
    return pl.BlockSpec(shape, lambda *_: (0,) * nd)


def _layer(a, l):
    nd = a.ndim
    return pl.BlockSpec((None,) + a.shape[1:], lambda *_: (l,) + (0,) * (nd - 1), pipeline_mode=pl.Buffered(1))


def _sub_tiles(rows):
    step = min(SUB_TILE, rows)
    return [slice(r, r + step) for r in range(0, rows, step)]


def _with_aliased(kern, in_specs, args, aliased):
    n = len(args)
    aliases = {}
    for arr, out_idx in aliased:
        in_specs.append(pl.BlockSpec(memory_space=pl.ANY))
        aliases[len(args)] = out_idx
        args.append(arr)
    k = len(aliased)
    if k == 0:
        return kern, aliases

    def wrapped(*refs):
        return kern(*refs[:n], *refs[n + k:])
    return wrapped, aliases


def _dot(a, b):
    return jnp.dot(a.astype(BF16), b.astype(BF16), preferred_element_type=F32)


def _dot_nt(a, b):
    return lax.dot_general(a.astype(BF16), b.astype(BF16), (((1,), (1,)), ((), ())), preferred_element_type=F32)


def _dot_tn(a, b):
    return lax.dot_general(a.astype(BF16), b.astype(BF16), (((0,), (0,)), ((), ())), preferred_element_type=F32)


def _split3(x):
    hi = x.astype(BF16)
    r = x - hi.astype(F32)
    mid = r.astype(BF16)
    lo = (r - mid.astype(F32)).astype(BF16)
    return hi, mid, lo


def _dot_exact_l(ones_bf16, x):
    hi, mid, lo = _split3(x)
    f = lambda p: jnp.dot(ones_bf16, p, preferred_element_type=F32)
    return f(hi) + f(mid) + f(lo)


def _dot_exact_r(x, ones_bf16):
    hi, mid, lo = _split3(x)
    f = lambda p: jnp.dot(p, ones_bf16, preferred_element_type=F32)
    return f(hi) + f(mid) + f(lo)


def _rms(x, g):
    return x * lax.rsqrt(jnp.mean(x * x, axis=-1, keepdims=True) + RMS_EPS) * g


def _sigmoid(x):
    return 1.0 / (1.0 + jnp.exp(-x))


def _softplus(x):
    return jnp.maximum(x, 0.0) + jnp.log(1.0 + jnp.exp(-jnp.abs(x)))


def _tri(q, lower):
    r = lax.broadcasted_iota(jnp.int32, (q, q), 0)
    c = lax.broadcasted_iota(jnp.int32, (q, q), 1)
    return (r >= c) if lower else (r <= c)


def _inproj_kernel(x_ref, g_ref, wm_ref, ws_ref, wst_ref, oa_ref, ob_ref, oc_ref, od_ref, os_ref, ost_ref):
    for rs in _sub_tiles(x_ref.shape[0]):
        h = _rms(x_ref[rs, :], g_ref[...]).astype(BF16)
        for o_ref, (lo, hi) in ((oa_ref, MAIN_SSD), (ob_ref, MAIN_SWA), (oc_ref, MAIN_S5), (od_ref, MAIN_ML)):
            o_ref[rs, :] = jnp.dot(h, wm_ref[:, lo:hi], preferred_element_type=F32)
        os_ref[rs, :] = jnp.dot(h, ws_ref[...], preferred_element_type=F32)
        ost_ref[0, :, rs] = lax.dot_general(wst_ref[...], h, (((1,), (1,)), ((), ())), preferred_element_type=F32)


def _inproj(x, wts, l):
    ta = x.shape[0]
    tm = TOKEN_TILE
    widths = [hi - lo for lo, hi in (MAIN_SSD, MAIN_SWA, MAIN_S5, MAIN_ML)] + [SMALL_COLS]
    row = lambda n: pl.BlockSpec((tm, n), lambda i: (i, 0))
    ws = (wts["norm_mix"], wts["w_main"], wts["w_small"], wts["w_small_t"])
    mixer_cols = pl.BlockSpec((None, D_MODEL, MAIN_COLS), lambda i: (l, 0, 0), pipeline_mode=pl.Buffered(1))
    return pl.pallas_call(
        _inproj_kernel,
        grid=(ta // tm,),
        in_specs=[row(D_MODEL), _layer(ws[0], l), mixer_cols, _layer(ws[2], l), _layer(ws[3], l)],
        out_specs=[row(n) for n in widths] + [pl.BlockSpec((1, SMALL_ROWS, tm), lambda i: (0, 0, i))],
        out_shape=[jax.ShapeDtypeStruct((ta, n), F32) for n in widths]
        + [jax.ShapeDtypeStruct((1, SMALL_ROWS, ta), F32)],
        compiler_params=_params(1),
        name="inproj",
    )(x, *ws)


def _ssd_kernel(p_ref, sm_ref, smt_ref, cinit_ref, sinit_ref, cw_ref, cb_ref, dtb_c_ref, dtb_r_ref,
                alog_c_ref, alog_r_ref, dskip_ref, g_ref, y_ref, sout_ref, ext_sc, st_sc, *, q, nsub, valid):
    c = pl.program_id(1)
    rows = q * nsub

    @pl.when(c == 0)
    def _():
        ext_sc[0:8, :] = cinit_ref[...]
        st_sc[...] = sinit_ref[...].reshape(st_sc.shape)

    ext_sc[8:8 + rows, :] = p_ref[:, SSD_D_INNER:SSD_SEG]
    conv = cb_ref[...] + cw_ref[0:1, :] * ext_sc[5:5 + rows, :]
    for j in range(1, SSD_CONV):
        conv = conv + cw_ref[j:j + 1, :] * ext_sc[5 + j:5 + j + rows, :]
    ext_sc[0:8, :] = ext_sc[rows:rows + 8, :]
    xbc_all = conv * _sigmoid(conv)
    z_all = p_ref[:, :SSD_D_INNER]

    dt_c_all = _softplus(sm_ref[:, 0:SSD_HEADS] + dtb_c_ref[...])
    dt_r_all = _softplus(smt_ref[0, 0:SSD_HEADS, :] + dtb_r_ref[...])
    if valid < q:
        dt_c_all = jnp.where(lax.broadcasted_iota(jnp.int32, dt_c_all.shape, 0) < valid, dt_c_all, 0.0)
        dt_r_all = jnp.where(lax.broadcasted_iota(jnp.int32, dt_r_all.shape, 1) < valid, dt_r_all, 0.0)
    da_c_all = dt_c_all * (-jnp.exp(alog_c_ref[...]))
    da_r_all = dt_r_all * (-jnp.exp(alog_r_ref[...]))
    causal = _tri(q, True)
    lower = jnp.where(causal, 1.0, 0.0).astype(BF16)
    upper = jnp.where(_tri(q, False), 1.0, 0.0).astype(BF16)

    rep = SSD_HEADS // SSD_GROUPS
    c_off = SSD_D_INNER + SSD_GROUPS * SSD_STATE
    hd = SSD_HEAD_DIM
    states = [st_sc[g] for g in range(SSD_GROUPS)]
    for j in range(nsub):
        rs = slice(j * q, (j + 1) * q)
        xbc, dt_c, dt_r = xbc_all[rs], dt_c_all[rs], dt_r_all[:, rs]
        xs = xbc[:, :SSD_D_INNER]
        cum_c = _dot_exact_l(lower, da_c_all[rs])
        cum_r = _dot_exact_r(da_r_all[:, rs], upper)
        groups = []
        for g in range(SSD_GROUPS):
            b_g = xbc[:, SSD_D_INNER + g * SSD_STATE:SSD_D_INNER + (g + 1) * SSD_STATE]
            c_g = xbc[:, c_off + g * SSD_STATE:c_off + (g + 1) * SSD_STATE]
            groups.append((b_g, _dot_nt(c_g, b_g), _dot_nt(c_g, states[g])))
        ws = []
        for h in range(SSD_HEADS):
            seg = jnp.where(causal, cum_c[:, h:h + 1] - cum_r[h:h + 1, :], MASKED)
            ws.append(groups[h // rep][1] * jnp.exp(seg) * dt_r[h:h + 1, :])
        ys = []
        for h in range(SSD_HEADS):
            hl = h % rep
            x_h = xs[:, h * hd:(h + 1) * hd]
            ys.append(_dot(ws[h], x_h) + jnp.exp(cum_c[:, h:h + 1]) * groups[h // rep][2][:, hl * hd:(hl + 1) * hd])
        z = z_all[rs]
        y = (jnp.concatenate(ys, axis=1) + dskip_ref[...] * xs) * (z * _sigmoid(z))
        y_ref[rs, :] = _rms(y, g_ref[...])
        for g in range(SSD_GROUPS):
            scaled, decay = [], []
            for h in range(g * rep, (g + 1) * rep):
                last = cum_c[q - 1:q, h:h + 1]
                tail = jnp.exp(last - cum_c[:, h:h + 1]) * dt_c[:, h:h + 1]
                scaled.append(xs[:, h * hd:(h + 1) * hd] * tail)
                decay.append(jnp.broadcast_to(jnp.exp(last), (hd, SSD_STATE)))
            states[g] = (states[g] * jnp.concatenate(decay, axis=0)
                         + _dot_tn(jnp.concatenate(scaled, axis=1), groups[g][0]))
    for g in range(SSD_GROUPS):
        st_sc[g] = states[g]

    @pl.when(c == pl.num_programs(1) - 1)
    def _():
        sout_ref[0] = st_sc[...].reshape(SSD_HEADS, SSD_HEAD_DIM, SSD_STATE)


def _seq_specs(rows_blk, nc, blk0, p_small_t):
    rows = lambda n: pl.BlockSpec((rows_blk, n), lambda b, c: (blk0 + b * nc + c, 0))
    if p_small_t.shape[0] == 1:
        smt_spec = pl.BlockSpec((1, SMALL_ROWS, rows_blk), lambda b, c: (0, 0, blk0 + b * nc + c))
    else:
        smt_spec = pl.BlockSpec((1, SMALL_ROWS, rows_blk), lambda b, c: (b, 0, 0))
    return rows, smt_spec


def _state_spec(a, sl):
    nd = a.ndim
    return pl.BlockSpec((None, None) + a.shape[2:], lambda b, c: (sl, b) + (0,) * (nd - 2))


def _ssd_call(p_ssd, p_small, p_small_t, conv_init, state_init, sl, wts, l, *, nb, nc, q, nsub, valid, row0, prev):
    ta = p_ssd.shape[0]
    blk = q * nsub
    assert row0 % blk == 0 and (valid == q or nsub == 1)
    rows, smt_spec = _seq_specs(blk, nc, row0 // blk, p_small_t)
    ws = tuple(wts[k] for k in ("ssd_conv_w", "ssd_conv_b", "dtb_c", "dtb_r", "alog_c", "alog_r", "ssd_dskip",
                                "ssd_norm"))
    in_specs = [rows(SSD_SEG), rows(SMALL_COLS), smt_spec, _state_spec(conv_init, sl), _state_spec(state_init, sl)]
    in_specs += [_layer(w, l) for w in ws]
    args = [p_ssd, p_small, p_small_t, conv_init, state_init, *ws]
    kern, aliases = _with_aliased(functools.partial(_ssd_kernel, q=q, nsub=nsub, valid=valid), in_specs, args,
                                  [] if prev is None else [(prev, 0)])
    return pl.pallas_call(
        kern,
        grid=(nb, nc),
        in_specs=in_specs,
        out_specs=[rows(SSD_D_INNER),
                   pl.BlockSpec((1, SSD_HEADS, SSD_HEAD_DIM, SSD_STATE), lambda b, c: (b, 0, 0, 0))],
        out_shape=[jax.ShapeDtypeStruct((ta, SSD_D_INNER), F32),
                   jax.ShapeDtypeStruct((nb, SSD_HEADS, SSD_HEAD_DIM, SSD_STATE), F32)],
        scratch_shapes=[pltpu.VMEM((blk + 8, SSD_CONV_DIM), F32),
                        pltpu.VMEM((SSD_GROUPS, SSD_HEADS // SSD_GROUPS * SSD_HEAD_DIM, SSD_STATE), F32)],
        input_output_aliases=aliases,
        compiler_params=_params(2),
        name="ssd_q%d" % q,
    )(*args)


def _mlstm_kernel(p_ref, sm_ref, smt_ref, cinit_ref, ninit_ref, minit_ref, ib_c_ref, ib_r_ref, fb_c_ref, fb_r_ref,
                  g_ref, y_ref, cout_ref, nout_ref, mout_ref, c_sc, n_sc, m_sc, *, q, nsub, valid):
    c = pl.program_id(1)

    @pl.when(c == 0)
    def _():
        c_sc[...] = cinit_ref[...]
        n_sc[...] = ninit_ref[...]
        m_sc[...] = minit_ref[...]

    w = ML_WIDTH
    i0 = SSD_HEADS
    f0 = SSD_HEADS + ML_HEADS
    ig_c_all = sm_ref[:, i0:i0 + ML_HEADS] + ib_c_ref[...]
    fg_c = sm_ref[:, f0:f0 + ML_HEADS] + fb_c_ref[...]
    ig_r_all = smt_ref[0, i0:i0 + ML_HEADS, :] + ib_r_ref[...]
    fg_r = smt_ref[0, f0:f0 + ML_HEADS, :] + fb_r_ref[...]
    lf_c_all = -_softplus(-fg_c)
    lf_r_all = -_softplus(-fg_r)
    if valid < q:
        ok_c = lax.broadcasted_iota(jnp.int32, ig_c_all.shape, 0) < valid
        ok_r = lax.broadcasted_iota(jnp.int32, ig_r_all.shape, 1) < valid
        ig_c_all = jnp.where(ok_c, ig_c_all, MASKED)
        ig_r_all = jnp.where(ok_r, ig_r_all, MASKED)
        lf_c_all = jnp.where(ok_c, lf_c_all, 0.0)
        lf_r_all = jnp.where(ok_r, lf_r_all, 0.0)
    causal = _tri(q, True)
    lower = jnp.where(causal, 1.0, 0.0).astype(BF16)
    upper = jnp.where(_tri(q, False), 1.0, 0.0).astype(BF16)

    scale = ML_HEAD_DIM ** -0.5
    c_st = [c_sc[h] for h in range(ML_HEADS)]
    n_st = [n_sc[h] for h in range(ML_HEADS)]
    m_st = [m_sc[h][:, 0:1] for h in range(ML_HEADS)]
    heads = range(ML_HEADS)
    col = lambda ref_rows, base, h: p_ref[ref_rows, base + h * ML_HEAD_DIM:base + (h + 1) * ML_HEAD_DIM]
    pre = []
    for j in range(nsub):
        rs = slice(j * q, (j + 1) * q)
        b_c = _dot_exact_l(lower, lf_c_all[rs])
        b_r = _dot_exact_r(lf_r_all[:, rs], upper)
        ig_r = ig_r_all[:, rs]
        per_head = []
        for h in heads:
            bc = b_c[:, h:h + 1]
            intra = jnp.where(causal, bc - b_r[h:h + 1, :] + ig_r[h:h + 1, :], MASKED)
            qk = _dot_nt(col(rs, 0, h), col(rs, w, h) * scale)
            per_head.append((bc, intra, jnp.max(intra, axis=1, keepdims=True), qk))
        pre.append((rs, b_c, per_head))
    for j, (rs, b_c, per_head) in enumerate(pre):
        ig_c = ig_c_all[rs]
        mid = []
        for h in heads:
            bc, intra, row_max, qk = per_head[h]
            q_h, v_h = col(rs, 0, h), col(rs, 2 * w, h)
            inter = bc + m_st[h]
            m_t = jnp.maximum(inter, row_max)
            wgt = qk * jnp.exp(intra - m_t)
            w_inter = jnp.exp(inter - m_t)
            num = _dot(wgt, v_h) + w_inter * _dot(q_h, c_st[h])
            den = jnp.sum(wgt, axis=1, keepdims=True) + w_inter * jnp.sum(q_h * n_st[h], axis=1, keepdims=True)
            mid.append((m_t, num, den))
        last = []
        for h in heads:
            m_t, num, den = mid[h]
            hh = num / jnp.maximum(jnp.abs(den), jnp.exp(-m_t))
            last.append((hh, jnp.mean(hh * hh, axis=-1, keepdims=True)))
        ys = []
        for h in heads:
            hs = slice(h * ML_HEAD_DIM, (h + 1) * ML_HEAD_DIM)
            bc, m_t = per_head[h][0], mid[h][0]
            hh, ms = last[h]
            ys.append(_sigmoid(col(rs, 3 * w, h)) * (hh * lax.rsqrt(ms + RMS_EPS) * g_ref[:, hs]))
            m_new = m_t[q - 1:q, :]
            b_last = b_c[q - 1:q, h:h + 1]
            wk = jnp.exp(b_last - bc + ig_c[:, h:h + 1] - m_new)
            decay = jnp.exp(b_last + m_st[h] - m_new)
            kw = col(rs, w, h) * scale * wk
            c_st[h] = decay * c_st[h] + _dot_tn(kw, col(rs, 2 * w, h))
            n_st[h] = decay * n_st[h] + jnp.sum(kw, axis=0, keepdims=True)
            m_st[h] = m_new
        y_ref[rs, :] = jnp.concatenate(ys, axis=1)
    for h in range(ML_HEADS):
        c_sc[h] = c_st[h]
        n_sc[h] = n_st[h]
        m_sc[h] = jnp.broadcast_to(m_st[h], (1, ML_HEAD_DIM))

    @pl.when(c == pl.num_programs(1) - 1)
    def _():
        cout_ref[0] = c_sc[...]
        nout_ref[0] = n_sc[...]
        mout_ref[0] = m_sc[...]


def _mlstm_call(p_ml, p_small, p_small_t, c_init, n_init, m_init, sl, wts, l, *, nb, nc, q, nsub, valid, row0,
                prev):
    ta = p_ml.shape[0]
    blk = q * nsub
    assert row0 % blk == 0 and (valid == q or nsub == 1)
    rows, smt_spec = _seq_specs(blk, nc, row0 // blk, p_small_t)
    ws = tuple(wts[k] for k in ("ib_c", "ib_r", "fb_c", "fb_r", "mlstm_norm"))
    c_spec = pl.BlockSpec((1, ML_HEADS, ML_HEAD_DIM, ML_HEAD_DIM), lambda b, c: (b, 0, 0, 0))
    v_spec = pl.BlockSpec((1, ML_HEADS, 1, ML_HEAD_DIM), lambda b, c: (b, 0, 0, 0))
    in_specs = [rows(4 * ML_WIDTH), rows(SMALL_COLS), smt_spec, _state_spec(c_init, sl), _state_spec(n_init, sl),
                _state_spec(m_init, sl)]
    in_specs += [_layer(w, l) for w in ws]
    args = [p_ml, p_small, p_small_t, c_init, n_init, m_init, *ws]
    kern, aliases = _with_aliased(functools.partial(_mlstm_kernel, q=q, nsub=nsub, valid=valid), in_specs, args,
                                  [] if prev is None else [(prev, 0)])
    vec = jax.ShapeDtypeStruct((nb, ML_HEADS, 1, ML_HEAD_DIM), F32)
    return pl.pallas_call(
        kern,
        grid=(nb, nc),
        in_specs=in_specs,
        out_specs=[rows(ML_WIDTH), c_spec, v_spec, v_spec],
        out_shape=[jax.ShapeDtypeStruct((ta, ML_WIDTH), F32),
                   jax.ShapeDtypeStruct((nb, ML_HEADS, ML_HEAD_DIM, ML_HEAD_DIM), F32), vec, vec],
        scratch_shapes=[pltpu.VMEM((ML_HEADS, ML_HEAD_DIM, ML_HEAD_DIM), F32),
                        pltpu.VMEM((ML_HEADS, 1, ML_HEAD_DIM), F32),
                        pltpu.VMEM((ML_HEADS, 1, ML_HEAD_DIM), F32)],
        input_output_aliases=aliases,
        compiler_params=_params(2),
        name="mlstm_q%d" % q,
    )(*args)


def _s5_kernel(u_ref, h0_ref, btc_ref, smc_ref, cmc_ref, e_ref, pa_ref, pb_ref, va_ref, vb_ref,
               y_ref, hout_ref, tz_ref, sm_ref, cm_ref, *, rows, scan):
    half = S5_SUPER_STATE // 2
    q, lanes, ng = S5_CHUNK, S5_LANES, S5_SUPER_GROUPS

    @pl.when(pl.program_id(1) == 0)
    def _():
        def expand(compact):
            hi = compact.astype(BF16)
            lo = (compact - hi.astype(F32)).astype(BF16)
            full = (jnp.dot(hi, e_ref[...], preferred_element_type=F32)
                    + jnp.dot(lo, e_ref[...], preferred_element_type=F32))
            rg = (lax.broadcasted_iota(jnp.int32, full.shape, 0) >> 4) & (ng - 1)
            cg = (lax.broadcasted_iota(jnp.int32, full.shape, 1) >> 6) & (ng - 1)
            return jnp.where(rg == cg, full, 0.0)

        def split(x):
            hi = x.astype(BF16)
            return hi, (x - hi.astype(F32)).astype(BF16)

        for r0 in range(0, q * lanes, 4 * lanes):
            sm_ref[r0:r0 + 4 * lanes, :] = expand(smc_ref[r0:r0 + 4 * lanes, :]).astype(BF16)
        bt_hi, bt_lo = split(expand(btc_ref[...]))
        nt = lambda a, b: lax.dot_general(a, b, (((1,), (1,)), ((), ())), preferred_element_type=F32)
        zero = jnp.zeros((lanes, lanes), BF16)
        per = 4
        for lag0 in range(0, q + 1, per):
            nl = min(per, q + 1 - lag0)
            blk = expand(cmc_ref[lag0 * lanes:(lag0 + nl) * lanes, :])
            cm_ref[lag0 * lanes:(lag0 + nl) * lanes, :] = blk.astype(BF16)
            if lag0 >= q:
                continue
            c_hi, c_lo = split(blk)
            k_all = (nt(bt_hi, c_hi) + nt(bt_hi, c_lo) + nt(bt_lo, c_hi)).astype(BF16)
            for lag in range(lag0, min(lag0 + nl, q)):
                k_lag = k_all[:, (lag - lag0) * lanes:(lag - lag0 + 1) * lanes]
                for s in range(q - lag):
                    t = s + lag
                    tz_ref[s * lanes:(s + 1) * lanes, t * lanes:(t + 1) * lanes] = k_lag
                    if lag > 0:
                        tz_ref[t * lanes:(t + 1) * lanes, s * lanes:(s + 1) * lanes] = zero

    ucat = jnp.concatenate([u_ref[pl.ds(s, rows, stride=S5_CHUNK), :] for s in range(S5_CHUNK)], axis=1)
    ub = ucat.astype(BF16)

    def cmul(a, b, x):
        return a * x + b * pltpu.roll(x, half, axis=1)

    contrib = jnp.dot(ub, sm_ref[...], preferred_element_type=F32)
    h0 = h0_ref[0]
    carried = cmul(va_ref[...], vb_ref[...], h0)
    if scan:
        ridx = lax.broadcasted_iota(jnp.int32, (rows, S5_SUPER_STATE), 0)
        x = contrib + jnp.where(ridx == 0, carried, 0.0)
        k = 0
        while (1 << k) < rows:
            s = 1 << k
            shifted = jnp.where(ridx >= s, pltpu.roll(x, s, axis=0), 0.0)
            x = x + cmul(pa_ref[k:k + 1, :], pb_ref[k:k + 1, :], shifted)
            k += 1
        hprev = jnp.where(ridx == 0, h0, pltpu.roll(x, 1, axis=0))
        hout_ref[0] = x[rows - 1:rows, :]
    else:
        x = contrib + carried
        hprev = h0
        hout_ref[0] = x
    fold = q * lanes // 2
    y_intra = jnp.concatenate(
        [jnp.dot(ub[:, :fold], tz_ref[:fold, :fold], preferred_element_type=F32),
         jnp.dot(ub, tz_ref[:, fold:], preferred_element_type=F32)], axis=1)
    y = y_intra + _dot_nt(hprev, cm_ref[lanes:, :])
    for t in range(S5_CHUNK):
        y_ref[pl.ds(t, rows, stride=S5_CHUNK), :] = y[:, t * S5_LANES:(t + 1) * S5_LANES]


def _s5_call(p_s5, h0, tab, l, *, nseq, rows, scan, row0, prev):
    ta = p_s5.shape[0]
    blk_rows = rows * S5_CHUNK
    blk0 = row0 // blk_rows
    hrows = h0.shape[2]
    tok = pl.BlockSpec((blk_rows, S5_LANES), lambda sb, b: (blk0 + b, sb))
    hspec = pl.BlockSpec((None, 1, hrows, S5_SUPER_STATE), lambda sb, b: (sb, b, 0, 0))
    va, vb = (tab["va16"], tab["vb16"]) if scan else (tab["va"], tab["vb"])
    per_sb = lambda a: pl.BlockSpec((None, None) + a.shape[2:], lambda sb, b: (l, sb) + (0,) * (a.ndim - 2))
    tabs = [(tab["btc"], per_sb), (tab["smc16"] if scan else tab["smc"], per_sb), (tab["cmc"], per_sb),
            (tab["e"], lambda a: _full(a.shape)),
            (tab["pa"], per_sb), (tab["pb"], per_sb), (va, per_sb), (vb, per_sb)]
    in_specs = [tok, hspec] + [mk(a) for a, mk in tabs]
    args = [p_s5, h0] + [a for a, _ in tabs]
    kern, aliases = _with_aliased(functools.partial(_s5_kernel, rows=rows, scan=scan), in_specs, args,
                                  [] if prev is None else [(prev, 0)])
    folded = S5_CHUNK * S5_LANES
    return pl.pallas_call(
        kern,
        grid=(S5_SUPER, nseq),
        in_specs=in_specs,
        out_specs=[tok, hspec],
        out_shape=[jax.ShapeDtypeStruct((ta, S5_WIDTH), F32), jax.ShapeDtypeStruct(h0.shape, F32)],
        scratch_shapes=[pltpu.VMEM((folded, folded), BF16), pltpu.VMEM((folded, S5_SUPER_STATE), BF16),
                        pltpu.VMEM((folded + S5_LANES, S5_SUPER_STATE), BF16)],
        input_output_aliases=aliases,
        compiler_params=_params(2),
        name="s5_scan" if scan else "s5_step",
    )(*args)


def _s5_tables(a_re, a_im, log_dt, b_re, b_im, c_re, c_im, valid):
    q = S5_CHUNK
    depth = a_re.shape[0]
    nsb, ng = S5_SUPER, S5_SUPER_GROUPS
    dt = jnp.exp(log_dt)[..., None]
    mag = jnp.exp(a_re * dt)
    ab_re = mag * jnp.cos(a_im * dt)
    ab_im = mag * jnp.sin(a_im * dt)
    inv = 1.0 / (a_re * a_re + a_im * a_im)
    co_re = ((ab_re - 1.0) * a_re + ab_im * a_im) * inv
    co_im = (ab_im * a_re - (ab_re - 1.0) * a_im) * inv

    def cmul(x, y):
        return x[0] * y[0] - x[1] * y[1], x[0] * y[1] + x[1] * y[0]

    pw = [(jnp.ones_like(ab_re), jnp.zeros_like(ab_re))]
    for _ in range(q):
        pw.append(cmul(pw[-1], (ab_re, ab_im)))
    row = lambda v: v[:, :, None, :]
    bt_t = (jnp.transpose(b_re, (0, 1, 3, 2)), jnp.transpose(b_im, (0, 1, 3, 2)))
    bt_re = row(co_re) * bt_t[0] - row(co_im) * bt_t[1]
    bt_im = row(co_re) * bt_t[1] + row(co_im) * bt_t[0]

    def compact(blocks):
        x = jnp.stack(blocks, axis=2)
        x = x.reshape(depth, nsb, ng, len(blocks), S5_GROUP, 2 * S5_STATE)
        return jnp.transpose(x, (0, 1, 3, 2, 4, 5)).reshape(depth, nsb, len(blocks) * S5_LANES, 2 * S5_STATE)

    def times(p):
        pr, pi = row(p[0]), row(p[1])
        b = jnp.concatenate([pr * bt_re - pi * bt_im, pr * bt_im + pi * bt_re], axis=-1)
        c = jnp.concatenate([c_re * pr - c_im * pi, -(c_re * pi + c_im * pr)], axis=-1)
        return b, c

    def state_in(nvalid):
        zero = jnp.zeros((depth, S5_GROUPS, S5_GROUP, 2 * S5_STATE), F32)
        return compact([times(pw[nvalid - 1 - s])[0] if s < nvalid else zero for s in range(q)])

    btc = compact([times(pw[0])[0]])
    cmc = compact([times(p)[1] for p in pw])
    e = np.zeros((2, S5_STATE, 2, ng, S5_STATE), np.float32)
    for g in range(ng):
        e[:, :, :, g, :] = np.eye(2 * S5_STATE).reshape(2, S5_STATE, 2, S5_STATE)
    e = jnp.asarray(e.reshape(2 * S5_STATE, S5_SUPER_STATE), BF16)

    def packed(p):
        pr = p[0].reshape(depth, nsb, ng * S5_STATE)
        pi = p[1].reshape(depth, nsb, ng * S5_STATE)
        return jnp.concatenate([pr, pr], axis=-1), jnp.concatenate([-pi, pi], axis=-1)

    doubling = [pw[q]]
    for _ in range(S5_DOUBLINGS - 1):
        doubling.append(cmul(doubling[-1], doubling[-1]))
    pa = jnp.stack([packed(p)[0] for p in doubling], axis=2)
    pb = jnp.stack([packed(p)[1] for p in doubling], axis=2)
    va16, vb16 = packed(pw[q])
    va, vb = packed(pw[valid])
    ex = lambda a: a[:, :, None, :]
    return {"btc": btc, "smc16": state_in(q), "smc": state_in(valid), "cmc": cmc, "e": e,
            "pa": pa, "pb": pb, "va16": ex(va16), "vb16": ex(vb16), "va": ex(va), "vb": ex(vb)}


def _t5_bucket(dist):
    dist = np.asarray(dist)
    large = REL_MAX_EXACT + (np.log(np.maximum(dist, 1) / REL_MAX_EXACT)
                             / math.log(REL_MAX_DIST / REL_MAX_EXACT)
                             * (REL_BUCKETS - REL_MAX_EXACT)).astype(np.int32)
    large = np.minimum(large, REL_BUCKETS - 1)
    return np.where(dist < REL_MAX_EXACT, dist, large).astype(np.int32)


def _bias_steps(rel_bias, grp, dil):
    heads = slice(grp * SWA_GROUP_HEADS, (grp + 1) * SWA_GROUP_HEADS)
    buckets = _t5_bucket(dil * np.arange(SWA_BLOCK + 1))
    onehot = np.zeros((SWA_BLOCK + 1, REL_BUCKETS), np.float32)
    onehot[np.arange(SWA_BLOCK + 1), buckets] = 1.0
    steps = jnp.einsum("jb,bh->hj", onehot, rel_bias[:, heads], precision=lax.Precision.HIGHEST)
    return steps.astype(F32)


def _softmax_pieces(logits):
    m = logits[0].max(axis=1, keepdims=True)
    for s in logits[1:]:
        m = jnp.maximum(m, s.max(axis=1, keepdims=True))
    ps = [jnp.exp(s - m) for s in logits]
    den = ps[0].sum(axis=1, keepdims=True)
    for p in ps[1:]:
        den = den + p.sum(axis=1, keepdims=True)
    return ps, den, m + jnp.log(den)


SWA_SAMPLE_LANES = 2048
SWA_UNROLL = 8
SWA_DENSE_BLOCKS = 8


def _swa_prompt_kernel(q_ref, k_ref, v_ref, kp_ref, vp_ref, bias_ref, o_ref, lse_ref, kvt_ref, *, dil):
    n = SWA_BLOCK
    pair = pl.program_id(1)
    first = pl.program_id(2) == 0
    scale = SWA_HEAD_DIM ** -0.5

    ones = jnp.ones((n, SWA_HEAD_DIM), BF16)

    def logits(sl, kp, vp, mask_prev):
        qq, kk = q_ref[sl, :], k_ref[sl, :]
        heads = []
        for j in range(2):
            hs = slice(j * SWA_HEAD_DIM, (j + 1) * SWA_HEAD_DIM)
            bias = bias_ref[2 * pair + j]
            bias_prev = bias[:, 0:n] if mask_prev is None else jnp.where(mask_prev, MASKED, bias[:, 0:n])
            q_h = qq[:, hs] * scale
            s_cur = _dot_nt(q_h, kk[:, hs]) + bias[:, n:2 * n]
            s_prev = _dot_nt(q_h, kp[:, hs]) + bias_prev
            heads.append((s_cur, s_prev, jnp.max(jnp.maximum(s_cur, s_prev), axis=1, keepdims=True)))
        return sl, vp, heads

    def finish(sl, vp, heads):
        vv = v_ref[sl, :]
        outs, lses = [], []
        for j, (s_cur, s_prev, m) in enumerate(heads):
            hs = slice(j * SWA_HEAD_DIM, (j + 1) * SWA_HEAD_DIM)
            p_cur = jnp.exp(s_cur - m).astype(BF16)
            p_prev = jnp.exp(s_prev - m).astype(BF16)
            den = jnp.dot(p_cur, ones, preferred_element_type=F32) + jnp.dot(p_prev, ones, preferred_element_type=F32)
            outs.append((_dot(p_cur, vv[:, hs]) + _dot(p_prev, vp[:, hs])) / den)
            lses.append(m + jnp.log(den))
        o_ref[sl, :] = jnp.concatenate(outs, axis=1)
        lse_ref[sl, :] = jnp.concatenate(lses, axis=1)

    def run(tiles):
        for t in [logits(*a) for a in tiles]:
            finish(*t)

    if dil == 1:
        tiles = []
        for j in range(SWA_DENSE_BLOCKS):
            if j == 0:
                tiles.append((slice(0, n), kp_ref[...], vp_ref[...], first))
            else:
                before = slice((j - 1) * n, j * n)
                tiles.append((slice(j * n, (j + 1) * n), k_ref[before, :], v_ref[before, :], None))
        run(tiles)
    else:
        unroll = min(dil, SWA_UNROLL)

        def group(i, carry):
            tiles = []
            for u in range(unroll):
                sl = pl.ds(i * unroll + u, n, stride=dil)
                tiles.append((sl, kp_ref[sl, :], vp_ref[sl, :], first))
            run(tiles)
            return carry
        if dil == unroll:
            group(0, 0)
        else:
            lax.fori_loop(0, dil // unroll, group, 0)
    rows = k_ref.shape[0]
    kvt_ref[0] = k_ref[rows - n * dil:rows, :].T
    kvt_ref[1] = v_ref[rows - n * dil:rows, :].T


def _swa_prompt_call(p_swa, bias, kvt_prev, l, depth, *, grp, dil, nb, seq):
    ta = p_swa.shape[0]
    win = SWA_BLOCK * dil
    sb = SWA_BLOCK * SWA_DENSE_BLOCKS if dil == 1 else win
    assert dil == 1 or dil % min(dil, SWA_UNROLL) == 0
    nsb = seq // sb
    lanes = 2 * SWA_HEAD_DIM
    npair = SWA_GROUP_HEADS // 2
    cur = lambda col: pl.BlockSpec((sb, lanes), lambda b, p, c: (b * nsb + c, 2 * col + p))
    if dil == 1:
        per = sb // SWA_BLOCK
        prv = lambda col: pl.BlockSpec(
            (SWA_BLOCK, lanes), lambda b, p, c: (jnp.maximum((b * nsb + c) * per - 1, 0), 2 * col + p))
    else:
        prv = lambda col: pl.BlockSpec((sb, lanes), lambda b, p, c: (b * nsb + jnp.maximum(c - 1, 0), 2 * col + p))
    out = pl.BlockSpec((sb, lanes), lambda b, p, c: (b * nsb + c, p))
    kvt_spec = pl.BlockSpec((None, None, 2, None, lanes, win), lambda b, p, c: (l, b, 0, p, 0, 0))
    shape = jax.ShapeDtypeStruct((ta, SWA_GROUP_WIDTH), F32)
    in_specs = [cur(grp), cur(3 + grp), cur(6 + grp), prv(3 + grp), prv(6 + grp), _full(bias.shape)]
    args = [p_swa, p_swa, p_swa, p_swa, p_swa, bias]
    kern, aliases = _with_aliased(functools.partial(_swa_prompt_kernel, dil=dil), in_specs, args,
                                  [] if kvt_prev is None else [(kvt_prev, 2)])
    return pl.pallas_call(
        kern,
        grid=(nb, npair, nsb),
        in_specs=in_specs,
        out_specs=[out, out, kvt_spec],
        out_shape=[shape, shape, jax.ShapeDtypeStruct((depth, nb, 2, npair, lanes, win), F32)],
        input_output_aliases=aliases,
        compiler_params=_params(3),
        name="swa_prompt_d%d" % dil,
    )(*args)


def _swa_prompt_bias(steps):
    n = SWA_BLOCK
    period = 3 * n + 1
    f = jnp.concatenate([steps[:, ::-1], jnp.full((steps.shape[0], period - (n + 1)), MASKED, F32)], axis=1)
    tiled = jnp.tile(f, (1, n))[:, :n * (period - 1)]
    return tiled.reshape(steps.shape[0], n, period - 1)[:, :, :2 * n]


def _swa_sample_kernel(q_ref, k_ref, v_ref, buf_ref, bias_buf_ref, bias_new_ref, o_ref, lse_ref, cache_ref, *,
                       width, t_new):
    scale = SWA_HEAD_DIM ** -0.5
    w = SWA_GROUP_WIDTH
    for i in range(buf_ref.shape[0]):
        rs = slice(i * SAMPLE_ROWS, (i + 1) * SAMPLE_ROWS)
        qq, kn, vn = q_ref[rs, :], k_ref[rs, :], v_ref[rs, :]
        outs, lses = [], []
        for h in range(SWA_GROUP_HEADS):
            hs = slice(h * SWA_HEAD_DIM, (h + 1) * SWA_HEAD_DIM)
            k_t = buf_ref[i, h * SWA_HEAD_DIM:(h + 1) * SWA_HEAD_DIM, :]
            v_t = buf_ref[i, w + h * SWA_HEAD_DIM:w + (h + 1) * SWA_HEAD_DIM, :]
            q_h = qq[:, hs] * scale
            ps, den, lse = _softmax_pieces([_dot(q_h, k_t) + bias_buf_ref[h],
                                            _dot_nt(q_h, kn[:, hs]) + bias_new_ref[h]])
            outs.append((_dot_nt(ps[0], v_t) + _dot(ps[1], vn[:, hs])) / den)
            lses.append(jnp.broadcast_to(lse, (SAMPLE_ROWS, SWA_HEAD_DIM)))
        o_ref[rs, :] = jnp.concatenate(outs, axis=1)
        lse_ref[rs, :] = jnp.concatenate(lses, axis=1)
        new_t = jnp.concatenate([kn, vn], axis=1).T
        cache_ref[i] = pltpu.roll(buf_ref[i], width - t_new, axis=1)
        cache_ref[i, :, width - t_new:width] = new_t[:, 0:t_new]


def _swa_sample_call(p_swa, cache_t, bias_buf, bias_new, prev_o, prev_lse, prev_cache, l, *, grp, nb, row0, t_new):
    ta = p_swa.shape[0]
    width = cache_t.shape[3]
    w = SWA_GROUP_WIDTH
    per = max(1, min(nb, SWA_SAMPLE_LANES // width))
    while nb % per:
        per -= 1
    assert row0 % (SAMPLE_ROWS * per) == 0
    blk0 = row0 // (SAMPLE_ROWS * per)
    nb = nb // per
    tok = lambda col: pl.BlockSpec((SAMPLE_ROWS * per, w), lambda b: (blk0 + b, col))
    cache_spec = pl.BlockSpec((None, per, 2 * w, width), lambda b: (l, b, 0, 0))
    in_specs = [tok(grp), tok(3 + grp), tok(6 + grp), cache_spec, _full(bias_buf.shape), _full(bias_new.shape)]
    args = [p_swa, p_swa, p_swa, cache_t, bias_buf, bias_new]
    aliased = [(prev_o, 0), (prev_lse, 1)] + ([] if prev_cache is None else [(prev_cache, 2)])
    kern, aliases = _with_aliased(functools.partial(_swa_sample_kernel, width=width, t_new=t_new), in_specs, args,
                                  aliased)
    return pl.pallas_call(
        kern,
        grid=(nb,),
        in_specs=in_specs,
        out_specs=[tok(0), tok(0), cache_spec],
        out_shape=[jax.ShapeDtypeStruct((ta, w), F32), jax.ShapeDtypeStruct((ta, w), F32),
                   jax.ShapeDtypeStruct(cache_t.shape, F32)],
        input_output_aliases=aliases,
        compiler_params=_params(1),
        name="swa_sample_w%d" % width,
    )(*args)


def _swa_sample_bias(steps, window, dil, width, t_new):
    n = window // dil
    nh = steps.shape[0]
    g = jnp.pad(steps[:, :, None], ((0, 0), (0, 0), (0, dil - 1)), constant_values=MASKED).reshape(nh, (n + 1) * dil)
    g = jnp.pad(g, ((0, 0), (0, width + SAMPLE_ROWS)), constant_values=MASKED)
    masked_row = jnp.full((nh, 1, width), MASKED, F32)
    rows = [g[:, t + 1:t + 1 + width][:, None, ::-1] if t < t_new else masked_row for t in range(SAMPLE_ROWS)]
    b_buf = jnp.concatenate(rows, axis=1)
    new_rows = []
    for t in range(SAMPLE_ROWS):
        if t < t_new:
            row = jnp.concatenate([g[:, 0:t + 1][:, ::-1], jnp.full((nh, SAMPLE_ROWS - t - 1), MASKED, F32)], axis=1)
        else:
            row = jnp.full((nh, SAMPLE_ROWS), MASKED, F32)
        new_rows.append(row[:, None, :])
    return b_buf, jnp.concatenate(new_rows, axis=1)


def _merge_kernel(x_ref, yssd_ref, o0_ref, l0_ref, o1_ref, l1_ref, o2_ref, l2_ref, ys5_ref, u_ref, yml_ref,
                  g_ref, wg01_ref, wg23_ref, wssd_ref, wswa_ref, ws5_ref, wml_ref, wglu_ref, wout_ref, d_ref, out_ref):
    for rs in _sub_tiles(x_ref.shape[0]):
        x = x_ref[rs, :]
        h = _rms(x, g_ref[...]).astype(BF16)
        l0, l1, l2 = l0_ref[rs, :], l1_ref[rs, :], l2_ref[rs, :]
        m = jnp.maximum(jnp.maximum(l0, l1), l2)
        e0, e1, e2 = jnp.exp(l0 - m), jnp.exp(l1 - m), jnp.exp(l2 - m)
        y_swa = (e0 * o0_ref[rs, :] + e1 * o1_ref[rs, :] + e2 * o2_ref[rs, :]) / (e0 + e1 + e2)
        y5 = ys5_ref[rs, :] + d_ref[...] * u_ref[rs, :]
        y_s5 = y5 * _sigmoid(_dot(y5, wglu_ref[...]))
        branches = ((yssd_ref[rs, :], wssd_ref), (y_swa, wswa_ref), (y_s5, ws5_ref), (yml_ref[rs, :], wml_ref))
        merged = None
        for i, (y, w_ref) in enumerate(branches):
            wg_ref = wg01_ref if i < 2 else wg23_ref
            gate = _sigmoid(jnp.dot(h, wg_ref[:, (i % 2) * D_MODEL:(i % 2 + 1) * D_MODEL],
                                    preferred_element_type=F32))
            term = gate * _dot(y, w_ref[...])
            merged = term if merged is None else merged + term
        out_ref[rs, :] = x + _dot(merged, wout_ref[...])


def _merge_call(x, acts, wts, l):
    ta = x.shape[0]
    tm = TOKEN_TILE
    row = lambda a: pl.BlockSpec((tm, a.shape[1]), lambda i: (i, 0))
    ws = tuple(wts[k] for k in ("w_br_ssd", "w_br_swa", "w_br_s5", "w_br_mlstm", "s5_w_glu", "w_out", "s5_d"))
    half = GATE_COLS // 2
    gate = lambda k: pl.BlockSpec((None, D_MODEL, half), lambda i: (l, 0, MAIN_COLS // half + k),
                                  pipeline_mode=pl.Buffered(1))
    w_all = wts["w_main"]
    return pl.pallas_call(
        _merge_kernel,
        grid=(ta // tm,),
        in_specs=[row(a) for a in (x, *acts)] + [_layer(wts["norm_mix"], l), gate(0), gate(1)]
        + [_layer(w, l) for w in ws],
        out_specs=row(x),
        out_shape=jax.ShapeDtypeStruct(x.shape, F32),
        compiler_params=_params(1),
        name="merge",
    )(x, *acts, wts["norm_mix"], w_all, w_all, *ws)


def _norm_matmul_kernel(x_ref, g_ref, w_ref, o_ref):
    o_ref[...] = jnp.dot(_rms(x_ref[...], g_ref[...]).astype(BF16), w_ref[...], preferred_element_type=F32)


def _norm_matmul(x, g, w, l, tm):
    rows = x.shape[0]
    return pl.pallas_call(
        _norm_matmul_kernel,
        grid=(rows // tm,),
        in_specs=[pl.BlockSpec((tm, x.shape[1]), lambda i: (i, 0)), _layer(g, l), _layer(w, l)],
        out_specs=pl.BlockSpec((tm, w.shape[2]), lambda i: (i, 0)),
        out_shape=jax.ShapeDtypeStruct((rows, w.shape[2]), F32),
        compiler_params=_params(1),
        name="norm_matmul",
    )(x, g, w)


def _xattn_heads(qs, kv_ref, kv_rows):
    scale = XA_HEAD_DIM ** -0.5
    n_mem = kv_ref.shape[0] // (2 * XA_HEADS) if kv_rows else kv_ref.shape[0]
    scored = []
    for q in qs:
        for h in range(XA_HEADS):
            hs = slice(h * XA_HEAD_DIM, (h + 1) * XA_HEAD_DIM)
            k_h = kv_ref[h * n_mem:(h + 1) * n_mem, :] if kv_rows else kv_ref[:, hs]
            logits = _dot_nt(q[:, hs], k_h) * scale
            scored.append((logits, logits.max(axis=1, keepdims=True)))
    outs = []
    for i, (logits, m) in enumerate(scored):
        h = i % XA_HEADS
        if kv_rows:
            v_h = kv_ref[(XA_HEADS + h) * n_mem:(XA_HEADS + h + 1) * n_mem, :]
        else:
            v_h = kv_ref[:, D_MODEL + h * XA_HEAD_DIM:D_MODEL + (h + 1) * XA_HEAD_DIM]
        p = jnp.exp(logits - m)
        outs.append(_dot(p / p.sum(axis=1, keepdims=True), v_h))
    return [jnp.concatenate(outs[i * XA_HEADS:(i + 1) * XA_HEADS], axis=1) for i in range(len(qs))]


def _xattn_prompt_kernel(x_ref, kv_ref, g_ref, wq_ref, wo_ref, out_ref):
    subs = _sub_tiles(x_ref.shape[0])
    xs = [x_ref[rs, :] for rs in subs]
    qs = [jnp.dot(_rms(x, g_ref[...]).astype(BF16), wq_ref[...], preferred_element_type=F32) for x in xs]
    for rs, x, a in zip(subs, xs, _xattn_heads(qs, kv_ref, False)):
        out_ref[rs, :] = x + _dot(a, wo_ref[...])


def _xattn_prompt_call(x, kv, wts, l, *, n_tiles, tiles_per_seq):
    ta = x.shape[0]
    tm = TOKEN_TILE
    row = pl.BlockSpec((tm, D_MODEL), lambda i: (i, 0))
    ws = tuple(wts[k] for k in ("norm_xa", "xa_wq", "xa_wo"))
    return pl.pallas_call(
        _xattn_prompt_kernel,
        grid=(n_tiles,),
        in_specs=[row, pl.BlockSpec((None,) + kv.shape[1:], lambda i: (i // tiles_per_seq, 0, 0))]
        + [_layer(w, l) for w in ws],
        out_specs=row,
        out_shape=jax.ShapeDtypeStruct((ta, D_MODEL), F32),
        compiler_params=_params(1),
        name="xattn_prompt",
    )(x, kv, *ws)


def _xattn_sample_kernel(x_ref, kv_ref, g_ref, wq_ref, wo_ref, out_ref, q_sc, o_sc):
    b = pl.program_id(0)

    @pl.when(b == 0)
    def _():
        q_sc[...] = jnp.dot(_rms(x_ref[...], g_ref[...]).astype(BF16), wq_ref[...], preferred_element_type=F32)

    rows = pl.ds(pl.multiple_of(b * SAMPLE_ROWS, SAMPLE_ROWS), SAMPLE_ROWS)
    o_sc[rows, :] = _xattn_heads([q_sc[rows, :]], kv_ref, True)[0]

    @pl.when(b == pl.num_programs(0) - 1)
    def _():
        out_ref[...] = x_ref[...] + _dot(o_sc[...], wo_ref[...])


def _xattn_sample_call(x, kv, wts, l, *, nb, row0, prev):
    ta = x.shape[0]
    n_s = nb * SAMPLE_ROWS
    rows = pl.BlockSpec((n_s, D_MODEL), lambda b: (row0 // n_s, 0))
    ws = tuple(wts[k] for k in ("norm_xa", "xa_wq", "xa_wo"))
    in_specs = [rows, pl.BlockSpec((None, None) + kv.shape[2:], lambda b: (l, b, 0, 0))] + [_layer(w, l) for w in ws]
    args = [x, kv, *ws]
    kern, aliases = _with_aliased(_xattn_sample_kernel, in_specs, args, [(prev, 0)])
    return pl.pallas_call(
        kern,
        grid=(nb,),
        in_specs=in_specs,
        out_specs=rows,
        out_shape=jax.ShapeDtypeStruct((ta, D_MODEL), F32),
        scratch_shapes=[pltpu.VMEM((n_s, D_MODEL), F32), pltpu.VMEM((n_s, D_MODEL), F32)],
        input_output_aliases=aliases,
        compiler_params=_params(1),
        name="xattn_sample",
    )(*args)


def _ffn_kernel(x_ref, g_ref, w1_ref, w2_ref, out_ref):
    for rs in _sub_tiles(x_ref.shape[0]):
        x = x_ref[rs, :]
        a = jnp.dot(_rms(x, g_ref[...]).astype(BF16), w1_ref[...], preferred_element_type=F32)
        a = jnp.square(jnp.maximum(a, 0.0))
        out_ref[rs, :] = x + _dot(a, w2_ref[...])


def _ffn_call(x, wts, l):
    ta = x.shape[0]
    tm = TOKEN_TILE
    row = pl.BlockSpec((tm, D_MODEL), lambda i: (i, 0))
    ws = tuple(wts[k] for k in ("norm_mlp", "w_ff1", "w_ff2"))
    return pl.pallas_call(
        _ffn_kernel,
        grid=(ta // tm,),
        in_specs=[row] + [_layer(w, l) for w in ws],
        out_specs=row,
        out_shape=jax.ShapeDtypeStruct(x.shape, F32),
        compiler_params=_params(1),
        name="ffn",
    )(x, *ws)


def _ffn_final_kernel(x_ref, g_ref, w1_ref, w2_ref, gf_ref, yp_ref, ys_ref, *, prompt_tiles):
    i = pl.program_id(0)
    ys = []
    for rs in _sub_tiles(x_ref.shape[0]):
        x = x_ref[rs, :]
        a = jnp.dot(_rms(x, g_ref[...]).astype(BF16), w1_ref[...], preferred_element_type=F32)
        a = jnp.square(jnp.maximum(a, 0.0))
        ys.append(_rms(x + _dot(a, w2_ref[...]), gf_ref[...]))
    y = jnp.concatenate(ys, axis=0)

    @pl.when(i < prompt_tiles)
    def _():
        yp_ref[...] = y

    @pl.when(i >= prompt_tiles)
    def _():
        ys_ref[...] = y


def _ffn_final_call(x, wts, l, g_final, n_p):
    ta = x.shape[0]
    tm = TOKEN_TILE
    n_pt = n_p // tm
    assert n_p % tm == 0 and ta - n_p == tm
    row = pl.BlockSpec((tm, D_MODEL), lambda i: (i, 0))
    ws = tuple(wts[k] for k in ("norm_mlp", "w_ff1", "w_ff2"))
    return pl.pallas_call(
        functools.partial(_ffn_final_kernel, prompt_tiles=n_pt),
        grid=(ta // tm,),
        in_specs=[row] + [_layer(w, l) for w in ws] + [_full(g_final.shape)],
        out_specs=[pl.BlockSpec((tm, D_MODEL), lambda i: (jnp.minimum(i, n_pt - 1), 0)),
                   pl.BlockSpec((tm, D_MODEL), lambda i: (0, 0))],
        out_shape=[jax.ShapeDtypeStruct((n_p, D_MODEL), F32), jax.ShapeDtypeStruct((tm, D_MODEL), F32)],
        compiler_params=_params(1),
        name="ffn_final",
    )(x, *ws, g_final)


def _prep_weights(w):
    o = IN_OFFS
    w_in = w["w_in"]
    small = jnp.concatenate([w_in[:, :, o[2]:o[3]], w_in[:, :, o[11]:o[13]]], axis=2)
    row = lambda v: v[:, None, :].astype(F32)
    col = lambda v: v[:, :, None].astype(F32)
    bf = lambda v: v.astype(BF16)
    return {
        "norm_mix": row(w["norm_mix"]),
        "w_main": jnp.concatenate([w_in[:, :, o[0]:o[2]], w_in[:, :, o[3]:o[11]], w_in[:, :, o[13]:o[14]]],
                                  axis=2).astype(BF16),
        "w_small": jnp.pad(small, ((0, 0), (0, 0), (0, SMALL_COLS - small.shape[2]))).astype(BF16),
        "w_small_t": jnp.transpose(small, (0, 2, 1)).astype(BF16),
        "ssd_conv_w": w["ssd_conv_w"], "ssd_conv_b": row(w["ssd_conv_b"]),
        "dtb_c": row(w["ssd_dt_bias"]), "dtb_r": col(w["ssd_dt_bias"]),
        "alog_c": row(w["ssd_a_log"]), "alog_r": col(w["ssd_a_log"]),
        "ssd_dskip": row(jnp.repeat(w["ssd_d"], SSD_HEAD_DIM, axis=1)), "ssd_norm": row(w["ssd_norm"]),
        "ib_c": row(w["mlstm_i_bias"]), "ib_r": col(w["mlstm_i_bias"]),
        "fb_c": row(w["mlstm_f_bias"]), "fb_r": col(w["mlstm_f_bias"]),
        "mlstm_norm": row(w["mlstm_norm"]),
        "s5_d": row(w["s5_d"]), "s5_w_glu": bf(w["s5_w_glu"]),
        "w_br_ssd": bf(w["w_br_ssd"]), "w_br_swa": bf(w["w_br_swa"]), "w_br_s5": bf(w["w_br_s5"]),
        "w_br_mlstm": bf(w["w_br_mlstm"]), "w_out": bf(w["w_out"]),
        "norm_xa": row(w["norm_xa"]), "xa_wq": bf(w["xa_wq"]), "xa_wo": bf(w["xa_wo"]),
        "norm_mem": row(w["norm_mem"]),
        "xa_wkv": jnp.concatenate([w["xa_wk"], w["xa_wv"]], axis=2).astype(BF16),
        "norm_mlp": row(w["norm_mlp"]), "w_ff1": bf(w["w_ff1"]), "w_ff2": bf(w["w_ff2"]),
    }


def kernel(x_prompt, x_sample, state_ssd, state_ssd_conv, cache_swa_w128, cache_swa_w512, cache_swa_w2048, state_s5, state_mlstm_c, state_mlstm_n, state_mlstm_m, cache_mem_kv, mem_prompt, norm_mix, w_in, ssd_conv_w, ssd_conv_b, ssd_dt_bias, ssd_a_log, ssd_d, ssd_norm, rel_bias, s5_a_re, s5_a_im, s5_log_dt, s5_b_re, s5_b_im, s5_c_re, s5_c_im, s5_d, s5_w_glu, mlstm_i_bias, mlstm_f_bias, mlstm_norm, w_br_ssd, w_br_swa, w_br_s5, w_br_mlstm, w_out, norm_xa, norm_mem, xa_wq, xa_wk, xa_wv, xa_wo, norm_mlp, w_ff1, w_ff2, norm_final):
    wts = _prep_weights(dict(
        norm_mix=norm_mix, w_in=w_in, ssd_conv_w=ssd_conv_w, ssd_conv_b=ssd_conv_b, ssd_dt_bias=ssd_dt_bias,
        ssd_a_log=ssd_a_log, ssd_d=ssd_d, ssd_norm=ssd_norm, s5_d=s5_d, s5_w_glu=s5_w_glu,
        mlstm_i_bias=mlstm_i_bias, mlstm_f_bias=mlstm_f_bias, mlstm_norm=mlstm_norm, w_br_ssd=w_br_ssd,
        w_br_swa=w_br_swa, w_br_s5=w_br_s5, w_br_mlstm=w_br_mlstm, w_out=w_out, norm_xa=norm_xa, norm_mem=norm_mem,
        xa_wq=xa_wq, xa_wk=xa_wk, xa_wv=xa_wv, xa_wo=xa_wo, norm_mlp=norm_mlp, w_ff1=w_ff1, w_ff2=w_ff2))
    nb_p, seq, _ = x_prompt.shape
    nb_s, t_new, _ = x_sample.shape
    depth = w_in.shape[0]
    n_mem = mem_prompt.shape[1]
    rows_s = SAMPLE_ROWS
    n_p = nb_p * seq
    n_s = nb_s * rows_s
    ta = n_p + n_s
    assert seq % (SWA_BLOCK * SWA_PATTERN[-1][1]) == 0 and ta % TOKEN_TILE == 0 and t_new <= rows_s
    assert n_p % n_s == 0 and (seq // S5_CHUNK) & (seq // S5_CHUNK - 1) == 0
    caches = (cache_swa_w128, cache_swa_w512, cache_swa_w2048)

    xs_pad = jnp.pad(x_sample, ((0, 0), (0, rows_s - t_new), (0, 0))).reshape(n_s, D_MODEL)
    x = jnp.concatenate([x_prompt.reshape(n_p, D_MODEL), xs_pad], axis=0)

    steps = [_bias_steps(rel_bias, g, dil) for g, (_, dil) in enumerate(SWA_PATTERN)]
    prompt_bias = [_swa_prompt_bias(s) for s in steps]
    sample_bias = [_swa_sample_bias(steps[g], win, dil, caches[g].shape[2], t_new)
                   for g, (win, dil) in enumerate(SWA_PATTERN)]
    caches_t = [jnp.transpose(c, (0, 1, 3, 4, 5, 2)).reshape(depth, nb_s, 2 * SWA_GROUP_WIDTH, c.shape[2])
                for c in caches]
    mem_kv_s = jnp.transpose(cache_mem_kv, (0, 1, 3, 4, 2, 5)).reshape(depth, nb_s, 2 * XA_HEADS * n_mem, XA_HEAD_DIM)
    s5_tab = _s5_tables(s5_a_re, s5_a_im, s5_log_dt, s5_b_re, s5_b_im, s5_c_re, s5_c_im, t_new)
    s5_h0_s = jnp.transpose(state_s5.reshape(depth, nb_s, S5_SUPER, S5_SUPER_GROUPS, S5_STATE, 2),
                            (0, 2, 1, 5, 3, 4)).reshape(depth, S5_SUPER, 1, nb_s, S5_SUPER_STATE)

    zeros = lambda *s: jnp.zeros(s, F32)
    zero_conv = zeros(1, nb_p, 8, SSD_CONV_DIM)
    zero_ssd = zeros(1, nb_p, SSD_HEADS, SSD_HEAD_DIM, SSD_STATE)
    zero_c = zeros(1, nb_p, ML_HEADS, ML_HEAD_DIM, ML_HEAD_DIM)
    zero_vec = zeros(1, nb_p, ML_HEADS, 1, ML_HEAD_DIM)
    zero_s5 = zeros(S5_SUPER, nb_p, 1, S5_SUPER_STATE)
    conv_init_s = jnp.pad(state_ssd_conv, ((0, 0), (0, 0), (8 - (SSD_CONV - 1), 0), (0, 0)))
    n_init_s = state_mlstm_n[:, :, :, None, :]
    m_init_s = jnp.broadcast_to(state_mlstm_m[:, :, :, None, None], (depth, nb_s, ML_HEADS, 1, ML_HEAD_DIM))

    outs = {k: [] for k in ("ssd_p", "ssd_s", "conv_p", "conv_s", "s5_p", "s5_s", "c_p", "c_s", "n_p", "n_s",
                            "m_p", "m_s", "kv_p")}
    kvt_p = [None] * len(SWA_PATTERN)
    cache_out = [None] * len(SWA_PATTERN)
    ssd_q, ml_q, nsub = 128, 128, 4
    nchunk = seq // S5_CHUNK

    for l in range(depth):
        p_ssd, p_swa, p_s5, p_ml, p_small, p_small_t = _inproj(x, wts, l)
        small_t_s = jnp.transpose(p_small[n_p:, :SMALL_ROWS].reshape(nb_s, rows_s, SMALL_ROWS), (0, 2, 1))

        y_ssd, st_p = _ssd_call(p_ssd, p_small, p_small_t, zero_conv, zero_ssd, 0, wts, l, nb=nb_p,
                                nc=seq // (ssd_q * nsub), q=ssd_q, nsub=nsub, valid=ssd_q, row0=0, prev=None)
        y_ssd, st_s = _ssd_call(p_ssd, p_small, small_t_s, conv_init_s, state_ssd, l, wts, l,
                                nb=nb_s, nc=1, q=rows_s, nsub=1, valid=t_new, row0=n_p, prev=y_ssd)
        outs["ssd_p"].append(st_p)
        outs["ssd_s"].append(st_s)
        tail = SSD_CONV - 1
        xbc_s = p_ssd[n_p:, SSD_D_INNER:].reshape(nb_s, rows_s, SSD_CONV_DIM)[:, :t_new]
        outs["conv_p"].append(jnp.stack([p_ssd[(b + 1) * seq - tail:(b + 1) * seq, SSD_D_INNER:]
                                         for b in range(nb_p)]))
        outs["conv_s"].append(jnp.concatenate([state_ssd_conv[l], xbc_s], axis=1)[:, -(SSD_CONV - 1):])

        swa_acts = []
        for g, (win, dil) in enumerate(SWA_PATTERN):
            o_g, lse_g, kvt_p[g] = _swa_prompt_call(p_swa, prompt_bias[g], kvt_p[g], l, depth, grp=g, dil=dil,
                                                    nb=nb_p, seq=seq)
            o_g, lse_g, cache_out[g] = _swa_sample_call(p_swa, caches_t[g], sample_bias[g][0], sample_bias[g][1],
                                                        o_g, lse_g, cache_out[g], l, grp=g, nb=nb_s, row0=n_p,
                                                        t_new=t_new)
            swa_acts += [o_g, lse_g]

        y_s5, h_p = _s5_call(p_s5, zero_s5, s5_tab, l, nseq=nb_p, rows=nchunk, scan=True, row0=0, prev=None)
        y_s5, h_s = _s5_call(p_s5, s5_h0_s[l], s5_tab, l, nseq=1, rows=nb_s, scan=False, row0=n_p, prev=y_s5)
        outs["s5_p"].append(h_p)
        outs["s5_s"].append(h_s)

        y_ml, c_p, nn_p, m_p = _mlstm_call(p_ml, p_small, p_small_t, zero_c, zero_vec, zero_vec, 0, wts, l, nb=nb_p,
                                           nc=seq // (ml_q * nsub), q=ml_q, nsub=nsub, valid=ml_q, row0=0,
                                           prev=None)
        y_ml, c_s, nn_s, m_s = _mlstm_call(p_ml, p_small, small_t_s, state_mlstm_c, n_init_s, m_init_s, l, wts, l,
                                           nb=nb_s, nc=1, q=rows_s, nsub=1, valid=t_new, row0=n_p, prev=y_ml)
        for key, val in (("c_p", c_p), ("c_s", c_s), ("n_p", nn_p[:, :, 0]), ("n_s", nn_s[:, :, 0]),
                         ("m_p", m_p[:, :, 0, 0]), ("m_s", m_s[:, :, 0, 0])):
            outs[key].append(val)

        x = _merge_call(x, (y_ssd, *swa_acts, y_s5, p_s5, y_ml), wts, l)

        kv_p = _norm_matmul(mem_prompt.reshape(nb_p * n_mem, D_MODEL), wts["norm_mem"], wts["xa_wkv"], l, n_mem)
        outs["kv_p"].append(kv_p.reshape(nb_p, n_mem, 2, XA_HEADS, XA_HEAD_DIM))
        x_new = _xattn_prompt_call(x, kv_p.reshape(nb_p, n_mem, 2 * D_MODEL), wts, l, n_tiles=n_p // TOKEN_TILE,
                                   tiles_per_seq=seq // TOKEN_TILE)
        x = _xattn_sample_call(x, mem_kv_s, wts, l, nb=nb_s, row0=n_p, prev=x_new)

        if l + 1 < depth:
            x = _ffn_call(x, wts, l)
        else:
            y_p, y_s = _ffn_final_call(x, wts, l, norm_final[None, :], n_p)

    y_prompt = y_p.reshape(nb_p, seq, D_MODEL)
    y_sample = y_s.reshape(nb_s, rows_s, D_MODEL)[:, :t_new]
    st = lambda k: jnp.stack(outs[k])

    def s5_state(k, nb):
        h = st(k).reshape(depth, S5_SUPER, nb, 2, S5_SUPER_GROUPS, S5_STATE)
        return jnp.transpose(h, (0, 2, 1, 4, 5, 3)).reshape(depth, nb, S5_GROUPS, S5_STATE, 2)

    swa_out = []
    for g in range(len(SWA_PATTERN)):
        width_p = kvt_p[g].shape[-1]
        kp = kvt_p[g].reshape(depth, nb_p, 2, SWA_GROUP_HEADS, SWA_HEAD_DIM, width_p)
        swa_out.append(jnp.transpose(kp, (0, 1, 5, 2, 3, 4)))
        cs = cache_out[g].reshape(depth, nb_s, 2, SWA_GROUP_HEADS, SWA_HEAD_DIM, caches[g].shape[2])
        swa_out.append(jnp.transpose(cs, (0, 1, 5, 2, 3, 4)))
    return (y_prompt, y_sample, st("ssd_p"), st("ssd_s"), st("conv_p"), st("conv_s"), *swa_out,
            s5_state("s5_p", nb_p), s5_state("s5_s", nb_s), st("c_p"), st("c_s"),
            st("n_p"), st("n_s"), st("m_p"), st("m_s"), st("kv_p"))
```
